```python
import math
import jax, jax.numpy as jnp
from jax import lax
import numpy as np

D_MODEL = 1024
BATCH = 2
SEQ = 8192
DEPTH = 2
DEC_BATCH = 128
DEC_SEQ = 1
PAST_LEN = 2048
PAGE_SIZE = 128

HEAD_DIM = 64
FOX_HEADS = 8
FOX_W = FOX_HEADS * HEAD_DIM
HG_HEADS = 8
HG_DK = 64
HG_DV = 64
HG_W = HG_HEADS * HG_DV
MIX_W = FOX_W + HG_W
IN_COLS = 3 * FOX_W + FOX_HEADS + 2 * HG_HEADS * HG_DK + 2 * HG_W
HG_CHUNK = 64
Q_BLOCK = 128
N_MEM = 256
X_HEADS = 4
X_HEAD_DIM = 128
X_W = X_HEADS * X_HEAD_DIM
PEER_HEADS = 8
PEER_KEYS = 128
PEER_EXPERTS = PEER_KEYS * PEER_KEYS
PEER_HALF = 128
PEER_TOPK = 16
PEER_BLOCK = 128
EPS = 1e-6

kernel_name = "hymba_fox_hgrn2_peer_decode_step"


def rmsnorm(x, g):
    xf = x.astype(jnp.float32)
    y = xf * lax.rsqrt(jnp.mean(xf * xf, axis=-1, keepdims=True) + EPS)
    return (y * g.astype(jnp.float32)).astype(x.dtype)


def head_rmsnorm(x, g):
    return rmsnorm(x, g.reshape(x.shape[-2], x.shape[-1]))


def project(h, w_in_l, b_f, lb):
    B, T, _ = h.shape
    sizes = (FOX_W, FOX_W, FOX_W, FOX_HEADS, HG_HEADS * HG_DK, HG_HEADS * HG_DK, HG_W, HG_W)
    cuts = np.cumsum(sizes)[:-1].tolist()
    fq, fk, fv, ff, hq, hf, hi, hgate = jnp.split(h @ w_in_l, cuts, axis=-1)
    heads = lambda a, n: a.reshape(B, T, n, a.shape[-1] // n)
    fox_logf = jax.nn.log_sigmoid((ff + b_f).astype(jnp.float32))
    hf32 = hf.astype(jnp.float32)
    lb32 = lb.astype(jnp.float32)
    hg_logf = jnp.logaddexp(jnp.log(lb32), jnp.log1p(-lb32) + jax.nn.log_sigmoid(hf32))
    hg_key = (1.0 - lb32) * jax.nn.sigmoid(-hf32)
    return (heads(fq, FOX_HEADS), heads(fk, FOX_HEADS), heads(fv, FOX_HEADS), fox_logf,
            heads(hq, HG_HEADS), heads(hg_key, HG_HEADS), heads(hi, HG_HEADS),
            heads(hg_logf, HG_HEADS), hgate)


def fox_attend(q, k, v, c_q, c_k, q_pos, k_pos):
    s = jnp.einsum('bqhd,bkhd->bhqk', q, k).astype(jnp.float32) * (HEAD_DIM ** -0.5)
    s = s + jnp.swapaxes(c_q, 1, 2)[..., :, None] - jnp.swapaxes(c_k, 1, 2)[..., None, :]
    s = jnp.where(k_pos[None, :] <= q_pos[:, None], s, -jnp.inf)
    p = jax.nn.softmax(s, axis=-1).astype(v.dtype)
    return jnp.einsum('bhqk,bkhd->bqhd', p, v)


def fox_sweep(q, k, v, c_q, c_k, q_start):
    B, Tq, H, Dh = q.shape
    k_pos = jnp.arange(k.shape[1])
    if Tq <= Q_BLOCK:
        return fox_attend(q, k, v, c_q, c_k, q_start + jnp.arange(Tq), k_pos)
    pad = (-Tq) % Q_BLOCK
    q = jnp.pad(q, ((0, 0), (0, pad), (0, 0), (0, 0)))
    c_q = jnp.pad(c_q, ((0, 0), (0, pad), (0, 0)))
    nb = (Tq + pad) // Q_BLOCK
    qb = q.reshape(B, nb, Q_BLOCK, H, Dh).swapaxes(0, 1)
    cb = c_q.reshape(B, nb, Q_BLOCK, H).swapaxes(0, 1)

    def one(args):
        i, qi, ci = args
        return fox_attend(qi, k, v, ci, c_k, q_start + i * Q_BLOCK + jnp.arange(Q_BLOCK), k_pos)

    out = lax.map(one, (jnp.arange(nb), qb, cb))
    return out.swapaxes(0, 1).reshape(B, nb * Q_BLOCK, H, Dh)[:, :Tq]


def hgrn2_scan(q, k, v, logf, s0):
    B, T, H, _ = q.shape
    C = min(HG_CHUNK, T)
    pad = (-T) % C
    if pad:
        pw = ((0, 0), (0, pad), (0, 0), (0, 0))
        q, k, v, logf = [jnp.pad(a, pw) for a in (q, k, v, logf)]
    n = (T + pad) // C

    def chunks(a):
        return a.reshape(B, n, C, H, a.shape[-1]).transpose(1, 0, 3, 2, 4).astype(jnp.float32)

    qc, kc, vc, fc = chunks(q), chunks(k), chunks(v), chunks(logf)
    causal = jnp.tril(jnp.ones((C, C), dtype=bool))[:, :, None]

    def step(S, inp):
        qi, ki, vi, fi = inp
        b = jnp.cumsum(fi, axis=2)
        o = jnp.einsum('bhcd,bhde->bhce', qi * jnp.exp(b), S)
        diff = b[:, :, :, None, :] - b[:, :, None, :, :]
        decay = jnp.exp(jnp.where(causal, diff, -jnp.inf))
        a = jnp.einsum('bhtd,bhsd,bhtsd->bhts', qi, ki, decay)
        o = o + jnp.einsum('bhts,bhse->bhte', a, vi)
        b_last = b[:, :, -1:, :]
        S_new = jnp.exp(b_last[:, :, 0, :])[..., None] * S + jnp.einsum(
            'bhsd,bhse->bhde', ki * jnp.exp(b_last - b), vi)
        return S_new, o

    S, o = lax.scan(step, s0.astype(jnp.float32), (qc, kc, vc, fc))
    o = o.transpose(1, 0, 3, 2, 4).reshape(B, n * C, H, -1)[:, :T]
    return o, S


def merge(fox_o, hg_o, hgate, fox_gn_l, hg_gn_l, w_out_l):
    B, T = fox_o.shape[:2]
    fo = head_rmsnorm(fox_o, fox_gn_l).reshape(B, T, FOX_W)
    ho = head_rmsnorm(hg_o.astype(fox_o.dtype), hg_gn_l).reshape(B, T, HG_W) * jax.nn.silu(hgate)
    return jnp.concatenate([fo, ho], axis=-1) @ w_out_l


def mem_kv(mem, g, wk, wv):
    B = mem.shape[0]
    m = rmsnorm(mem, g)
    return ((m @ wk).reshape(B, N_MEM, X_HEADS, X_HEAD_DIM),
            (m @ wv).reshape(B, N_MEM, X_HEADS, X_HEAD_DIM))


def cross_attend(h, mk, mv, wq, wo):
    B, T, _ = h.shape
    q = (h @ wq).reshape(B, T, X_HEADS, X_HEAD_DIM)
    s = jnp.einsum('bqhd,bkhd->bhqk', q, mk).astype(jnp.float32) * (X_HEAD_DIM ** -0.5)
    p = jax.nn.softmax(s, axis=-1).astype(mv.dtype)
    return jnp.einsum('bhqk,bkhd->bqhd', p, mv).reshape(B, T, X_W) @ wo


def peer(h, wq, subkeys, u, v):
    B, T, D = h.shape
    n = B * T
    pad = (-n) % PEER_BLOCK
    xb = jnp.pad(h.reshape(n, D), ((0, pad), (0, 0))).reshape(-1, PEER_BLOCK, D)

    def one(xi):
        q = (xi @ wq).reshape(PEER_BLOCK, PEER_HEADS, 2, PEER_HALF)
        s = jnp.einsum('thpc,hpkc->thpk', q, subkeys).astype(jnp.float32)
        sv, si = lax.top_k(s, PEER_TOPK)
        cand = (sv[:, :, 0, :, None] + sv[:, :, 1, None, :]).reshape(PEER_BLOCK, PEER_HEADS, -1)
        cidx = (si[:, :, 0, :, None] * PEER_KEYS + si[:, :, 1, None, :]).reshape(PEER_BLOCK, PEER_HEADS, -1)
        top_v, top_i = lax.top_k(cand, PEER_TOPK)
        eidx = jnp.take_along_axis(cidx, top_i, axis=-1)
        g = jax.nn.softmax(top_v, axis=-1)
        a = jax.nn.gelu(jnp.einsum('td,thkd->thk', xi, u[eidx]).astype(jnp.float32), approximate=False)
        return jnp.einsum('thk,thkd->td', (g * a).astype(xi.dtype), v[eidx])

    out = lax.map(one, xb).reshape(-1, D)[:n]
    return out.reshape(B, T, D)


def setup_inputs(seed: int = 0) -> dict:
    key = jax.random.key(seed)
    ks = jax.random.split(key, 32)
    f32 = jnp.float32
    n_pages = PAST_LEN // PAGE_SIZE
    n_used = DEC_BATCH * n_pages
    n_phys = n_used + max(1, n_used // 4)
    nrm = lambda k, shape, scale=1.0: jax.random.normal(k, shape, f32) * scale
    gain = lambda k, shape: 1.0 + 0.05 * jax.random.normal(k, shape, f32)
    page_table = jax.random.permutation(ks[8], n_phys)[:n_used].reshape(DEC_BATCH, n_pages).astype(jnp.int32)
    return {
        "x_prompt": nrm(ks[0], (BATCH, SEQ, D_MODEL)),
        "x_sample": nrm(ks[1], (DEC_BATCH, DEC_SEQ, D_MODEL)),
        "cache_k": nrm(ks[2], (DEPTH, n_phys, PAGE_SIZE, FOX_HEADS, HEAD_DIM)),
        "cache_v": nrm(ks[3], (DEPTH, n_phys, PAGE_SIZE, FOX_HEADS, HEAD_DIM)),
        "cache_logf": jax.nn.log_sigmoid(2.5 + nrm(ks[4], (DEPTH, n_phys, PAGE_SIZE, FOX_HEADS))),
        "state_hg": nrm(ks[5], (DEPTH, DEC_BATCH, HG_HEADS, HG_DK, HG_DV), 0.3),
        "cache_mem_k": nrm(ks[6], (DEPTH, DEC_BATCH, N_MEM, X_HEADS, X_HEAD_DIM)),
        "cache_mem_v": nrm(ks[7], (DEPTH, DEC_BATCH, N_MEM, X_HEADS, X_HEAD_DIM)),
        "page_table": page_table,
        "mem_prompt": nrm(ks[9], (BATCH, N_MEM, D_MODEL)),
        "ln_mix": gain(ks[10], (DEPTH, D_MODEL)),
        "w_in": nrm(ks[11], (DEPTH, D_MODEL, IN_COLS), D_MODEL ** -0.5),
        "b_fox_f": jax.random.uniform(ks[12], (DEPTH, FOX_HEADS), f32, 1.0, 4.0),
        "hg_lb": nrm(ks[13], (DEPTH, HG_HEADS * HG_DK), 0.5),
        "fox_gn": gain(ks[14], (DEPTH, FOX_W)),
        "hg_gn": gain(ks[15], (DEPTH, HG_W)),
        "w_out": nrm(ks[16], (DEPTH, MIX_W, D_MODEL), MIX_W ** -0.5),
        "ln_x": gain(ks[17], (DEPTH, D_MODEL)),
        "ln_mem": gain(ks[18], (DEPTH, D_MODEL)),
        "w_xq": nrm(ks[19], (DEPTH, D_MODEL, X_W), D_MODEL ** -0.5),
        "w_xk": nrm(ks[20], (DEPTH, D_MODEL, X_W), D_MODEL ** -0.5),
        "w_xv": nrm(ks[21], (DEPTH, D_MODEL, X_W), D_MODEL ** -0.5),
        "w_xo": nrm(ks[22], (DEPTH, X_W, D_MODEL), X_W ** -0.5),
        "ln_ffn": gain(ks[23], (DEPTH, D_MODEL)),
        "peer_wq": nrm(ks[24], (DEPTH, D_MODEL, PEER_HEADS * 2 * PEER_HALF), D_MODEL ** -0.5),
        "peer_subkeys": nrm(ks[25], (DEPTH, PEER_HEADS, 2, PEER_KEYS, PEER_HALF), PEER_HALF ** -0.5),
        "peer_u": nrm(ks[26], (DEPTH, PEER_EXPERTS, D_MODEL), D_MODEL ** -0.5),
        "peer_v": nrm(ks[27], (DEPTH, PEER_EXPERTS, D_MODEL), 0.5 * PEER_HEADS ** -0.5),
        "ln_final": gain(ks[28], (D_MODEL,)),
    }


def reference(x_prompt, x_sample, cache_k, cache_v, cache_logf, state_hg, cache_mem_k, cache_mem_v,
              page_table, mem_prompt, ln_mix, w_in, b_fox_f, hg_lb, fox_gn, hg_gn, w_out, ln_x, ln_mem,
              w_xq, w_xk, w_xv, w_xo, ln_ffn, peer_wq, peer_subkeys, peer_u, peer_v, ln_final):
    lb_all = jnp.cumsum(jax.nn.softmax(hg_lb.astype(jnp.float32), axis=0), axis=0)
    n_pages = PAST_LEN // PAGE_SIZE
    xp, xs = x_prompt, x_sample
    Bp, Bs = xp.shape[0], xs.shape[0]
    kp_l, vp_l, fp_l, hp_l, mkp_l, mvp_l = [], [], [], [], [], []
    ks_l, vs_l, fs_l, hs_l = [], [], [], []
    for l in range(DEPTH):
        lb = lb_all[l] - lb_all[0]
        h = rmsnorm(xp, ln_mix[l])
        fq, fk, fv, flogf, hq, hk, hv, hlogf, hgate = project(h, w_in[l], b_fox_f[l], lb)
        c = jnp.cumsum(flogf, axis=1)
        fo = fox_sweep(fq, fk, fv, c, c, 0)
        s0 = jnp.zeros((Bp, HG_HEADS, HG_DK, HG_DV), jnp.float32)
        ho, S_p = hgrn2_scan(hq, hk, hv, hlogf, s0)
        xp = xp + merge(fo, ho, hgate, fox_gn[l], hg_gn[l], w_out[l])
        mk, mv = mem_kv(mem_prompt, ln_mem[l], w_xk[l], w_xv[l])
        xp = xp + cross_attend(rmsnorm(xp, ln_x[l]), mk, mv, w_xq[l], w_xo[l])
        xp = xp + peer(rmsnorm(xp, ln_ffn[l]), peer_wq[l], peer_subkeys[l], peer_u[l], peer_v[l])
        kp_l.append(fk); vp_l.append(fv); fp_l.append(flogf.astype(cache_logf.dtype))
        hp_l.append(S_p.astype(state_hg.dtype)); mkp_l.append(mk); mvp_l.append(mv)
        h = rmsnorm(xs, ln_mix[l])
        sq, sk, sv, slogf, tq, tk, tv, tlogf, tgate = project(h, w_in[l], b_fox_f[l], lb)
        k_past = cache_k[l][page_table].reshape(Bs, n_pages * PAGE_SIZE, FOX_HEADS, HEAD_DIM)
        v_past = cache_v[l][page_table].reshape(Bs, n_pages * PAGE_SIZE, FOX_HEADS, HEAD_DIM)
        f_past = cache_logf[l][page_table].reshape(Bs, n_pages * PAGE_SIZE, FOX_HEADS).astype(jnp.float32)
        k_all = jnp.concatenate([k_past, sk.astype(k_past.dtype)], axis=1)
        v_all = jnp.concatenate([v_past, sv.astype(v_past.dtype)], axis=1)
        c_all = jnp.cumsum(jnp.concatenate([f_past, slogf], axis=1), axis=1)
        fo_s = fox_sweep(sq, k_all, v_all, c_all[:, PAST_LEN:], c_all, PAST_LEN)
        ho_s, S_s = hgrn2_scan(tq, tk, tv, tlogf, state_hg[l])
        xs = xs + merge(fo_s, ho_s, tgate, fox_gn[l], hg_gn[l], w_out[l])
        xs = xs + cross_attend(rmsnorm(xs, ln_x[l]), cache_mem_k[l], cache_mem_v[l], w_xq[l], w_xo[l])
        xs = xs + peer(rmsnorm(xs, ln_ffn[l]), peer_wq[l], peer_subkeys[l], peer_u[l], peer_v[l])
        ks_l.append(sk); vs_l.append(sv); fs_l.append(slogf.astype(cache_logf.dtype))
        hs_l.append(S_s.astype(state_hg.dtype))
    y_prompt = rmsnorm(xp, ln_final)
    y_sample = rmsnorm(xs, ln_final)
    new_k_prompt = jnp.stack(kp_l)
    new_v_prompt = jnp.stack(vp_l)
    new_logf_prompt = jnp.stack(fp_l)
    new_hg_prompt = jnp.stack(hp_l)
    new_mem_k_prompt = jnp.stack(mkp_l)
    new_mem_v_prompt = jnp.stack(mvp_l)
    new_k_sample = jnp.stack(ks_l)
    new_v_sample = jnp.stack(vs_l)
    new_logf_sample = jnp.stack(fs_l)
    new_hg_sample = jnp.stack(hs_l)
    return (y_prompt, y_sample, new_k_prompt, new_v_prompt, new_logf_prompt, new_hg_prompt,
            new_mem_k_prompt, new_mem_v_prompt, new_k_sample, new_v_sample, new_logf_sample, new_hg_sample)
```

```python
import functools
import math

import jax
import jax.numpy as jnp
from jax import lax
from jax.experimental import pallas as pl
from jax.experimental.pallas import tpu as pltpu

F32 = jnp.float32
BF16 = jnp.bfloat16
I32 = jnp.int32
EPS = 1e-6
HIGHEST = lax.Precision.HIGHEST
NEG_INF = float("-inf")

HEAD_DIM = 64
HG_DK = 64
HG_CHUNK = 64
X_HEAD_DIM = 128
PEER_TOPK = 16
LANES = 128
VMEM_LIMIT = 56 * 1024 * 1024

_NT = (((1,), (1,)), ((), ()))
_TN = (((0,), (0,)), ((), ()))


def _params(*sem):
    return pltpu.CompilerParams(dimension_semantics=sem, vmem_limit_bytes=VMEM_LIMIT)


def _rms(x, g):
    ms = jnp.mean(x * x, axis=-1, keepdims=True)
    return x * lax.rsqrt(ms + EPS) * g


def _log_sigmoid(x):
    return jnp.minimum(x, 0.0) - jnp.log1p(jnp.exp(-jnp.abs(x)))


def _dot(a, b):
    return jnp.dot(a, b, preferred_element_type=F32)


def _dot_exact(a, b):
    return jnp.dot(a, b, precision=HIGHEST, preferred_element_type=F32)


def _iota(shape, axis):
    return lax.broadcasted_iota(I32, shape, axis)


def _inproj_kernel(x_ref, g_ref, w_ref, wff_ref, bf_ref, lb_ref,
                   fq_ref, fk_ref, fv_ref, lf_ref, hq_ref, hk_ref, hv_ref, hlf_ref, gate_ref, *, fox_heads):
    h = _rms(x_ref[...], g_ref[...]).astype(BF16)
    w = w_ref.shape[1] // 7

    def mm(j):
        return _dot(h, w_ref[:, j * w:(j + 1) * w])

    fq_ref[...] = mm(0)
    fk_ref[...] = mm(1)
    fv_ref[...] = mm(2)
    ff = _dot(h, wff_ref[...])
    lf_ref[...] = _log_sigmoid(ff + bf_ref[...])[:, :fox_heads]
    hq_ref[...] = mm(3)
    z = mm(4)
    lb = lb_ref[...]
    a = jnp.log(lb)
    b = jnp.log1p(-lb) + _log_sigmoid(z)
    hlf_ref[...] = jnp.maximum(a, b) + jnp.log1p(jnp.exp(-jnp.abs(a - b)))
    hk_ref[...] = (1.0 - lb) * (1.0 / (1.0 + jnp.exp(z)))
    hv_ref[...] = mm(5)
    gate_ref[...] = mm(6)


def _inproj(x, g, w_main, w_ff, b_f, lb, tm, fox_heads):
    n, d = x.shape
    w = w_main.shape[1] // 7
    full = lambda a: pl.BlockSpec(a.shape, lambda i: (0,) * a.ndim)
    row = lambda c: pl.BlockSpec((tm, c), lambda i: (i, 0))
    outs = [jax.ShapeDtypeStruct((n, w), F32)] * 3 + [jax.ShapeDtypeStruct((n, fox_heads), F32)] + \
           [jax.ShapeDtypeStruct((n, w), F32)] * 5
    out_specs = [row(w)] * 3 + [row(fox_heads)] + [row(w)] * 5
    return pl.pallas_call(
        functools.partial(_inproj_kernel, fox_heads=fox_heads),
        grid=(n // tm,),
        in_specs=[row(d), full(g), full(w_main), full(w_ff), full(b_f), full(lb)],
        out_specs=out_specs, out_shape=outs,
        compiler_params=_params("parallel"), name="inproj",
    )(x, g, w_main, w_ff, b_f, lb)


def _cumsum_kernel(x_ref, o_ref, carry_ref):
    @pl.when(pl.program_id(0) == 0)
    def _():
        carry_ref[...] = jnp.zeros_like(carry_ref)

    x = x_ref[...]
    n = x.shape[1]
    tri = (_iota((n, n), 0) <= _iota((n, n), 1)).astype(F32)
    c = _dot_exact(x, tri) + carry_ref[:, :1]
    o_ref[...] = c
    carry_ref[...] = jnp.broadcast_to(c[:, n - 1:n], carry_ref.shape)


def _cumsum_lanes(x, tc):
    r, t = x.shape
    return pl.pallas_call(
        _cumsum_kernel, grid=(t // tc,),
        in_specs=[pl.BlockSpec((r, tc), lambda i: (0, i))],
        out_specs=pl.BlockSpec((r, tc), lambda i: (0, i)),
        out_shape=jax.ShapeDtypeStruct((r, t), F32),
        scratch_shapes=[pltpu.VMEM((r, LANES), F32)],
        compiler_params=_params("arbitrary"), name="logf_cumsum",
    )(x)


def _fox_kernel(q_ref, k_ref, v_ref, cq_ref, ck_ref, o_ref, m_ref, l_ref, acc_ref, *, scale):
    qi = pl.program_id(2)
    ki = pl.program_id(3)
    tq = q_ref.shape[0]
    tk = k_ref.shape[0]

    @pl.when(ki == 0)
    def _():
        m_ref[...] = jnp.full_like(m_ref, NEG_INF)
        l_ref[...] = jnp.zeros_like(l_ref)
        acc_ref[...] = jnp.zeros_like(acc_ref)

    @pl.when(ki <= qi)
    def _():
        q = q_ref[...] * scale
        k = k_ref[...].astype(BF16)
        v = v_ref[...].astype(BF16)
        lane = _iota((1, LANES), 1)
        row = qi * tq + _iota((tq, 1), 0)
        col = ki * tk + _iota((1, tk), 1)
        causal = col <= row
        alphas, pvs = [], []
        for hh in range(2):
            in_head = (lane // HEAD_DIM) == hh
            qh = jnp.where(in_head, q, 0.0).astype(BF16)
            s = lax.dot_general(qh, k, _NT, preferred_element_type=F32)
            s = s + cq_ref[:, hh:hh + 1] - ck_ref[hh:hh + 1, :]
            s = jnp.where(causal, s, NEG_INF)
            m_prev = m_ref[hh]
            m_new = jnp.maximum(m_prev, jnp.max(s, axis=1, keepdims=True))
            p = jnp.exp(s - m_new)
            alpha = jnp.exp(m_prev - m_new)
            l_ref[hh] = alpha * l_ref[hh] + jnp.sum(p, axis=1, keepdims=True)
            m_ref[hh] = m_new
            alphas.append(alpha)
            pvs.append(_dot(p.astype(BF16), v))
        first = lane < HEAD_DIM
        acc_ref[...] = jnp.where(first, alphas[0], alphas[1]) * acc_ref[...] + jnp.where(first, pvs[0], pvs[1])

    @pl.when(ki == qi)
    def _():
        first = _iota((1, LANES), 1) < HEAD_DIM
        o_ref[...] = acc_ref[...] / jnp.where(first, l_ref[0], l_ref[1])


def _fox_prompt(q, k, v, cq, ck, batch, tq):
    n, w = q.shape
    t = n // batch
    pairs = w // LANES
    nq = t // tq
    kern = functools.partial(_fox_kernel, scale=HEAD_DIM ** -0.5)
    return pl.pallas_call(
        kern, grid=(batch, pairs, nq, nq),
        in_specs=[
            pl.BlockSpec((tq, LANES), lambda b, g, i, j: (b * nq + i, g)),
            pl.BlockSpec((tq, LANES), lambda b, g, i, j: (b * nq + jnp.minimum(i, j), g)),
            pl.BlockSpec((tq, LANES), lambda b, g, i, j: (b * nq + jnp.minimum(i, j), g)),
            pl.BlockSpec((None, tq, 2), lambda b, g, i, j: (g, b * nq + i, 0)),
            pl.BlockSpec((None, 2, tq), lambda b, g, i, j: (b * pairs + g, 0, jnp.minimum(i, j))),
        ],
        out_specs=pl.BlockSpec((tq, LANES), lambda b, g, i, j: (b * nq + i, g)),
        out_shape=jax.ShapeDtypeStruct((n, w), F32),
        scratch_shapes=[pltpu.VMEM((2, tq, 1), F32), pltpu.VMEM((2, tq, 1), F32), pltpu.VMEM((tq, LANES), F32)],
        compiler_params=_params("parallel", "parallel", "parallel", "arbitrary"), name="fox_prompt",
    )(q, k, v, cq, ck)


def _hgrn_kernel(q_ref, k_ref, v_ref, lf_ref, o_ref, st_ref, st_sc):
    n = pl.program_id(1)
    c = q_ref.shape[0]
    pairs = q_ref.shape[1] // LANES

    @pl.when(n == 0)
    def _():
        st_sc[...] = jnp.zeros_like(st_sc)

    tril = (_iota((c, c), 1) <= _iota((c, c), 0)).astype(F32)
    same_head = (_iota((LANES, LANES), 0) // HG_DK) == (_iota((LANES, LANES), 1) // HG_DK)
    seg = same_head.astype(BF16)
    ti = _iota((c, c, LANES), 0)
    si = _iota((c, c, LANES), 1)
    for g in range(pairs):
        sl = slice(g * LANES, (g + 1) * LANES)
        q = q_ref[:, sl]
        k = k_ref[:, sl]
        v = v_ref[:, sl]
        b = _dot_exact(tril, lf_ref[:, sl])
        st = st_sc[g]
        qe = (q * jnp.exp(b)).astype(BF16)
        o = lax.dot_general(qe, st.astype(BF16), _NT, preferred_element_type=F32)
        decay = jnp.exp(jnp.where(si <= ti, b[:, None, :] - b[None, :, :], NEG_INF))
        p = (q[:, None, :] * k[None, :, :]) * decay
        a = _dot(p.reshape(c * c, LANES).astype(BF16), seg)
        o = o + jnp.sum(a.reshape(c, c, LANES) * v[None, :, :], axis=1)
        o_ref[:, sl] = o
        b_last = b[c - 1:c, :]
        kd = (k * jnp.exp(b_last - b)).astype(BF16)
        upd = lax.dot_general(v.astype(BF16), kd, _TN, preferred_element_type=F32)
        st_sc[g] = st * jnp.exp(b_last) + jnp.where(same_head, upd, 0.0)

    @pl.when(n == pl.num_programs(1) - 1)
    def _():
        st_ref[...] = st_sc[...]


def _hgrn_prompt(q, k, v, lf, batch):
    n, w = q.shape
    t = n // batch
    nc = t // HG_CHUNK
    pairs = w // LANES
    blk = pl.BlockSpec((HG_CHUNK, w), lambda b, i: (b * nc + i, 0))
    return pl.pallas_call(
        _hgrn_kernel, grid=(batch, nc),
        in_specs=[blk] * 4,
        out_specs=[blk, pl.BlockSpec((None, pairs, LANES, LANES), lambda b, i: (b, 0, 0, 0))],
        out_shape=[jax.ShapeDtypeStruct((n, w), F32), jax.ShapeDtypeStruct((batch, pairs, LANES, LANES), F32)],
        scratch_shapes=[pltpu.VMEM((pairs, LANES, LANES), F32)],
        compiler_params=_params("parallel", "arbitrary"), name="hgrn_prompt",
    )(q, k, v, lf)


def _head_rows(x_row, heads, width):
    w = x_row.shape[1]
    keep = (_iota((8, w), 1) // width) == _iota((8, w), 0)
    return jnp.where(keep, jnp.broadcast_to(x_row, (8, w)), 0.0)


def _expand_matrix(heads_pad, w, width, dtype):
    return ((_iota((heads_pad, w), 1) // width) == _iota((heads_pad, w), 0)).astype(dtype)


def _fox_decode_kernel(pt_ref, q_ref, kn_ref, vn_ref, lfn_ref, k_ref, v_ref, lf_ref, o_ref,
                       m_ref, l_ref, acc_ref, carry_ref, *, scale):
    p = pl.program_id(1)
    nh = lf_ref.shape[1]
    w = q_ref.shape[1]
    ps = k_ref.shape[0]

    @pl.when(p == 0)
    def _():
        m_ref[...] = jnp.full_like(m_ref, NEG_INF)
        l_ref[...] = jnp.zeros_like(l_ref)
        acc_ref[...] = jnp.zeros_like(acc_ref)
        carry_ref[...] = jnp.zeros_like(carry_ref)

    qrows = _head_rows(q_ref[...] * scale, nh, HEAD_DIM).astype(BF16)
    expand = _expand_matrix(nh, w, HEAD_DIM, F32)

    def update(s, vals):
        m_prev = m_ref[...]
        m_new = jnp.maximum(m_prev, jnp.max(s, axis=0, keepdims=True))
        pr = jnp.exp(s - m_new)
        alpha = jnp.exp(m_prev - m_new)
        l_ref[...] = alpha * l_ref[...] + jnp.sum(pr, axis=0, keepdims=True)
        m_ref[...] = m_new
        pe = _dot(pr.astype(BF16), expand.astype(BF16))
        acc_ref[...] = acc_ref[...] * _dot_exact(alpha, expand) + jnp.sum(pe * vals, axis=0, keepdims=True)

    s = lax.dot_general(k_ref[...].astype(BF16), qrows, _NT, preferred_element_type=F32)
    tril = (_iota((ps, ps), 1) <= _iota((ps, ps), 0)).astype(F32)
    lf = lf_ref[...]
    prefix = _dot_exact(tril, lf) + carry_ref[...]
    update(s - prefix, v_ref[...])
    carry_ref[...] = prefix[ps - 1:ps, :]

    @pl.when(p == pl.num_programs(1) - 1)
    def _():
        kn = jnp.broadcast_to(kn_ref[...], (8, w)).astype(BF16)
        s_new = lax.dot_general(kn, qrows, _NT, preferred_element_type=F32)[:1, :]
        update(s_new - (carry_ref[...] + lfn_ref[...]), vn_ref[...])
        o_ref[...] = acc_ref[...] / _dot_exact(l_ref[...], expand)


def _fox_decode(page_table, q, k_new, v_new, lf_new, cache_k, cache_v, cache_lf, layer):
    r, _, w = q.shape
    nh = lf_new.shape[2]
    n_pages = page_table.shape[1]
    ps = cache_k.shape[2]
    pt = page_table.reshape(-1)
    tok = lambda c: pl.BlockSpec((None, 1, c), lambda i, p, pt: (i, 0, 0))
    page = lambda c: pl.BlockSpec((None, None, ps, c), lambda i, p, pt: (layer, pt[i * n_pages + p], 0, 0))
    grid_spec = pltpu.PrefetchScalarGridSpec(
        num_scalar_prefetch=1, grid=(r, n_pages),
        in_specs=[tok(w), tok(w), tok(w), tok(nh), page(w), page(w), page(nh)],
        out_specs=tok(w),
        scratch_shapes=[pltpu.VMEM((1, nh), F32), pltpu.VMEM((1, nh), F32), pltpu.VMEM((1, w), F32),
                        pltpu.VMEM((1, nh), F32)],
    )
    return pl.pallas_call(
        functools.partial(_fox_decode_kernel, scale=HEAD_DIM ** -0.5),
        grid_spec=grid_spec, out_shape=jax.ShapeDtypeStruct((r, 1, w), F32),
        compiler_params=_params("parallel", "arbitrary"), name="fox_decode",
    )(pt, q, k_new, v_new, lf_new, cache_k, cache_v, cache_lf)


def _hgrn_step_kernel(q_ref, k_ref, lf_ref, v_ref, s_ref, o_ref, so_ref):
    q = q_ref[...]
    k = k_ref[...]
    f = jnp.exp(lf_ref[...])
    v = v_ref[...]
    s = s_ref[...]
    o_ref[...] = jnp.sum((q * f) * s, axis=2, keepdims=True) + jnp.sum(q * k, axis=2, keepdims=True) * v
    so_ref[...] = f * s + k * v


def _hgrn_step(q, k, lf, v, state, layer, rb):
    r, h, dk, _ = q.shape
    dv = v.shape[3]
    col = pl.BlockSpec((rb, h, dk, 1), lambda i: (i, 0, 0, 0))
    rowv = pl.BlockSpec((rb, h, 1, dv), lambda i: (i, 0, 0, 0))
    return pl.pallas_call(
        _hgrn_step_kernel, grid=(r // rb,),
        in_specs=[col, col, col, rowv, pl.BlockSpec((None, rb, h, dk, dv), lambda i: (layer, i, 0, 0, 0))],
        out_specs=[rowv, pl.BlockSpec((rb, h, dk, dv), lambda i: (i, 0, 0, 0))],
        out_shape=[jax.ShapeDtypeStruct((r, h, 1, dv), F32), jax.ShapeDtypeStruct((r, h, dk, dv), F32)],
        compiler_params=_params("parallel"), name="hgrn_step",
    )(q, k, lf, v, state)


def _head_mean_sq(y, width):
    w = y.shape[1]
    seg = ((_iota((w, w), 0) // width) == (_iota((w, w), 1) // width)).astype(BF16)
    sq = y * y
    hi = sq.astype(BF16)
    lo = (sq - hi.astype(F32)).astype(BF16)
    return (_dot(hi, seg) + _dot(lo, seg)) * (1.0 / width)


def _merge_kernel(fo_ref, ho_ref, gate_ref, x_ref, fgn_ref, hgn_ref, wo_ref, lnx_ref, wxq_ref, x1_ref, qx_ref):
    fo = fo_ref[...]
    ho = ho_ref[...]
    fw = fo.shape[1]
    fn = fo * lax.rsqrt(_head_mean_sq(fo, HEAD_DIM) + EPS) * fgn_ref[...]
    gate = gate_ref[...]
    hn = ho * lax.rsqrt(_head_mean_sq(ho, HEAD_DIM) + EPS) * hgn_ref[...] * (gate / (1.0 + jnp.exp(-gate)))
    y = _dot(fn.astype(BF16), wo_ref[:fw, :]) + _dot(hn.astype(BF16), wo_ref[fw:, :])
    x1 = x_ref[...] + y
    x1_ref[...] = x1
    qx_ref[...] = _dot(_rms(x1, lnx_ref[...]).astype(BF16), wxq_ref[...])


def _merge(fo, ho, gate, x, fgn, hgn, w_out, ln_x, w_xq, tm):
    n, d = x.shape
    xw = w_xq.shape[1]
    full = lambda a: pl.BlockSpec(a.shape, lambda i: (0,) * a.ndim)
    row = lambda c: pl.BlockSpec((tm, c), lambda i: (i, 0))
    return pl.pallas_call(
        _merge_kernel, grid=(n // tm,),
        in_specs=[row(fo.shape[1]), row(ho.shape[1]), row(gate.shape[1]), row(d),
                  full(fgn), full(hgn), full(w_out), full(ln_x), full(w_xq)],
        out_specs=[row(d), row(xw)],
        out_shape=[jax.ShapeDtypeStruct((n, d), F32), jax.ShapeDtypeStruct((n, xw), F32)],
        compiler_params=_params("parallel"), name="merge",
    )(fo, ho, gate, x, fgn, hgn, w_out, ln_x, w_xq)


def _memkv_kernel(m_ref, g_ref, wk_ref, wv_ref, k_ref, v_ref):
    m = _rms(m_ref[...], g_ref[...]).astype(BF16)
    k_ref[...] = _dot(m, wk_ref[...])
    v_ref[...] = _dot(m, wv_ref[...])


def _memkv(mem, g, wk, wv, tm):
    n, d = mem.shape
    xw = wk.shape[1]
    full = lambda a: pl.BlockSpec(a.shape, lambda i: (0,) * a.ndim)
    row = lambda c: pl.BlockSpec((tm, c), lambda i: (i, 0))
    return pl.pallas_call(
        _memkv_kernel, grid=(n // tm,),
        in_specs=[row(d), full(g), full(wk), full(wv)],
        out_specs=[row(xw), row(xw)],
        out_shape=[jax.ShapeDtypeStruct((n, xw), F32)] * 2,
        compiler_params=_params("parallel"), name="memkv",
    )(mem, g, wk, wv)


def _xattn_prompt_kernel(q_ref, mk_ref, mv_ref, o_ref, *, scale):
    heads = q_ref.shape[1] // X_HEAD_DIM
    for h in range(heads):
        sl = slice(h * X_HEAD_DIM, (h + 1) * X_HEAD_DIM)
        q = (q_ref[:, sl] * scale).astype(BF16)
        s = lax.dot_general(q, mk_ref[:, sl].astype(BF16), _NT, preferred_element_type=F32)
        e = jnp.exp(s - jnp.max(s, axis=1, keepdims=True))
        p = e / jnp.sum(e, axis=1, keepdims=True)
        o_ref[:, sl] = _dot(p.astype(BF16), mv_ref[:, sl].astype(BF16))


def _xattn_prompt(q, mk, mv, batch, tm):
    n, xw = q.shape
    t = n // batch
    nm = mk.shape[0] // batch
    nt = t // tm
    return pl.pallas_call(
        functools.partial(_xattn_prompt_kernel, scale=X_HEAD_DIM ** -0.5), grid=(batch, nt),
        in_specs=[pl.BlockSpec((tm, xw), lambda b, i: (b * nt + i, 0)),
                  pl.BlockSpec((nm, xw), lambda b, i: (b, 0)),
                  pl.BlockSpec((nm, xw), lambda b, i: (b, 0))],
        out_specs=pl.BlockSpec((tm, xw), lambda b, i: (b * nt + i, 0)),
        out_shape=jax.ShapeDtypeStruct((n, xw), F32),
        compiler_params=_params("parallel", "parallel"), name="xattn_prompt",
    )(q, mk, mv)


def _xattn_sample_kernel(q_ref, mk_ref, mv_ref, o_ref, *, scale):
    rb, _, w = q_ref.shape
    heads = w // X_HEAD_DIM
    expand = _expand_matrix(8, w, X_HEAD_DIM, BF16)
    valid = _iota((1, 8), 1) < heads
    for r in range(rb):
        qrows = _head_rows(q_ref[r] * scale, heads, X_HEAD_DIM).astype(BF16)
        s = lax.dot_general(mk_ref[r].astype(BF16), qrows, _NT, preferred_element_type=F32)
        e = jnp.exp(s - jnp.max(s, axis=0, keepdims=True))
        p = jnp.where(valid, e / jnp.sum(e, axis=0, keepdims=True), 0.0)
        pe = _dot(p.astype(BF16), expand)
        o_ref[r] = jnp.sum(pe * mv_ref[r], axis=0, keepdims=True)


def _xattn_sample(q, mem_k, mem_v, layer, rb):
    r, _, w = q.shape
    nm = mem_k.shape[2]
    tok = pl.BlockSpec((rb, 1, w), lambda i: (i, 0, 0))
    mem = pl.BlockSpec((None, rb, nm, w), lambda i: (layer, i, 0, 0))
    return pl.pallas_call(
        functools.partial(_xattn_sample_kernel, scale=X_HEAD_DIM ** -0.5), grid=(r // rb,),
        in_specs=[tok, mem, mem], out_specs=tok,
        out_shape=jax.ShapeDtypeStruct((r, 1, w), F32),
        compiler_params=_params("parallel"), name="xattn_sample",
    )(q, mem_k, mem_v)


def _xo_peerq_kernel(x1_ref, ctx_ref, wxo_ref, lnf_ref, wq_ref, x2_ref, h_ref, pq_ref):
    x2 = x1_ref[...] + _dot(ctx_ref[...].astype(BF16), wxo_ref[...])
    x2_ref[...] = x2
    h = _rms(x2, lnf_ref[...]).astype(BF16)
    h_ref[...] = h
    pq_ref[...] = _dot(h, wq_ref[...])


def _xo_peerq(x1, ctx, w_xo, ln_ffn, peer_wq, tm):
    n, d = x1.shape
    qw = peer_wq.shape[1]
    full = lambda a: pl.BlockSpec(a.shape, lambda i: (0,) * a.ndim)
    row = lambda c: pl.BlockSpec((tm, c), lambda i: (i, 0))
    return pl.pallas_call(
        _xo_peerq_kernel, grid=(n // tm,),
        in_specs=[row(d), row(ctx.shape[1]), full(w_xo), full(ln_ffn), full(peer_wq)],
        out_specs=[row(d), row(d), row(qw)],
        out_shape=[jax.ShapeDtypeStruct((n, d), F32), jax.ShapeDtypeStruct((n, d), BF16),
                   jax.ShapeDtypeStruct((n, qw), F32)],
        compiler_params=_params("parallel"), name="xo_peerq",
    )(x1, ctx, w_xo, ln_ffn, peer_wq)


def _top16(s):
    kk = s.shape[0]
    kio = _iota(s.shape, 0)
    vals, idxs = [], []
    for _ in range(PEER_TOPK):
        m = jnp.max(s, axis=0, keepdims=True)
        idx = jnp.min(jnp.where(s == m, kio, kk), axis=0, keepdims=True)
        s = jnp.where(kio == idx, NEG_INF, s)
        vals.append(m)
        idxs.append(idx)
    return jnp.concatenate(vals, axis=0), jnp.concatenate(idxs, axis=0)


def _route_kernel(pq_ref, sk_ref, i_ref, j_ref, g_ref):
    half = sk_ref.shape[2]
    tt = pq_ref.shape[0]
    sv, si = [], []
    for p in range(2):
        q = pq_ref[:, p * half:(p + 1) * half].astype(BF16)
        st = lax.dot_general(sk_ref[p], q, _NT, preferred_element_type=F32)
        v, i = _top16(st)
        sv.append(v)
        si.append(i)
    k = PEER_TOPK
    cand = jnp.concatenate([sv[0][a:a + 1, :] + sv[1] for a in range(k)], axis=0)
    ic = jnp.concatenate([jnp.broadcast_to(si[0][a:a + 1, :], (k, tt)) for a in range(k)], axis=0)
    jc = jnp.concatenate([si[1]] * k, axis=0)
    pio = _iota(cand.shape, 0)
    tv, ti, tj = [], [], []
    for _ in range(k):
        m = jnp.max(cand, axis=0, keepdims=True)
        pos = jnp.min(jnp.where(cand == m, pio, k * k), axis=0, keepdims=True)
        sel = pio == pos
        ti.append(jnp.sum(jnp.where(sel, ic, 0), axis=0, keepdims=True))
        tj.append(jnp.sum(jnp.where(sel, jc, 0), axis=0, keepdims=True))
        cand = jnp.where(sel, NEG_INF, cand)
        tv.append(m)
    top = jnp.concatenate(tv, axis=0)
    e = jnp.exp(top - top[0:1, :])
    g_ref[...] = e / jnp.sum(e, axis=0, keepdims=True)
    i_ref[...] = jnp.concatenate(ti, axis=0)
    j_ref[...] = jnp.concatenate(tj, axis=0)


def _route(pq, subkeys, tt):
    n = pq.shape[0]
    hp, keys, half = subkeys.shape
    heads = hp // 2
    k = PEER_TOPK
    out = pl.BlockSpec((k, tt), lambda i, h: (h, i))
    return pl.pallas_call(
        _route_kernel, grid=(n // tt, heads),
        in_specs=[pl.BlockSpec((tt, 2 * half), lambda i, h: (i, h)),
                  pl.BlockSpec((2, keys, half), lambda i, h: (h, 0, 0))],
        out_specs=[out, out, out],
        out_shape=[jax.ShapeDtypeStruct((heads * k, n), I32), jax.ShapeDtypeStruct((heads * k, n), I32),
                   jax.ShapeDtypeStruct((heads * k, n), F32)],
        compiler_params=_params("parallel", "parallel"), name="peer_route",
    )(pq, subkeys)


def _peer_u_kernel(h_ref, u_ref, i_ref, j_ref, a_ref, *, ib):
    step = pl.program_id(1)

    @pl.when(step == 0)
    def _():
        a_ref[...] = jnp.zeros_like(a_ref)

    h = h_ref[...]
    isel = i_ref[...]
    jsel = j_ref[...]
    acc = a_ref[...]
    nj = jsel.shape[1]
    for ii in range(ib):
        a = lax.dot_general(h, u_ref[ii * nj:(ii + 1) * nj, :], _NT, preferred_element_type=F32)
        acc = jnp.where(isel == step * ib + ii, jnp.take_along_axis(a, jsel, axis=1), acc)
    a_ref[...] = acc


def _peer_u(h, u, isel, jsel, tt, ib):
    n, d = h.shape
    slots = isel.shape[1]
    nblk = u.shape[0] // (slots * ib)
    tok = lambda c: pl.BlockSpec((tt, c), lambda t, e: (t, 0))
    return pl.pallas_call(
        functools.partial(_peer_u_kernel, ib=ib), grid=(n // tt, nblk),
        in_specs=[tok(d), pl.BlockSpec((slots * ib, d), lambda t, e: (e, 0)), tok(slots), tok(slots)],
        out_specs=tok(slots), out_shape=jax.ShapeDtypeStruct((n, slots), F32),
        compiler_params=_params("parallel", "arbitrary"), name="peer_u",
    )(h, u, isel, jsel)


def _peer_v_kernel(i_ref, j_ref, g_ref, a_ref, x_ref, v_ref, o_ref, z_sc, zs_sc, *, ib, stride):
    step = pl.program_id(1)
    tt, slots = i_ref.shape

    @pl.when(step == 0)
    def _():
        a = a_ref[...]
        z_sc[...] = g_ref[...] * (0.5 * a * (1.0 + lax.erf(a * (2.0 ** -0.5))))

        def scatter(t, carry):
            irow = i_ref[pl.ds(t, 1), :]
            jrow = j_ref[pl.ds(t, 1), :]
            zrow = z_sc[pl.ds(t, 1), :]
            io = _iota((slots, slots), 0)
            zit = jnp.where(irow == io, zrow, 0.0).astype(BF16)
            oht = jnp.where(jrow == io, 1.0, 0.0).astype(BF16)
            zs_sc[pl.ds(t, slots, stride=stride), :] = lax.dot_general(zit, oht, _NT, preferred_element_type=F32)
            return carry

        lax.fori_loop(0, tt, scatter, 0)
        o_ref[...] = x_ref[...]

    base = step * ib
    zblk = jnp.concatenate(
        [zs_sc[pl.ds(pl.multiple_of((base + ii) * stride, 8), tt), :].astype(BF16) for ii in range(ib)], axis=1)
    o_ref[...] += _dot(zblk, v_ref[...])


def _peer_v(isel, jsel, gates, act, x, v, tt, ib):
    n, d = x.shape
    slots = isel.shape[1]
    nblk = v.shape[0] // (slots * ib)
    stride = tt + 8
    tok = lambda c: pl.BlockSpec((tt, c), lambda t, e: (t, 0))
    return pl.pallas_call(
        functools.partial(_peer_v_kernel, ib=ib, stride=stride), grid=(n // tt, nblk),
        in_specs=[tok(slots)] * 4 + [tok(d), pl.BlockSpec((slots * ib, d), lambda t, e: (e, 0))],
        out_specs=tok(d), out_shape=jax.ShapeDtypeStruct((n, d), F32),
        scratch_shapes=[pltpu.VMEM((tt, slots), F32), pltpu.VMEM((slots * stride, slots), F32)],
        compiler_params=_params("parallel", "arbitrary"), name="peer_v",
    )(isel, jsel, gates, act, x, v)


def _final_kernel(x_ref, g_ref, o_ref):
    o_ref[...] = _rms(x_ref[...], g_ref[...])


def _final_norm(x, g, tm):
    n, d = x.shape
    return pl.pallas_call(
        _final_kernel, grid=(n // tm,),
        in_specs=[pl.BlockSpec((tm, d), lambda i: (i, 0)), pl.BlockSpec((1, d), lambda i: (0, 0))],
        out_specs=pl.BlockSpec((tm, d), lambda i: (i, 0)), out_shape=jax.ShapeDtypeStruct((n, d), F32),
        compiler_params=_params("parallel"), name="final_norm",
    )(x, g)


def _pick(n, pref):
    return pref if n % pref == 0 else n


def _peer(x2, h3, pq, lw):
    n = x2.shape[0]
    it, jt, gt = _route(pq, lw["subkeys"], _pick(n, 256))
    isel, jsel, gates = it.T, jt.T, gt.T
    act = _peer_u(h3, lw["u"], isel, jsel, _pick(n, 512), 4)
    return _peer_v(isel, jsel, gates, act, x2, lw["v"], _pick(n, 256), 16)


def kernel(x_prompt, x_sample, cache_k, cache_v, cache_logf, state_hg, cache_mem_k, cache_mem_v, page_table, mem_prompt, ln_mix, w_in, b_fox_f, hg_lb, fox_gn, hg_gn, w_out, ln_x, ln_mem, w_xq, w_xk, w_xv, w_xo, ln_ffn, peer_wq, peer_subkeys, peer_u, peer_v, ln_final):
    depth, d_model, _ = w_in.shape
    bp, seq, _ = x_prompt.shape
    bs = x_sample.shape[0]
    fox_heads = b_fox_f.shape[1]
    fox_w = fox_gn.shape[1]
    hg_w = hg_gn.shape[1]
    hg_heads = hg_w // HG_DK
    n_phys, page = cache_k.shape[1], cache_k.shape[2]
    n_mem = mem_prompt.shape[1]
    x_w = w_xq.shape[2]
    x_heads = x_w // X_HEAD_DIM
    p_heads = peer_subkeys.shape[1]
    p_keys, p_half = peer_subkeys.shape[3], peer_subkeys.shape[4]
    pairs = fox_w // LANES

    lb_all = jnp.cumsum(jax.nn.softmax(hg_lb.astype(F32), axis=0), axis=0)
    row = lambda a: a.reshape(1, -1).astype(F32)

    xp = x_prompt.reshape(bp * seq, d_model)
    xs = x_sample.reshape(bs, d_model)
    mem = mem_prompt.reshape(bp * n_mem, d_model)
    ck_pages = cache_k.reshape(depth, n_phys, page, fox_w)
    cv_pages = cache_v.reshape(depth, n_phys, page, fox_w)
    memk = cache_mem_k.reshape(depth, bs, n_mem, x_w)
    memv = cache_mem_v.reshape(depth, bs, n_mem, x_w)

    outs = {k: [] for k in ("kp", "vp", "fp", "hp", "mkp", "mvp", "ks", "vs", "fs", "hs")}
    for l in range(depth):
        wl = w_in[l]
        c0 = 3 * fox_w
        lw = dict(
            w_main=jnp.concatenate([wl[:, :c0], wl[:, c0 + fox_heads:]], axis=1).astype(BF16),
            w_ff=jnp.pad(wl[:, c0:c0 + fox_heads], ((0, 0), (0, LANES - fox_heads))).astype(BF16),
            b_f=jnp.pad(row(b_fox_f[l]), ((0, 0), (0, LANES - fox_heads))),
            lb=row(lb_all[l] - lb_all[0]),
            subkeys=peer_subkeys[l].reshape(p_heads * 2, p_keys, p_half).astype(BF16),
            u=peer_u[l].astype(BF16), v=peer_v[l].astype(BF16),
        )
        w_out_b, w_xq_b, w_xo_b = w_out[l].astype(BF16), w_xq[l].astype(BF16), w_xo[l].astype(BF16)
        peer_wq_b = peer_wq[l].astype(BF16)

        tm = _pick(bp * seq, 512)
        fq, fk, fv, flf, hq, hk, hv, hlf, hgate = _inproj(xp, row(ln_mix[l]), lw["w_main"], lw["w_ff"], lw["b_f"],
                                                           lw["lb"], tm, fox_heads)
        lf_t = flf.reshape(bp, seq, fox_heads).transpose(0, 2, 1).reshape(bp * fox_heads, seq)
        c_t = _cumsum_lanes(lf_t, _pick(seq, 512))
        ck = c_t.reshape(bp * pairs, 2, seq)
        cq = c_t.reshape(bp, pairs, 2, seq).transpose(1, 0, 3, 2).reshape(pairs, bp * seq, 2)
        fo = _fox_prompt(fq, fk, fv, cq, ck, bp, _pick(seq, 512))
        ho, st = _hgrn_prompt(hq, hk, hv, hlf, bp)
        x1, qx = _merge(fo, ho, hgate, xp, row(fox_gn[l]), row(hg_gn[l]), w_out_b, row(ln_x[l]), w_xq_b, tm)
        mk, mv = _memkv(mem, row(ln_mem[l]), w_xk[l].astype(BF16), w_xv[l].astype(BF16), n_mem)
        ctx = _xattn_prompt(qx, mk, mv, bp, tm)
        x2, h3, pq = _xo_peerq(x1, ctx, w_xo_b, row(ln_ffn[l]), peer_wq_b, tm)
        xp = _peer(x2, h3, pq, lw)
        outs["kp"].append(fk.reshape(bp, seq, fox_heads, HEAD_DIM))
        outs["vp"].append(fv.reshape(bp, seq, fox_heads, HEAD_DIM))
        outs["fp"].append(flf.reshape(bp, seq, fox_heads))
        st5 = st.reshape(bp, pairs, 2, HG_DK, 2, HG_DK)
        s_heads = jnp.stack([st5[:, :, 0, :, 0, :], st5[:, :, 1, :, 1, :]], axis=2)
        outs["hp"].append(s_heads.reshape(bp, hg_heads, HG_DK, HG_DK).transpose(0, 1, 3, 2))
        outs["mkp"].append(mk.reshape(bp, n_mem, x_heads, X_HEAD_DIM))
        outs["mvp"].append(mv.reshape(bp, n_mem, x_heads, X_HEAD_DIM))

        sq, sk, sv, slf, tq, tk, tv, tlf, tgate = _inproj(xs, row(ln_mix[l]), lw["w_main"], lw["w_ff"], lw["b_f"],
                                                           lw["lb"], bs, fox_heads)
        tok3 = lambda a: a.reshape(bs, 1, a.shape[1])
        fo_s = _fox_decode(page_table, tok3(sq), tok3(sk), tok3(sv), tok3(slf), ck_pages, cv_pages,
                           cache_logf, l).reshape(bs, fox_w)
        colv = lambda a: a.reshape(bs, hg_heads, HG_DK, 1)
        ho_s, s_new = _hgrn_step(colv(tq), colv(tk), colv(tlf), tv.reshape(bs, hg_heads, 1, HG_DK), state_hg, l,
                                 _pick(bs, 8))
        x1s, qxs = _merge(fo_s, ho_s.reshape(bs, hg_w), tgate, xs, row(fox_gn[l]), row(hg_gn[l]), w_out_b,
                          row(ln_x[l]), w_xq_b, bs)
        ctx_s = _xattn_sample(tok3(qxs), memk, memv, l, _pick(bs, 8)).reshape(bs, x_w)
        x2s, h3s, pqs = _xo_peerq(x1s, ctx_s, w_xo_b, row(ln_ffn[l]), peer_wq_b, bs)
        xs = _peer(x2s, h3s, pqs, lw)
        outs["ks"].append(sk.reshape(bs, 1, fox_heads, HEAD_DIM))
        outs["vs"].append(sv.reshape(bs, 1, fox_heads, HEAD_DIM))
        outs["fs"].append(slf.reshape(bs, 1, fox_heads))
        outs["hs"].append(s_new)

    y_prompt = _final_norm(xp, row(ln_final), _pick(bp * seq, 512)).reshape(bp, seq, d_model)
    y_sample = _final_norm(xs, row(ln_final), bs).reshape(bs, 1, d_model)
    st_ = lambda k: jnp.stack(outs[k])
    return (y_prompt, y_sample, st_("kp"), st_("vp"), st_("fp"), st_("hp"), st_("mkp"), st_("mvp"),
            st_("ks"), st_("vs"), st_("fs"), st_("hs"))
```

```python
import functools
import math

import jax
import jax.numpy as jnp
from jax import lax
from jax.experimental import pallas as pl
from jax.experimental.pallas import tpu as pltpu

F32 = jnp.float32
BF16 = jnp.bfloat16
I32 = jnp.int32
EPS = 1e-6
HIGHEST = lax.Precision.HIGHEST
NEG_INF = float("-inf")

HEAD_DIM = 64
HG_DK = 64
HG_CHUNK = 64
X_HEAD_DIM = 128
PEER_TOPK = 16
LANES = 128
VMEM_LIMIT = 56 * 1024 * 1024

_NT = (((1,), (1,)), ((), ()))
_TN = (((0,), (0,)), ((), ()))


def _params(*sem):
    return pltpu.CompilerParams(dimension_semantics=sem, vmem_limit_bytes=VMEM_LIMIT)


def _rms(x, g):
    ms = jnp.mean(x * x, axis=-1, keepdims=True)
    return x * lax.rsqrt(ms + EPS) * g


def _log_sigmoid(x):
    return jnp.minimum(x, 0.0) - jnp.log1p(jnp.exp(-jnp.abs(x)))


def _dot(a, b):
    return jnp.dot(a, b, preferred_element_type=F32)


def _dot_exact(a, b):
    return jnp.dot(a, b, precision=HIGHEST, preferred_element_type=F32)


def _iota(shape, axis):
    return lax.broadcasted_iota(I32, shape, axis)


def _inproj_kernel(x_ref, g_ref, w_ref, wff_ref, bf_ref, lb_ref,
                   fq_ref, fk_ref, fv_ref, lf_ref, hq_ref, hk_ref, hv_ref, hlf_ref, gate_ref, *, fox_heads):
    h = _rms(x_ref[...], g_ref[...]).astype(BF16)
    w = w_ref.shape[1] // 7

    def mm(j):
        return _dot(h, w_ref[:, j * w:(j + 1) * w])

    fq_ref[...] = mm(0)
    fk_ref[...] = mm(1)
    fv_ref[...] = mm(2)
    ff = _dot(h, wff_ref[...])
    lf_ref[...] = _log_sigmoid(ff + bf_ref[...])[:, :fox_heads]
    hq_ref[...] = mm(3)
    z = mm(4)
    lb = lb_ref[...]
    a = jnp.log(lb)
    b = jnp.log1p(-lb) + _log_sigmoid(z)
    hlf_ref[...] = jnp.maximum(a, b) + jnp.log1p(jnp.exp(-jnp.abs(a - b)))
    hk_ref[...] = (1.0 - lb) * (1.0 / (1.0 + jnp.exp(z)))
    hv_ref[...] = mm(5)
    gate_ref[...] = mm(6)


def _inproj(x, g, w_main, w_ff, b_f, lb, tm, fox_heads):
    n, d = x.shape
    w = w_main.shape[1] // 7
    full = lambda a: pl.BlockSpec(a.shape, lambda i: (0,) * a.ndim)
    row = lambda c: pl.BlockSpec((tm, c), lambda i: (i, 0))
    outs = [jax.ShapeDtypeStruct((n, w), F32)] * 3 + [jax.ShapeDtypeStruct((n, fox_heads), F32)] + \
           [jax.ShapeDtypeStruct((n, w), F32)] * 5
    out_specs = [row(w)] * 3 + [row(fox_heads)] + [row(w)] * 5
    return pl.pallas_call(
        functools.partial(_inproj_kernel, fox_heads=fox_heads),
        grid=(n // tm,),
        in_specs=[row(d), full(g), full(w_main), full(w_ff), full(b_f), full(lb)],
        out_specs=out_specs, out_shape=outs,
        compiler_params=_params("parallel"), name="inproj",
    )(x, g, w_main, w_ff, b_f, lb)


def _cumsum_kernel(x_ref, o_ref, carry_ref):
    @pl.when(pl.program_id(0) == 0)
    def _():
        carry_ref[...] = jnp.zeros_like(carry_ref)

    x = x_ref[...]
    n = x.shape[1]
    tri = (_iota((n, n), 0) <= _iota((n, n), 1)).astype(F32)
    c = _dot_exact(x, tri) + carry_ref[:, :1]
    o_ref[...] = c
    carry_ref[...] = jnp.broadcast_to(c[:, n - 1:n], carry_ref.shape)


def _cumsum_lanes(x, tc):
    r, t = x.shape
    return pl.pallas_call(
        _cumsum_kernel, grid=(t // tc,),
        in_specs=[pl.BlockSpec((r, tc), lambda i: (0, i))],
        out_specs=pl.BlockSpec((r, tc), lambda i: (0, i)),
        out_shape=jax.ShapeDtypeStruct((r, t), F32),
        scratch_shapes=[pltpu.VMEM((r, LANES), F32)],
        compiler_params=_params("arbitrary"), name="logf_cumsum",
    )(x)


def _fox_kernel(qt_ref, kt_ref, q_ref, k_ref, v_ref, cq_ref, ck_ref, o_ref, m_ref, cqr_ref, acc_ref, *, scale):
    qi = qt_ref[pl.program_id(2)]
    ki = kt_ref[pl.program_id(2)]
    tq = q_ref.shape[0]
    tk = k_ref.shape[0]
    lane = _iota((1, LANES), 1)

    @pl.when(ki == 0)
    def _():
        m_ref[...] = jnp.full_like(m_ref, NEG_INF)
        acc_ref[...] = jnp.zeros_like(acc_ref)
        for hh in range(2):
            cqr_ref[hh] = jnp.broadcast_to(cq_ref[:, hh:hh + 1], (tq, LANES))

    def step(masked):
        q = q_ref[...] * scale
        k = k_ref[...].astype(BF16)
        v = v_ref[...]
        if masked:
            causal = (ki * tk + _iota((1, tk), 1)) <= (qi * tq + _iota((tq, 1), 0))
        for hh in range(2):
            in_head = (lane // HEAD_DIM) == hh
            qh = jnp.where(in_head, q, 0.0).astype(BF16)
            vh = jnp.where(in_head, v, 1.0).astype(BF16)
            s = lax.dot_general(qh, k, _NT, preferred_element_type=F32) - ck_ref[hh:hh + 1, :]
            if masked:
                s = jnp.where(causal, s, NEG_INF)
            m_prev = m_ref[hh]
            cq = cqr_ref[hh]
            m_new = jnp.maximum(m_prev, jnp.max(s, axis=1, keepdims=True) + cq)
            p = jnp.exp(s - jnp.tile(m_new - cq, (1, tk // LANES)))
            acc_ref[hh] = jnp.exp(m_prev - m_new) * acc_ref[hh] + _dot(p.astype(BF16), vh)
            m_ref[hh] = m_new

    @pl.when(ki < qi)
    def _():
        step(False)

    @pl.when(ki == qi)
    def _():
        step(True)
        a0 = acc_ref[0]
        a1 = acc_ref[1]
        o0 = a0 / pltpu.roll(a0, HEAD_DIM, axis=1)
        o1 = a1 / pltpu.roll(a1, HEAD_DIM, axis=1)
        o_ref[...] = jnp.where(lane < HEAD_DIM, o0, o1)


def _fox_prompt(q, k, v, cq, ck, batch, tq):
    n, w = q.shape
    t = n // batch
    pairs = w // LANES
    nq = t // tq
    kern = functools.partial(_fox_kernel, scale=HEAD_DIM ** -0.5)
    steps = [(i, j) for i in range(nq) for j in range(i + 1)]
    qi_tab = jnp.asarray([s[0] for s in steps], I32)
    ki_tab = jnp.asarray([s[1] for s in steps], I32)
    grid_spec = pltpu.PrefetchScalarGridSpec(
        num_scalar_prefetch=2, grid=(batch, pairs, len(steps)),
        in_specs=[
            pl.BlockSpec((tq, LANES), lambda b, g, s, qt, kt: (b * nq + qt[s], g)),
            pl.BlockSpec((tq, LANES), lambda b, g, s, qt, kt: (b * nq + kt[s], g)),
            pl.BlockSpec((tq, LANES), lambda b, g, s, qt, kt: (b * nq + kt[s], g)),
            pl.BlockSpec((None, tq, 2), lambda b, g, s, qt, kt: (g, b * nq + qt[s], 0)),
            pl.BlockSpec((None, 2, tq), lambda b, g, s, qt, kt: (b * pairs + g, 0, kt[s])),
        ],
        out_specs=pl.BlockSpec((tq, LANES), lambda b, g, s, qt, kt: (b * nq + qt[s], g)),
        scratch_shapes=[pltpu.VMEM((2, tq, LANES), F32)] * 3,
    )
    return pl.pallas_call(
        kern, grid_spec=grid_spec, out_shape=jax.ShapeDtypeStruct((n, w), F32),
        compiler_params=_params("parallel", "parallel", "arbitrary"), name="fox_prompt",
    )(qi_tab, ki_tab, q, k, v, cq, ck)


def _hgrn_kernel(q_ref, k_ref, v_ref, lf_ref, o_ref, st_ref, st_sc):
    n = pl.program_id(1)
    c = q_ref.shape[0]
    pairs = q_ref.shape[1] // LANES

    @pl.when(n == 0)
    def _():
        st_sc[...] = jnp.zeros_like(st_sc)

    tril = (_iota((c, c), 1) <= _iota((c, c), 0)).astype(F32)
    same_head = (_iota((LANES, LANES), 0) // HG_DK) == (_iota((LANES, LANES), 1) // HG_DK)
    seg = same_head.astype(BF16)
    ti = _iota((c, c, LANES), 0)
    si = _iota((c, c, LANES), 1)
    for g in range(pairs):
        sl = slice(g * LANES, (g + 1) * LANES)
        q = q_ref[:, sl]
        k = k_ref[:, sl]
        v = v_ref[:, sl]
        b = _dot_exact(tril, lf_ref[:, sl])
        st = st_sc[g]
        qe = (q * jnp.exp(b)).astype(BF16)
        o = lax.dot_general(qe, st.astype(BF16), _NT, preferred_element_type=F32)
        decay = jnp.exp(jnp.where(si <= ti, b[:, None, :] - b[None, :, :], NEG_INF))
        p = (q[:, None, :] * k[None, :, :]) * decay
        a = _dot(p.reshape(c * c, LANES).astype(BF16), seg)
        o = o + jnp.sum(a.reshape(c, c, LANES) * v[None, :, :], axis=1)
        o_ref[:, sl] = o
        b_last = b[c - 1:c, :]
        kd = (k * jnp.exp(b_last - b)).astype(BF16)
        upd = lax.dot_general(v.astype(BF16), kd, _TN, preferred_element_type=F32)
        st_sc[g] = st * jnp.exp(b_last) + jnp.where(same_head, upd, 0.0)

    @pl.when(n == pl.num_programs(1) - 1)
    def _():
        st_ref[...] = st_sc[...]


def _hgrn_prompt(q, k, v, lf, batch):
    n, w = q.shape
    t = n // batch
    nc = t // HG_CHUNK
    pairs = w // LANES
    blk = pl.BlockSpec((HG_CHUNK, w), lambda b, i: (b * nc + i, 0))
    return pl.pallas_call(
        _hgrn_kernel, grid=(batch, nc),
        in_specs=[blk] * 4,
        out_specs=[blk, pl.BlockSpec((None, pairs, LANES, LANES), lambda b, i: (b, 0, 0, 0))],
        out_shape=[jax.ShapeDtypeStruct((n, w), F32), jax.ShapeDtypeStruct((batch, pairs, LANES, LANES), F32)],
        scratch_shapes=[pltpu.VMEM((pairs, LANES, LANES), F32)],
        compiler_params=_params("parallel", "arbitrary"), name="hgrn_prompt",
    )(q, k, v, lf)


def _head_rows(x_row, heads, width):
    w = x_row.shape[1]
    keep = (_iota((8, w), 1) // width) == _iota((8, w), 0)
    return jnp.where(keep, jnp.broadcast_to(x_row, (8, w)), 0.0)


def _expand_matrix(heads_pad, w, width, dtype):
    return ((_iota((heads_pad, w), 1) // width) == _iota((heads_pad, w), 0)).astype(dtype)


def _fox_decode_kernel(pt_ref, q_ref, kn_ref, vn_ref, lfn_ref, *refs, scale, n_pages):
    k_refs = refs[:n_pages]
    v_refs = refs[n_pages:2 * n_pages]
    lf_refs = refs[2 * n_pages:3 * n_pages]
    o_ref = refs[3 * n_pages]
    w, ps = k_refs[0].shape
    nh = lf_refs[0].shape[0]
    hd = w // nh
    heads = lambda x: x.reshape(nh, hd, x.shape[1])
    q = q_ref[...] * scale
    qb = jnp.broadcast_to(q, (w, ps))
    triu = (_iota((ps, ps), 0) <= _iota((ps, ps), 1)).astype(F32)

    carry = jnp.zeros((nh, 1), F32)
    scores = []
    for p in range(n_pages):
        s = jnp.sum(heads(k_refs[p][...] * qb), axis=1)
        prefix = _dot_exact(lf_refs[p][...], triu) + carry
        carry = prefix[:, ps - 1:ps]
        scores.append(s - prefix)
    s_new = jnp.sum(heads(kn_ref[...] * q), axis=1) - (carry + lfn_ref[...])

    m_tile = scores[0]
    for s in scores[1:]:
        m_tile = jnp.maximum(m_tile, s)
    m = jnp.maximum(s_new, jnp.max(m_tile, axis=1, keepdims=True))
    e_new = jnp.exp(s_new - m)
    exps = [jnp.exp(s - m) for s in scores]
    l_tile = exps[0]
    for e in exps[1:]:
        l_tile = l_tile + e
    inv = 1.0 / (e_new + jnp.sum(l_tile, axis=1, keepdims=True))
    acc = jnp.zeros((w, ps), F32)
    for p in range(n_pages):
        pb = jnp.broadcast_to(exps[p][:, None, :], (nh, hd, ps)).reshape(w, ps)
        acc = acc + v_refs[p][...] * pb
    rep = lambda x: jnp.broadcast_to(x[:, None, :], (nh, hd, 1)).reshape(w, 1)
    o_ref[...] = (jnp.sum(acc, axis=1, keepdims=True) + rep(e_new) * vn_ref[...]) * rep(inv)


def _fox_decode(page_table, q, k_new, v_new, lf_new, cache_kt, cache_vt, cache_lft, layer):
    r, w, _ = q.shape
    nh = lf_new.shape[1]
    n_pages = page_table.shape[1]
    ps = cache_kt.shape[3]
    pt = page_table.reshape(-1)
    tok = lambda c: pl.BlockSpec((None, c, 1), lambda i, pt: (i, 0, 0))

    def page(p, rows):
        return pl.BlockSpec((None, None, rows, ps), lambda i, pt: (layer, pt[i * n_pages + p], 0, 0))

    pages = lambda rows: [page(p, rows) for p in range(n_pages)]
    grid_spec = pltpu.PrefetchScalarGridSpec(
        num_scalar_prefetch=1, grid=(r,),
        in_specs=[tok(w), tok(w), tok(w), tok(nh)] + pages(w) + pages(w) + pages(nh),
        out_specs=tok(w),
    )
    return pl.pallas_call(
        functools.partial(_fox_decode_kernel, scale=HEAD_DIM ** -0.5, n_pages=n_pages),
        grid_spec=grid_spec, out_shape=jax.ShapeDtypeStruct((r, w, 1), F32),
        compiler_params=_params("parallel"), name="fox_decode",
    )(pt, q, k_new, v_new, lf_new, *([cache_kt] * n_pages), *([cache_vt] * n_pages), *([cache_lft] * n_pages))


def _hgrn_step_kernel(q_ref, k_ref, lf_ref, v_ref, s_ref, o_ref, so_ref):
    q = q_ref[...]
    k = k_ref[...]
    f = jnp.exp(lf_ref[...])
    v = v_ref[...]
    s = s_ref[...]
    o_ref[...] = jnp.sum((q * f) * s, axis=2, keepdims=True) + jnp.sum(q * k, axis=2, keepdims=True) * v
    so_ref[...] = f * s + k * v


def _hgrn_step(q, k, lf, v, state, layer, rb):
    r, h, dk, _ = q.shape
    dv = v.shape[3]
    col = pl.BlockSpec((rb, h, dk, 1), lambda i: (i, 0, 0, 0))
    rowv = pl.BlockSpec((rb, h, 1, dv), lambda i: (i, 0, 0, 0))
    return pl.pallas_call(
        _hgrn_step_kernel, grid=(r // rb,),
        in_specs=[col, col, col, rowv, pl.BlockSpec((None, rb, h, dk, dv), lambda i: (layer, i, 0, 0, 0))],
        out_specs=[rowv, pl.BlockSpec((rb, h, dk, dv), lambda i: (i, 0, 0, 0))],
        out_shape=[jax.ShapeDtypeStruct((r, h, 1, dv), F32), jax.ShapeDtypeStruct((r, h, dk, dv), F32)],
        compiler_params=_params("parallel"), name="hgrn_step",
    )(q, k, lf, v, state)


def _head_mean_sq(y, width):
    w = y.shape[1]
    seg = ((_iota((w, w), 0) // width) == (_iota((w, w), 1) // width)).astype(BF16)
    sq = y * y
    hi = sq.astype(BF16)
    lo = (sq - hi.astype(F32)).astype(BF16)
    return (_dot(hi, seg) + _dot(lo, seg)) * (1.0 / width)


def _merge_kernel(fo_ref, ho_ref, gate_ref, x_ref, fgn_ref, hgn_ref, wo_ref, lnx_ref, wxq_ref, x1_ref, qx_ref):
    fo = fo_ref[...]
    ho = ho_ref[...]
    fw = fo.shape[1]
    fn = fo * lax.rsqrt(_head_mean_sq(fo, HEAD_DIM) + EPS) * fgn_ref[...]
    gate = gate_ref[...]
    hn = ho * lax.rsqrt(_head_mean_sq(ho, HEAD_DIM) + EPS) * hgn_ref[...] * (gate / (1.0 + jnp.exp(-gate)))
    y = _dot(fn.astype(BF16), wo_ref[:fw, :]) + _dot(hn.astype(BF16), wo_ref[fw:, :])
    x1 = x_ref[...] + y
    x1_ref[...] = x1
    qx_ref[...] = _dot(_rms(x1, lnx_ref[...]).astype(BF16), wxq_ref[...])


def _merge(fo, ho, gate, x, fgn, hgn, w_out, ln_x, w_xq, tm):
    n, d = x.shape
    xw = w_xq.shape[1]
    full = lambda a: pl.BlockSpec(a.shape, lambda i: (0,) * a.ndim)
    row = lambda c: pl.BlockSpec((tm, c), lambda i: (i, 0))
    return pl.pallas_call(
        _merge_kernel, grid=(n // tm,),
        in_specs=[row(fo.shape[1]), row(ho.shape[1]), row(gate.shape[1]), row(d),
                  full(fgn), full(hgn), full(w_out), full(ln_x), full(w_xq)],
        out_specs=[row(d), row(xw)],
        out_shape=[jax.ShapeDtypeStruct((n, d), F32), jax.ShapeDtypeStruct((n, xw), F32)],
        compiler_params=_params("parallel"), name="merge",
    )(fo, ho, gate, x, fgn, hgn, w_out, ln_x, w_xq)


def _memkv_kernel(m_ref, g_ref, wk_ref, wv_ref, k_ref, v_ref):
    m = _rms(m_ref[...], g_ref[...]).astype(BF16)
    k_ref[...] = _dot(m, wk_ref[...])
    v_ref[...] = _dot(m, wv_ref[...])


def _memkv(mem, g, wk, wv, tm):
    n, d = mem.shape
    xw = wk.shape[1]
    full = lambda a: pl.BlockSpec(a.shape, lambda i: (0,) * a.ndim)
    row = lambda c: pl.BlockSpec((tm, c), lambda i: (i, 0))
    return pl.pallas_call(
        _memkv_kernel, grid=(n // tm,),
        in_specs=[row(d), full(g), full(wk), full(wv)],
        out_specs=[row(xw), row(xw)],
        out_shape=[jax.ShapeDtypeStruct((n, xw), F32)] * 2,
        compiler_params=_params("parallel"), name="memkv",
    )(mem, g, wk, wv)


def _xattn_prompt_kernel(q_ref, mk_ref, mv_ref, o_ref, *, scale):
    heads = q_ref.shape[1] // X_HEAD_DIM
    for h in range(heads):
        sl = slice(h * X_HEAD_DIM, (h + 1) * X_HEAD_DIM)
        q = (q_ref[:, sl] * scale).astype(BF16)
        s = lax.dot_general(q, mk_ref[:, sl].astype(BF16), _NT, preferred_element_type=F32)
        e = jnp.exp(s - jnp.max(s, axis=1, keepdims=True))
        p = e / jnp.sum(e, axis=1, keepdims=True)
        o_ref[:, sl] = _dot(p.astype(BF16), mv_ref[:, sl].astype(BF16))


def _xattn_prompt(q, mk, mv, batch, tm):
    n, xw = q.shape
    t = n // batch
    nm = mk.shape[0] // batch
    nt = t // tm
    return pl.pallas_call(
        functools.partial(_xattn_prompt_kernel, scale=X_HEAD_DIM ** -0.5), grid=(batch, nt),
        in_specs=[pl.BlockSpec((tm, xw), lambda b, i: (b * nt + i, 0)),
                  pl.BlockSpec((nm, xw), lambda b, i: (b, 0)),
                  pl.BlockSpec((nm, xw), lambda b, i: (b, 0))],
        out_specs=pl.BlockSpec((tm, xw), lambda b, i: (b * nt + i, 0)),
        out_shape=jax.ShapeDtypeStruct((n, xw), F32),
        compiler_params=_params("parallel", "parallel"), name="xattn_prompt",
    )(q, mk, mv)


def _xattn_sample_kernel(q_ref, mk_ref, mv_ref, o_ref, *, scale):
    rb, _, w = q_ref.shape
    heads = w // X_HEAD_DIM
    expand = _expand_matrix(8, w, X_HEAD_DIM, BF16)
    valid = _iota((1, 8), 1) < heads
    for r in range(rb):
        qrows = _head_rows(q_ref[r] * scale, heads, X_HEAD_DIM).astype(BF16)
        s = lax.dot_general(mk_ref[r].astype(BF16), qrows, _NT, preferred_element_type=F32)
        e = jnp.exp(s - jnp.max(s, axis=0, keepdims=True))
        p = jnp.where(valid, e / jnp.sum(e, axis=0, keepdims=True), 0.0)
        pe = _dot(p.astype(BF16), expand)
        o_ref[r] = jnp.sum(pe * mv_ref[r], axis=0, keepdims=True)


def _xattn_sample(q, mem_k, mem_v, layer, rb):
    r, _, w = q.shape
    nm = mem_k.shape[2]
    tok = pl.BlockSpec((rb, 1, w), lambda i: (i, 0, 0))
    mem = pl.BlockSpec((None, rb, nm, w), lambda i: (layer, i, 0, 0))
    return pl.pallas_call(
        functools.partial(_xattn_sample_kernel, scale=X_HEAD_DIM ** -0.5), grid=(r // rb,),
        in_specs=[tok, mem, mem], out_specs=tok,
        out_shape=jax.ShapeDtypeStruct((r, 1, w), F32),
        compiler_params=_params("parallel"), name="xattn_sample",
    )(q, mem_k, mem_v)


def _xo_peerq_kernel(x1_ref, ctx_ref, wxo_ref, lnf_ref, wq_ref, x2_ref, h_ref, pq_ref):
    x2 = x1_ref[...] + _dot(ctx_ref[...].astype(BF16), wxo_ref[...])
    x2_ref[...] = x2
    h = _rms(x2, lnf_ref[...]).astype(BF16)
    h_ref[...] = h
    pq_ref[...] = _dot(h, wq_ref[...])


def _xo_peerq(x1, ctx, w_xo, ln_ffn, peer_wq, tm):
    n, d = x1.shape
    qw = peer_wq.shape[1]
    full = lambda a: pl.BlockSpec(a.shape, lambda i: (0,) * a.ndim)
    row = lambda c: pl.BlockSpec((tm, c), lambda i: (i, 0))
    return pl.pallas_call(
        _xo_peerq_kernel, grid=(n // tm,),
        in_specs=[row(d), row(ctx.shape[1]), full(w_xo), full(ln_ffn), full(peer_wq)],
        out_specs=[row(d), row(d), row(qw)],
        out_shape=[jax.ShapeDtypeStruct((n, d), F32), jax.ShapeDtypeStruct((n, d), BF16),
                   jax.ShapeDtypeStruct((n, qw), F32)],
        compiler_params=_params("parallel"), name="xo_peerq",
    )(x1, ctx, w_xo, ln_ffn, peer_wq)


def _top16(s):
    kk = s.shape[0]
    kio = _iota(s.shape, 0)
    vals, idxs = [], []
    for _ in range(PEER_TOPK):
        m = jnp.max(s, axis=0, keepdims=True)
        idx = jnp.min(jnp.where(s == m, kio, kk), axis=0, keepdims=True)
        s = jnp.where(kio == idx, NEG_INF, s)
        vals.append(m)
        idxs.append(idx)
    return jnp.concatenate(vals, axis=0), jnp.concatenate(idxs, axis=0)


def _route_kernel(pq_ref, sk_ref, i_ref, j_ref, g_ref):
    half = sk_ref.shape[2]
    tt = pq_ref.shape[0]
    sv, si = [], []
    for p in range(2):
        q = pq_ref[:, p * half:(p + 1) * half].astype(BF16)
        st = lax.dot_general(sk_ref[p], q, _NT, preferred_element_type=F32)
        v, i = _top16(st)
        sv.append(v)
        si.append(i)
    k = PEER_TOPK
    sub = 8
    rows8 = _iota((sub, tt), 0)
    cands, ics, jcs = [sv[0][0:1, :] + sv[1]], [jnp.broadcast_to(si[0][0:1, :], (k, tt))], [si[1]]
    for a in range(1, sub):
        nb = k // (a + 1)
        cands.append(jnp.where(rows8 < nb, sv[0][a:a + 1, :] + sv[1][:sub, :], NEG_INF))
        ics.append(jnp.broadcast_to(si[0][a:a + 1, :], (sub, tt)))
        jcs.append(si[1][:sub, :])
    cands.append(sv[0][sub:, :] + sv[1][0:1, :])
    ics.append(si[0][sub:, :])
    jcs.append(jnp.broadcast_to(si[1][0:1, :], (k - sub, tt)))
    cand = jnp.concatenate(cands, axis=0)
    ic = jnp.concatenate(ics, axis=0)
    jc = jnp.concatenate(jcs, axis=0)
    pio = _iota(cand.shape, 0)
    tv, ti, tj = [], [], []
    for _ in range(k):
        m = jnp.max(cand, axis=0, keepdims=True)
        pos = jnp.min(jnp.where(cand == m, pio, cand.shape[0]), axis=0, keepdims=True)
        sel = pio == pos
        ti.append(jnp.sum(jnp.where(sel, ic, 0), axis=0, keepdims=True))
        tj.append(jnp.sum(jnp.where(sel, jc, 0), axis=0, keepdims=True))
        cand = jnp.where(sel, NEG_INF, cand)
        tv.append(m)
    top = jnp.concatenate(tv, axis=0)
    e = jnp.exp(top - top[0:1, :])
    g_ref[...] = e / jnp.sum(e, axis=0, keepdims=True)
    i_ref[...] = jnp.concatenate(ti, axis=0)
    j_ref[...] = jnp.concatenate(tj, axis=0)


def _route(pq, subkeys, tt):
    n = pq.shape[0]
    hp, keys, half = subkeys.shape
    heads = hp // 2
    k = PEER_TOPK
    out = pl.BlockSpec((k, tt), lambda i, h: (h, i))
    return pl.pallas_call(
        _route_kernel, grid=(n // tt, heads),
        in_specs=[pl.BlockSpec((tt, 2 * half), lambda i, h: (i, h)),
                  pl.BlockSpec((2, keys, half), lambda i, h: (h, 0, 0))],
        out_specs=[out, out, out],
        out_shape=[jax.ShapeDtypeStruct((heads * k, n), I32), jax.ShapeDtypeStruct((heads * k, n), I32),
                   jax.ShapeDtypeStruct((heads * k, n), F32)],
        compiler_params=_params("parallel", "parallel"), name="peer_route",
    )(pq, subkeys)


def _peer_u_kernel(h_ref, ut_ref, i_ref, j_ref, a_ref, *, ib):
    step = pl.program_id(1)

    @pl.when(step == 0)
    def _():
        a_ref[...] = jnp.zeros_like(a_ref)

    isel = i_ref[...]
    jsel = j_ref[...]
    acc = a_ref[...]
    nj = jsel.shape[1]
    a_all = _dot(h_ref[...], ut_ref[...])
    for ii in range(ib):
        a = a_all[:, ii * nj:(ii + 1) * nj]
        acc = jnp.where(isel == step * ib + ii, jnp.take_along_axis(a, jsel, axis=1), acc)
    a_ref[...] = acc


def _peer_u(h, ut, isel, jsel, tt, ib):
    n, d = h.shape
    slots = isel.shape[1]
    nblk = ut.shape[1] // (slots * ib)
    tok = lambda c: pl.BlockSpec((tt, c), lambda t, e: (t, 0))
    return pl.pallas_call(
        functools.partial(_peer_u_kernel, ib=ib), grid=(n // tt, nblk),
        in_specs=[tok(d), pl.BlockSpec((d, slots * ib), lambda t, e: (0, e)), tok(slots), tok(slots)],
        out_specs=tok(slots), out_shape=jax.ShapeDtypeStruct((n, slots), F32),
        compiler_params=_params("parallel", "arbitrary"), name="peer_u",
    )(h, ut, isel, jsel)


def _peer_v_kernel(i_ref, j_ref, g_ref, a_ref, x_ref, v_ref, o_ref, z_sc, zs_sc, *, ib, stride):
    step = pl.program_id(1)
    tt, slots = i_ref.shape

    @pl.when(step == 0)
    def _():
        a = a_ref[...]
        z_sc[...] = g_ref[...] * (0.5 * a * (1.0 + lax.erf(a * (2.0 ** -0.5))))

        def scatter(t, carry):
            irow = i_ref[pl.ds(t, 1), :]
            jrow = j_ref[pl.ds(t, 1), :]
            zrow = z_sc[pl.ds(t, 1), :]
            io = _iota((slots, slots), 0)
            zit = jnp.where(irow == io, zrow, 0.0).astype(BF16)
            oht = jnp.where(jrow == io, 1.0, 0.0).astype(BF16)
            zs_sc[pl.ds(t, slots, stride=stride), :] = lax.dot_general(zit, oht, _NT, preferred_element_type=F32)
            return carry

        lax.fori_loop(0, tt, scatter, 0, unroll=8)
        o_ref[...] = x_ref[...]

    base = step * ib
    zblk = jnp.concatenate(
        [zs_sc[pl.ds(pl.multiple_of((base + ii) * stride, 8), tt), :].astype(BF16) for ii in range(ib)], axis=1)
    o_ref[...] += _dot(zblk, v_ref[...])


def _peer_v(isel, jsel, gates, act, x, v, tt, ib):
    n, d = x.shape
    slots = isel.shape[1]
    nblk = v.shape[0] // (slots * ib)
    stride = tt + 8
    tok = lambda c: pl.BlockSpec((tt, c), lambda t, e: (t, 0))
    return pl.pallas_call(
        functools.partial(_peer_v_kernel, ib=ib, stride=stride), grid=(n // tt, nblk),
        in_specs=[tok(slots)] * 4 + [tok(d), pl.BlockSpec((slots * ib, d), lambda t, e: (e, 0))],
        out_specs=tok(d), out_shape=jax.ShapeDtypeStruct((n, d), F32),
        scratch_shapes=[pltpu.VMEM((tt, slots), F32), pltpu.VMEM((slots * stride, slots), F32)],
        compiler_params=_params("parallel", "arbitrary"), name="peer_v",
    )(isel, jsel, gates, act, x, v)


def _final_kernel(x_ref, g_ref, o_ref):
    o_ref[...] = _rms(x_ref[...], g_ref[...])


def _final_norm(x, g, tm):
    n, d = x.shape
    return pl.pallas_call(
        _final_kernel, grid=(n // tm,),
        in_specs=[pl.BlockSpec((tm, d), lambda i: (i, 0)), pl.BlockSpec((1, d), lambda i: (0, 0))],
        out_specs=pl.BlockSpec((tm, d), lambda i: (i, 0)), out_shape=jax.ShapeDtypeStruct((n, d), F32),
        compiler_params=_params("parallel"), name="final_norm",
    )(x, g)


def _pick(n, pref):
    return pref if n % pref == 0 else n


def _peer(x2, h3, pq, lw):
    n = x2.shape[0]
    it, jt, gt = _route(pq, lw["subkeys"], _pick(n, 256))
    isel, jsel, gates = it.T, jt.T, gt.T
    act = _peer_u(h3, lw["ut"], isel, jsel, _pick(n, 512), 4)
    return _peer_v(isel, jsel, gates, act, x2, lw["v"], _pick(n, 256), 16)


def kernel(x_prompt, x_sample, cache_k, cache_v, cache_logf, state_hg, cache_mem_k, cache_mem_v, page_table, mem_prompt, ln_mix, w_in, b_fox_f, hg_lb, fox_gn, hg_gn, w_out, ln_x, ln_mem, w_xq, w_xk, w_xv, w_xo, ln_ffn, peer_wq, peer_subkeys, peer_u, peer_v, ln_final):
    depth, d_model, _ = w_in.shape
    bp, seq, _ = x_prompt.shape
    bs = x_sample.shape[0]
    fox_heads = b_fox_f.shape[1]
    fox_w = fox_gn.shape[1]
    hg_w = hg_gn.shape[1]
    hg_heads = hg_w // HG_DK
    n_phys, page = cache_k.shape[1], cache_k.shape[2]
    n_mem = mem_prompt.shape[1]
    x_w = w_xq.shape[2]
    x_heads = x_w // X_HEAD_DIM
    p_heads = peer_subkeys.shape[1]
    p_keys, p_half = peer_subkeys.shape[3], peer_subkeys.shape[4]
    pairs = fox_w // LANES

    lb_all = jnp.cumsum(jax.nn.softmax(hg_lb.astype(F32), axis=0), axis=0)
    row = lambda a: a.reshape(1, -1).astype(F32)

    xp = x_prompt.reshape(bp * seq, d_model)
    xs = x_sample.reshape(bs, d_model)
    mem = mem_prompt.reshape(bp * n_mem, d_model)
    ck_t = cache_k.transpose(0, 1, 3, 4, 2).reshape(depth, n_phys, fox_w, page)
    cv_t = cache_v.transpose(0, 1, 3, 4, 2).reshape(depth, n_phys, fox_w, page)
    clf_t = cache_logf.transpose(0, 1, 3, 2)
    memk = cache_mem_k.reshape(depth, bs, n_mem, x_w)
    memv = cache_mem_v.reshape(depth, bs, n_mem, x_w)

    outs = {k: [] for k in ("kp", "vp", "fp", "hp", "mkp", "mvp", "ks", "vs", "fs", "hs")}
    for l in range(depth):
        wl = w_in[l]
        c0 = 3 * fox_w
        lw = dict(
            w_main=jnp.concatenate([wl[:, :c0], wl[:, c0 + fox_heads:]], axis=1).astype(BF16),
            w_ff=jnp.pad(wl[:, c0:c0 + fox_heads], ((0, 0), (0, LANES - fox_heads))).astype(BF16),
            b_f=jnp.pad(row(b_fox_f[l]), ((0, 0), (0, LANES - fox_heads))),
            lb=row(lb_all[l] - lb_all[0]),
            subkeys=peer_subkeys[l].reshape(p_heads * 2, p_keys, p_half).astype(BF16),
            ut=peer_u[l].astype(BF16).T, v=peer_v[l].astype(BF16),
        )
        w_out_b, w_xq_b, w_xo_b = w_out[l].astype(BF16), w_xq[l].astype(BF16), w_xo[l].astype(BF16)
        peer_wq_b = peer_wq[l].astype(BF16)

        tm = _pick(bp * seq, 512)
        fq, fk, fv, flf, hq, hk, hv, hlf, hgate = _inproj(xp, row(ln_mix[l]), lw["w_main"], lw["w_ff"], lw["b_f"],
                                                           lw["lb"], tm, fox_heads)
        lf_t = flf.reshape(bp, seq, fox_heads).transpose(0, 2, 1).reshape(bp * fox_heads, seq)
        c_t = _cumsum_lanes(lf_t, _pick(seq, 512))
        ck = c_t.reshape(bp * pairs, 2, seq)
        cq = c_t.reshape(bp, pairs, 2, seq).transpose(1, 0, 3, 2).reshape(pairs, bp * seq, 2)
        fo = _fox_prompt(fq, fk, fv, cq, ck, bp, _pick(seq, 512))
        ho, st = _hgrn_prompt(hq, hk, hv, hlf, bp)
        x1, qx = _merge(fo, ho, hgate, xp, row(fox_gn[l]), row(hg_gn[l]), w_out_b, row(ln_x[l]), w_xq_b, tm)
        mk, mv = _memkv(mem, row(ln_mem[l]), w_xk[l].astype(BF16), w_xv[l].astype(BF16), n_mem)
        ctx = _xattn_prompt(qx, mk, mv, bp, tm)
        x2, h3, pq = _xo_peerq(x1, ctx, w_xo_b, row(ln_ffn[l]), peer_wq_b, tm)
        xp = _peer(x2, h3, pq, lw)
        outs["kp"].append(fk.reshape(bp, seq, fox_heads, HEAD_DIM))
        outs["vp"].append(fv.reshape(bp, seq, fox_heads, HEAD_DIM))
        outs["fp"].append(flf.reshape(bp, seq, fox_heads))
        st5 = st.reshape(bp, pairs, 2, HG_DK, 2, HG_DK)
        s_heads = jnp.stack([st5[:, :, 0, :, 0, :], st5[:, :, 1, :, 1, :]], axis=2)
        outs["hp"].append(s_heads.reshape(bp, hg_heads, HG_DK, HG_DK).transpose(0, 1, 3, 2))
        outs["mkp"].append(mk.reshape(bp, n_mem, x_heads, X_HEAD_DIM))
        outs["mvp"].append(mv.reshape(bp, n_mem, x_heads, X_HEAD_DIM))

        sq, sk, sv, slf, tq, tk, tv, tlf, tgate = _inproj(xs, row(ln_mix[l]), lw["w_main"], lw["w_ff"], lw["b_f"],
                                                           lw["lb"], bs, fox_heads)
        tok3 = lambda a: a.reshape(bs, 1, a.shape[1])
        colm = lambda a: a.reshape(bs, a.shape[1], 1)
        fo_s = _fox_decode(page_table, colm(sq), colm(sk), colm(sv), colm(slf), ck_t, cv_t, clf_t,
                           l).reshape(bs, fox_w)
        colv = lambda a: a.reshape(bs, hg_heads, HG_DK, 1)
        ho_s, s_new = _hgrn_step(colv(tq), colv(tk), colv(tlf), tv.reshape(bs, hg_heads, 1, HG_DK), state_hg, l,
                                 _pick(bs, 8))
        x1s, qxs = _merge(fo_s, ho_s.reshape(bs, hg_w), tgate, xs, row(fox_gn[l]), row(hg_gn[l]), w_out_b,
                          row(ln_x[l]), w_xq_b, bs)
        ctx_s = _xattn_sample(tok3(qxs), memk, memv, l, _pick(bs, 8)).reshape(bs, x_w)
        x2s, h3s, pqs = _xo_peerq(x1s, ctx_s, w_xo_b, row(ln_ffn[l]), peer_wq_b, bs)
        xs = _peer(x2s, h3s, pqs, lw)
        outs["ks"].append(sk.reshape(bs, 1, fox_heads, HEAD_DIM))
        outs["vs"].append(sv.reshape(bs, 1, fox_heads, HEAD_DIM))
        outs["fs"].append(slf.reshape(bs, 1, fox_heads))
        outs["hs"].append(s_new)

    y_prompt = _final_norm(xp, row(ln_final), _pick(bp * seq, 512)).reshape(bp, seq, d_model)
    y_sample = _final_norm(xs, row(ln_final), bs).reshape(bs, 1, d_model)
    st_ = lambda k: jnp.stack(outs[k])
    return (y_prompt, y_sample, st_("kp"), st_("vp"), st_("fp"), st_("hp"), st_("mkp"), st_("mvp"),
            st_("ks"), st_("vs"), st_("fs"), st_("hs"))
```

```python
import functools
import math

import jax
import jax.numpy as jnp
from jax import lax
from jax.experimental import pallas as pl
from jax.experimental.pallas import tpu as pltpu

F32 = jnp.float32
BF16 = jnp.bfloat16
I32 = jnp.int32
EPS = 1e-6
HIGHEST = lax.Precision.HIGHEST
NEG_INF = float("-inf")

HEAD_DIM = 64
HG_DK = 64
HG_CHUNK = 64
HG_SUB = 16
X_HEAD_DIM = 128
PEER_TOPK = 16
LANES = 128
VMEM_LIMIT = 56 * 1024 * 1024

_NT = (((1,), (1,)), ((), ()))
_TN = (((0,), (0,)), ((), ()))


def _params(*sem):
    return pltpu.CompilerParams(dimension_semantics=sem, vmem_limit_bytes=VMEM_LIMIT)


def _rms(x, g):
    ms = jnp.mean(x * x, axis=-1, keepdims=True)
    return x * lax.rsqrt(ms + EPS) * g


def _log_sigmoid(x):
    return jnp.minimum(x, 0.0) - jnp.log1p(jnp.exp(-jnp.abs(x)))


def _dot(a, b):
    return jnp.dot(a, b, preferred_element_type=F32)


def _split3(x):
    hi = x.astype(BF16)
    r = x - hi.astype(F32)
    mid = r.astype(BF16)
    lo = (r - mid.astype(F32)).astype(BF16)
    return hi, mid, lo


def _prefix_rows(tril, x):
    return _dot(jnp.concatenate([tril] * 3, axis=1), jnp.concatenate(_split3(x), axis=0))


def _prefix_lanes(x, triu):
    return _dot(jnp.concatenate(_split3(x), axis=1), jnp.concatenate([triu] * 3, axis=0))


def _iota(shape, axis):
    return lax.broadcasted_iota(I32, shape, axis)


def _inproj_kernel(x_ref, g_ref, w_ref, wff_ref, bf_ref, lb_ref,
                   fq_ref, fk_ref, fv_ref, lf_ref, hq_ref, hk_ref, hv_ref, hlf_ref, gate_ref, *, fox_heads):
    h = _rms(x_ref[...], g_ref[...]).astype(BF16)
    w = w_ref.shape[1] // 7

    def mm(j):
        return _dot(h, w_ref[:, j * w:(j + 1) * w])

    fq_ref[...] = mm(0)
    fk_ref[...] = mm(1)
    fv_ref[...] = mm(2)
    ff = _dot(h, wff_ref[...])
    lf_ref[...] = _log_sigmoid(ff + bf_ref[...])[:, :fox_heads]
    hq_ref[...] = mm(3)
    z = mm(4)
    lb = lb_ref[...]
    a = jnp.log(lb)
    b = jnp.log1p(-lb) + _log_sigmoid(z)
    hlf_ref[...] = jnp.maximum(a, b) + jnp.log1p(jnp.exp(-jnp.abs(a - b)))
    hk_ref[...] = (1.0 - lb) * (1.0 / (1.0 + jnp.exp(z)))
    hv_ref[...] = mm(5)
    gate_ref[...] = mm(6)


def _inproj(x, g, w_main, w_ff, b_f, lb, tm, fox_heads):
    n, d = x.shape
    w = w_main.shape[1] // 7
    full = lambda a: pl.BlockSpec(a.shape, lambda i: (0,) * a.ndim)
    row = lambda c: pl.BlockSpec((tm, c), lambda i: (i, 0))
    outs = [jax.ShapeDtypeStruct((n, w), F32)] * 3 + [jax.ShapeDtypeStruct((n, fox_heads), F32)] + \
           [jax.ShapeDtypeStruct((n, w), F32)] * 5
    out_specs = [row(w)] * 3 + [row(fox_heads)] + [row(w)] * 5
    return pl.pallas_call(
        functools.partial(_inproj_kernel, fox_heads=fox_heads),
        grid=(n // tm,),
        in_specs=[row(d), full(g), full(w_main), full(w_ff), full(b_f), full(lb)],
        out_specs=out_specs, out_shape=outs,
        compiler_params=_params("parallel"), name="inproj",
    )(x, g, w_main, w_ff, b_f, lb)


def _cumsum_kernel(x_ref, o_ref, carry_ref):
    @pl.when(pl.program_id(0) == 0)
    def _():
        carry_ref[...] = jnp.zeros_like(carry_ref)

    x = x_ref[...]
    n = x.shape[1]
    tri = (_iota((n, n), 0) <= _iota((n, n), 1)).astype(BF16)
    c = _prefix_lanes(x, tri) + carry_ref[:, :1]
    o_ref[...] = c
    carry_ref[...] = jnp.broadcast_to(c[:, n - 1:n], carry_ref.shape)


def _cumsum_lanes(x, tc):
    r, t = x.shape
    return pl.pallas_call(
        _cumsum_kernel, grid=(t // tc,),
        in_specs=[pl.BlockSpec((r, tc), lambda i: (0, i))],
        out_specs=pl.BlockSpec((r, tc), lambda i: (0, i)),
        out_shape=jax.ShapeDtypeStruct((r, t), F32),
        scratch_shapes=[pltpu.VMEM((r, LANES), F32)],
        compiler_params=_params("arbitrary"), name="logf_cumsum",
    )(x)


def _fox_kernel(qt_ref, kt_ref, q_ref, k_ref, v_ref, cq_ref, ck_ref, o_ref, m_ref, cqr_ref, acc_ref, qh_ref, *,
                scale):
    qi = qt_ref[pl.program_id(2)]
    ki = kt_ref[pl.program_id(2)]
    tq = q_ref.shape[0]
    tk = k_ref.shape[0]
    lane = _iota((1, LANES), 1)
    log2e = math.log2(math.e)

    @pl.when(ki == 0)
    def _():
        m_ref[...] = jnp.full_like(m_ref, NEG_INF)
        acc_ref[...] = jnp.zeros_like(acc_ref)
        q = q_ref[...] * (scale * log2e)
        for hh in range(2):
            cqr_ref[hh] = jnp.broadcast_to(cq_ref[:, hh:hh + 1] * log2e, (tq, LANES))
            qh_ref[hh] = jnp.where((lane // HEAD_DIM) == hh, q, 0.0).astype(BF16)

    def step(masked):
        k = k_ref[...].astype(BF16)
        v = v_ref[...]
        ck = ck_ref[...] * log2e
        if masked:
            causal = (ki * tk + _iota((1, tk), 1)) <= (qi * tq + _iota((tq, 1), 0))
        for hh in range(2):
            vh = jnp.where((lane // HEAD_DIM) == hh, v, 1.0).astype(BF16)
            s = lax.dot_general(qh_ref[hh], k, _NT, preferred_element_type=F32) - ck[hh:hh + 1, :]
            if masked:
                s = jnp.where(causal, s, NEG_INF)
            m_prev = m_ref[hh]
            cq = cqr_ref[hh]
            m_new = jnp.maximum(m_prev, jnp.max(s, axis=1, keepdims=True) + cq)
            p = jnp.exp2(s - jnp.tile(m_new - cq, (1, tk // LANES)))
            acc_ref[hh] = jnp.exp2(m_prev - m_new) * acc_ref[hh] + _dot(p.astype(BF16), vh)
            m_ref[hh] = m_new

    @pl.when(ki < qi)
    def _():
        step(False)

    @pl.when(ki == qi)
    def _():
        step(True)
        a0 = acc_ref[0]
        a1 = acc_ref[1]
        o0 = a0 / pltpu.roll(a0, HEAD_DIM, axis=1)
        o1 = a1 / pltpu.roll(a1, HEAD_DIM, axis=1)
        o_ref[...] = jnp.where(lane < HEAD_DIM, o0, o1)


def _fox_prompt(q, k, v, cq, ck, batch, tq):
    n, w = q.shape
    t = n // batch
    pairs = w // LANES
    nq = t // tq
    kern = functools.partial(_fox_kernel, scale=HEAD_DIM ** -0.5)
    steps = [(i, j) for i in range(nq) for j in range(i + 1)]
    qi_tab = jnp.asarray([s[0] for s in steps], I32)
    ki_tab = jnp.asarray([s[1] for s in steps], I32)
    grid_spec = pltpu.PrefetchScalarGridSpec(
        num_scalar_prefetch=2, grid=(batch, pairs, len(steps)),
        in_specs=[
            pl.BlockSpec((tq, LANES), lambda b, g, s, qt, kt: (b * nq + qt[s], g)),
            pl.BlockSpec((tq, LANES), lambda b, g, s, qt, kt: (b * nq + kt[s], g)),
            pl.BlockSpec((tq, LANES), lambda b, g, s, qt, kt: (b * nq + kt[s], g)),
            pl.BlockSpec((None, tq, 2), lambda b, g, s, qt, kt: (g, b * nq + qt[s], 0)),
            pl.BlockSpec((None, 2, tq), lambda b, g, s, qt, kt: (b * pairs + g, 0, kt[s])),
        ],
        out_specs=pl.BlockSpec((tq, LANES), lambda b, g, s, qt, kt: (b * nq + qt[s], g)),
        scratch_shapes=[pltpu.VMEM((2, tq, LANES), F32)] * 3 + [pltpu.VMEM((2, tq, LANES), BF16)],
    )
    return pl.pallas_call(
        kern, grid_spec=grid_spec, out_shape=jax.ShapeDtypeStruct((n, w), F32),
        compiler_params=_params("parallel", "parallel", "arbitrary"), name="fox_prompt",
    )(qi_tab, ki_tab, q, k, v, cq, ck)


def _hgrn_kernel(q_ref, k_ref, v_ref, lf_ref, o_ref, st_ref, st_sc):
    n = pl.program_id(0)
    nb, c, w = q_ref.shape
    pairs = w // LANES

    @pl.when(n == 0)
    def _():
        st_sc[...] = jnp.zeros_like(st_sc)

    tril = (_iota((c, c), 1) <= _iota((c, c), 0)).astype(BF16)
    same_head = (_iota((LANES, LANES), 0) // HG_DK) == (_iota((LANES, LANES), 1) // HG_DK)
    seg = same_head.astype(BF16)
    sc = HG_SUB
    ti = _iota((sc, sc, LANES), 0)
    si = _iota((sc, sc, LANES), 1)
    first_head = _iota((1, LANES), 1) < HG_DK
    for bg in range(nb * pairs):
        r, g = divmod(bg, pairs)
        sl = slice(g * LANES, (g + 1) * LANES)
        q = q_ref[r, :, sl]
        k = k_ref[r, :, sl]
        v = v_ref[r, :, sl]
        vb = v.astype(BF16)
        b = _prefix_rows(tril, lf_ref[r, :, sl])
        st = st_sc[r, g]
        qe = (q * jnp.exp(b)).astype(BF16)
        o = lax.dot_general(qe, st.astype(BF16), _NT, preferred_element_type=F32)
        rows = []
        for i in range(c // sc):
            lo, hi = i * sc, (i + 1) * sc
            qi, ki, bi = q[lo:hi], k[lo:hi], b[lo:hi]
            decay = jnp.exp(jnp.where(si <= ti, bi[:, None, :] - bi[None, :, :], NEG_INF))
            p = (qi[:, None, :] * ki[None, :, :]) * decay
            a = _dot(p.reshape(sc * sc, LANES).astype(BF16), seg)
            oi = jnp.sum(a.reshape(sc, sc, LANES) * v[None, lo:hi, :], axis=1)
            if i > 0:
                br = b[lo - 1:lo, :]
                qt = qi * jnp.exp(bi - br)
                q2 = jnp.concatenate([jnp.where(first_head, qt, 0.0), jnp.where(first_head, 0.0, qt)], axis=0)
                ks = (k[:lo] * jnp.exp(br - b[:lo])).astype(BF16)
                a2 = lax.dot_general(q2.astype(BF16), ks, _NT, preferred_element_type=F32)
                o2 = _dot(a2.astype(BF16), vb[:lo])
                oi = oi + jnp.where(first_head, o2[:sc], o2[sc:])
            rows.append(oi)
        o_ref[r, :, sl] = o + jnp.concatenate(rows, axis=0)
        b_last = b[c - 1:c, :]
        kd = (k * jnp.exp(b_last - b)).astype(BF16)
        upd = lax.dot_general(vb, kd, _TN, preferred_element_type=F32)
        st_sc[r, g] = st * jnp.exp(b_last) + jnp.where(same_head, upd, 0.0)

    @pl.when(n == pl.num_programs(0) - 1)
    def _():
        st_ref[...] = st_sc[...]


def _hgrn_prompt(q, k, v, lf, batch):
    n, w = q.shape
    t = n // batch
    pairs = w // LANES
    rows3 = lambda a: a.reshape(batch, t, w)
    blk = pl.BlockSpec((batch, HG_CHUNK, w), lambda i: (0, i, 0))
    st_shape = (batch, pairs, LANES, LANES)
    o, st = pl.pallas_call(
        _hgrn_kernel, grid=(t // HG_CHUNK,),
        in_specs=[blk] * 4,
        out_specs=[blk, pl.BlockSpec(st_shape, lambda i: (0, 0, 0, 0))],
        out_shape=[jax.ShapeDtypeStruct((batch, t, w), F32), jax.ShapeDtypeStruct(st_shape, F32)],
        scratch_shapes=[pltpu.VMEM(st_shape, F32)],
        compiler_params=_params("arbitrary"), name="hgrn_prompt",
    )(rows3(q), rows3(k), rows3(v), rows3(lf))
    return o.reshape(n, w), st


def _head_rows(x_row, heads, width):
    w = x_row.shape[1]
    keep = (_iota((8, w), 1) // width) == _iota((8, w), 0)
    return jnp.where(keep, jnp.broadcast_to(x_row, (8, w)), 0.0)


def _expand_matrix(heads_pad, w, width, dtype):
    return ((_iota((heads_pad, w), 1) // width) == _iota((heads_pad, w), 0)).astype(dtype)


def _fox_decode_kernel(pt_ref, q_ref, kn_ref, vn_ref, lfn_ref, *refs, scale, n_pages):
    k_refs = refs[:n_pages]
    v_refs = refs[n_pages:2 * n_pages]
    lf_refs = refs[2 * n_pages:3 * n_pages]
    o_ref = refs[3 * n_pages]
    w, ps = k_refs[0].shape
    nh = lf_refs[0].shape[0]
    hd = w // nh
    heads = lambda x: x.reshape(nh, hd, x.shape[1])
    q = q_ref[...] * scale
    qb = jnp.broadcast_to(q, (w, ps))
    triu = (_iota((ps, ps), 0) <= _iota((ps, ps), 1)).astype(BF16)

    carry = jnp.zeros((nh, 1), F32)
    scores = []
    for p in range(n_pages):
        s = jnp.sum(heads(k_refs[p][...] * qb), axis=1)
        prefix = _prefix_lanes(lf_refs[p][...], triu) + carry
        carry = prefix[:, ps - 1:ps]
        scores.append(s - prefix)
    s_new = jnp.sum(heads(kn_ref[...] * q), axis=1) - (carry + lfn_ref[...])

    m_tile = scores[0]
    for s in scores[1:]:
        m_tile = jnp.maximum(m_tile, s)
    m = jnp.maximum(s_new, jnp.max(m_tile, axis=1, keepdims=True))
    e_new = jnp.exp(s_new - m)
    exps = [jnp.exp(s - m) for s in scores]
    l_tile = exps[0]
    for e in exps[1:]:
        l_tile = l_tile + e
    inv = 1.0 / (e_new + jnp.sum(l_tile, axis=1, keepdims=True))
    acc = jnp.zeros((w, ps), F32)
    for p in range(n_pages):
        pb = jnp.broadcast_to(exps[p][:, None, :], (nh, hd, ps)).reshape(w, ps)
        acc = acc + v_refs[p][...] * pb
    rep = lambda x: jnp.broadcast_to(x[:, None, :], (nh, hd, 1)).reshape(w, 1)
    o_ref[...] = (jnp.sum(acc, axis=1, keepdims=True) + rep(e_new) * vn_ref[...]) * rep(inv)


def _fox_decode(page_table, q, k_new, v_new, lf_new, cache_kt, cache_vt, cache_lft, layer):
    r, w, _ = q.shape
    nh = lf_new.shape[1]
    n_pages = page_table.shape[1]
    ps = cache_kt.shape[3]
    pt = page_table.reshape(-1)
    tok = lambda c: pl.BlockSpec((None, c, 1), lambda i, pt: (i, 0, 0))

    def page(p, rows):
        return pl.BlockSpec((None, None, rows, ps), lambda i, pt: (layer, pt[i * n_pages + p], 0, 0))

    pages = lambda rows: [page(p, rows) for p in range(n_pages)]
    grid_spec = pltpu.PrefetchScalarGridSpec(
        num_scalar_prefetch=1, grid=(r,),
        in_specs=[tok(w), tok(w), tok(w), tok(nh)] + pages(w) + pages(w) + pages(nh),
        out_specs=tok(w),
    )
    return pl.pallas_call(
        functools.partial(_fox_decode_kernel, scale=HEAD_DIM ** -0.5, n_pages=n_pages),
        grid_spec=grid_spec, out_shape=jax.ShapeDtypeStruct((r, w, 1), F32),
        compiler_params=_params("parallel"), name="fox_decode",
    )(pt, q, k_new, v_new, lf_new, *([cache_kt] * n_pages), *([cache_vt] * n_pages), *([cache_lft] * n_pages))


def _hgrn_step_kernel(q_ref, k_ref, lf_ref, v_ref, s_ref, o_ref, so_ref):
    q = q_ref[...]
    k = k_ref[...]
    f = jnp.exp(lf_ref[...])
    v = v_ref[...]
    s = s_ref[...]
    o_ref[...] = jnp.sum((q * f) * s, axis=2, keepdims=True) + jnp.sum(q * k, axis=2, keepdims=True) * v
    so_ref[...] = f * s + k * v


def _hgrn_step(q, k, lf, v, state, layer, rb):
    r, h, dk, _ = q.shape
    dv = v.shape[3]
    col = pl.BlockSpec((rb, h, dk, 1), lambda i: (i, 0, 0, 0))
    rowv = pl.BlockSpec((rb, h, 1, dv), lambda i: (i, 0, 0, 0))
    return pl.pallas_call(
        _hgrn_step_kernel, grid=(r // rb,),
        in_specs=[col, col, col, rowv, pl.BlockSpec((None, rb, h, dk, dv), lambda i: (layer, i, 0, 0, 0))],
        out_specs=[rowv, pl.BlockSpec((rb, h, dk, dv), lambda i: (i, 0, 0, 0))],
        out_shape=[jax.ShapeDtypeStruct((r, h, 1, dv), F32), jax.ShapeDtypeStruct((r, h, dk, dv), F32)],
        compiler_params=_params("parallel"), name="hgrn_step",
    )(q, k, lf, v, state)


def _head_mean_sq(y, width):
    w = y.shape[1]
    seg = ((_iota((w, w), 0) // width) == (_iota((w, w), 1) // width)).astype(BF16)
    sq = y * y
    hi = sq.astype(BF16)
    lo = (sq - hi.astype(F32)).astype(BF16)
    return (_dot(hi, seg) + _dot(lo, seg)) * (1.0 / width)


def _merge_kernel(fo_ref, ho_ref, gate_ref, x_ref, fgn_ref, hgn_ref, wo_ref, lnx_ref, wxq_ref, x1_ref, qx_ref):
    fo = fo_ref[...]
    ho = ho_ref[...]
    fw = fo.shape[1]
    fn = fo * lax.rsqrt(_head_mean_sq(fo, HEAD_DIM) + EPS) * fgn_ref[...]
    gate = gate_ref[...]
    hn = ho * lax.rsqrt(_head_mean_sq(ho, HEAD_DIM) + EPS) * hgn_ref[...] * (gate / (1.0 + jnp.exp(-gate)))
    y = _dot(fn.astype(BF16), wo_ref[:fw, :]) + _dot(hn.astype(BF16), wo_ref[fw:, :])
    x1 = x_ref[...] + y
    x1_ref[...] = x1
    qx_ref[...] = _dot(_rms(x1, lnx_ref[...]).astype(BF16), wxq_ref[...])


def _merge(fo, ho, gate, x, fgn, hgn, w_out, ln_x, w_xq, tm):
    n, d = x.shape
    xw = w_xq.shape[1]
    full = lambda a: pl.BlockSpec(a.shape, lambda i: (0,) * a.ndim)
    row = lambda c: pl.BlockSpec((tm, c), lambda i: (i, 0))
    return pl.pallas_call(
        _merge_kernel, grid=(n // tm,),
        in_specs=[row(fo.shape[1]), row(ho.shape[1]), row(gate.shape[1]), row(d),
                  full(fgn), full(hgn), full(w_out), full(ln_x), full(w_xq)],
        out_specs=[row(d), row(xw)],
        out_shape=[jax.ShapeDtypeStruct((n, d), F32), jax.ShapeDtypeStruct((n, xw), F32)],
        compiler_params=_params("parallel"), name="merge",
    )(fo, ho, gate, x, fgn, hgn, w_out, ln_x, w_xq)


def _memkv_kernel(m_ref, g_ref, wk_ref, wv_ref, k_ref, v_ref):
    m = _rms(m_ref[...], g_ref[...]).astype(BF16)
    k_ref[...] = _dot(m, wk_ref[...])
    v_ref[...] = _dot(m, wv_ref[...])


def _memkv(mem, g, wk, wv, tm):
    n, d = mem.shape
    xw = wk.shape[1]
    full = lambda a: pl.BlockSpec(a.shape, lambda i: (0,) * a.ndim)
    row = lambda c: pl.BlockSpec((tm, c), lambda i: (i, 0))
    return pl.pallas_call(
        _memkv_kernel, grid=(n // tm,),
        in_specs=[row(d), full(g), full(wk), full(wv)],
        out_specs=[row(xw), row(xw)],
        out_shape=[jax.ShapeDtypeStruct((n, xw), F32)] * 2,
        compiler_params=_params("parallel"), name="memkv",
    )(mem, g, wk, wv)


def _xattn_prompt_kernel(q_ref, mk_ref, mv_ref, o_ref, *, scale):
    heads = q_ref.shape[1] // X_HEAD_DIM
    for h in range(heads):
        sl = slice(h * X_HEAD_DIM, (h + 1) * X_HEAD_DIM)
        q = (q_ref[:, sl] * scale).astype(BF16)
        s = lax.dot_general(q, mk_ref[:, sl].astype(BF16), _NT, preferred_element_type=F32)
        e = jnp.exp(s - jnp.max(s, axis=1, keepdims=True))
        p = e / jnp.sum(e, axis=1, keepdims=True)
        o_ref[:, sl] = _dot(p.astype(BF16), mv_ref[:, sl].astype(BF16))


def _xattn_prompt(q, mk, mv, batch, tm):
    n, xw = q.shape
    t = n // batch
    nm = mk.shape[0] // batch
    nt = t // tm
    return pl.pallas_call(
        functools.partial(_xattn_prompt_kernel, scale=X_HEAD_DIM ** -0.5), grid=(batch, nt),
        in_specs=[pl.BlockSpec((tm, xw), lambda b, i: (b * nt + i, 0)),
                  pl.BlockSpec((nm, xw), lambda b, i: (b, 0)),
                  pl.BlockSpec((nm, xw), lambda b, i: (b, 0))],
        out_specs=pl.BlockSpec((tm, xw), lambda b, i: (b * nt + i, 0)),
        out_shape=jax.ShapeDtypeStruct((n, xw), F32),
        compiler_params=_params("parallel", "parallel"), name="xattn_prompt",
    )(q, mk, mv)


def _xattn_sample_kernel(q_ref, mk_ref, mv_ref, o_ref, *, scale):
    rb, _, w = q_ref.shape
    heads = w // X_HEAD_DIM
    expand = _expand_matrix(8, w, X_HEAD_DIM, BF16)
    valid = _iota((1, 8), 1) < heads
    for r in range(rb):
        qrows = _head_rows(q_ref[r] * scale, heads, X_HEAD_DIM).astype(BF16)
        s = lax.dot_general(mk_ref[r].astype(BF16), qrows, _NT, preferred_element_type=F32)
        e = jnp.exp(s - jnp.max(s, axis=0, keepdims=True))
        p = jnp.where(valid, e / jnp.sum(e, axis=0, keepdims=True), 0.0)
        pe = _dot(p.astype(BF16), expand)
        o_ref[r] = jnp.sum(pe * mv_ref[r], axis=0, keepdims=True)


def _xattn_sample(q, mem_k, mem_v, layer, rb):
    r, _, w = q.shape
    nm = mem_k.shape[2]
    tok = pl.BlockSpec((rb, 1, w), lambda i: (i, 0, 0))
    mem = pl.BlockSpec((None, rb, nm, w), lambda i: (layer, i, 0, 0))
    return pl.pallas_call(
        functools.partial(_xattn_sample_kernel, scale=X_HEAD_DIM ** -0.5), grid=(r // rb,),
        in_specs=[tok, mem, mem], out_specs=tok,
        out_shape=jax.ShapeDtypeStruct((r, 1, w), F32),
        compiler_params=_params("parallel"), name="xattn_sample",
    )(q, mem_k, mem_v)


def _xo_peerq_kernel(x1_ref, ctx_ref, wxo_ref, lnf_ref, wq_ref, x2_ref, h_ref, pq_ref):
    x2 = x1_ref[...] + _dot(ctx_ref[...].astype(BF16), wxo_ref[...])
    x2_ref[...] = x2
    h = _rms(x2, lnf_ref[...]).astype(BF16)
    h_ref[...] = h
    pq_ref[...] = _dot(h, wq_ref[...])


def _xo_peerq(x1, ctx, w_xo, ln_ffn, peer_wq, tm):
    n, d = x1.shape
    qw = peer_wq.shape[1]
    full = lambda a: pl.BlockSpec(a.shape, lambda i: (0,) * a.ndim)
    row = lambda c: pl.BlockSpec((tm, c), lambda i: (i, 0))
    return pl.pallas_call(
        _xo_peerq_kernel, grid=(n // tm,),
        in_specs=[row(d), row(ctx.shape[1]), full(w_xo), full(ln_ffn), full(peer_wq)],
        out_specs=[row(d), row(d), row(qw)],
        out_shape=[jax.ShapeDtypeStruct((n, d), F32), jax.ShapeDtypeStruct((n, d), BF16),
                   jax.ShapeDtypeStruct((n, qw), F32)],
        compiler_params=_params("parallel"), name="xo_peerq",
    )(x1, ctx, w_xo, ln_ffn, peer_wq)


def _top16(s):
    kk = s.shape[0]
    kio = _iota(s.shape, 0)
    vals, idxs = [], []
    for _ in range(PEER_TOPK):
        m = jnp.max(s, axis=0, keepdims=True)
        idx = jnp.min(jnp.where(s == m, kio, kk), axis=0, keepdims=True)
        s = jnp.where(kio == idx, NEG_INF, s)
        vals.append(m)
        idxs.append(idx)
    return jnp.concatenate(vals, axis=0), jnp.concatenate(idxs, axis=0)


def _route_kernel(pq_ref, sk_ref, i_ref, j_ref, g_ref):
    half = sk_ref.shape[2]
    tt = min(2 * LANES, pq_ref.shape[0])
    k = PEER_TOPK

    def sub_tile(n, carry):
        r0 = pl.multiple_of(n * tt, tt)
        sv, si = [], []
        for p in range(2):
            q = pq_ref[pl.ds(r0, tt), p * half:(p + 1) * half].astype(BF16)
            st = lax.dot_general(sk_ref[p], q, _NT, preferred_element_type=F32)
            v, i = _top16(st)
            sv.append(v)
            si.append(i)
        sub = 8
        rows8 = _iota((sub, tt), 0)
        cands, ics, jcs = [sv[0][0:1, :] + sv[1]], [jnp.broadcast_to(si[0][0:1, :], (k, tt))], [si[1]]
        for a in range(1, sub):
            nb = k // (a + 1)
            cands.append(jnp.where(rows8 < nb, sv[0][a:a + 1, :] + sv[1][:sub, :], NEG_INF))
            ics.append(jnp.broadcast_to(si[0][a:a + 1, :], (sub, tt)))
            jcs.append(si[1][:sub, :])
        cands.append(sv[0][sub:, :] + sv[1][0:1, :])
        ics.append(si[0][sub:, :])
        jcs.append(jnp.broadcast_to(si[1][0:1, :], (k - sub, tt)))
        cand = jnp.concatenate(cands, axis=0)
        ic = jnp.concatenate(ics, axis=0)
        jc = jnp.concatenate(jcs, axis=0)
        pio = _iota(cand.shape, 0)
        tv, ti, tj = [], [], []
        for _ in range(k):
            m = jnp.max(cand, axis=0, keepdims=True)
            pos = jnp.min(jnp.where(cand == m, pio, cand.shape[0]), axis=0, keepdims=True)
            sel = pio == pos
            ti.append(jnp.sum(jnp.where(sel, ic, 0), axis=0, keepdims=True))
            tj.append(jnp.sum(jnp.where(sel, jc, 0), axis=0, keepdims=True))
            cand = jnp.where(sel, NEG_INF, cand)
            tv.append(m)
        top = jnp.concatenate(tv, axis=0)
        e = jnp.exp(top - top[0:1, :])
        g_ref[:, pl.ds(r0, tt)] = e / jnp.sum(e, axis=0, keepdims=True)
        i_ref[:, pl.ds(r0, tt)] = jnp.concatenate(ti, axis=0)
        j_ref[:, pl.ds(r0, tt)] = jnp.concatenate(tj, axis=0)
        return carry

    lax.fori_loop(0, pq_ref.shape[0] // tt, sub_tile, 0)


def _route(pq, subkeys, tt):
    n = pq.shape[0]
    hp, keys, half = subkeys.shape
    heads = hp // 2
    k = PEER_TOPK
    out = pl.BlockSpec((k, tt), lambda i, h: (h, i))
    return pl.pallas_call(
        _route_kernel, grid=(n // tt, heads),
        in_specs=[pl.BlockSpec((tt, 2 * half), lambda i, h: (i, h)),
                  pl.BlockSpec((2, keys, half), lambda i, h: (h, 0, 0))],
        out_specs=[out, out, out],
        out_shape=[jax.ShapeDtypeStruct((heads * k, n), I32), jax.ShapeDtypeStruct((heads * k, n), I32),
                   jax.ShapeDtypeStruct((heads * k, n), F32)],
        compiler_params=_params("parallel", "parallel"), name="peer_route",
    )(pq, subkeys)


def _peer_u_kernel(h_ref, ut_ref, i_ref, j_ref, a_ref, *, ib):
    step = pl.program_id(1)

    @pl.when(step == 0)
    def _():
        a_ref[...] = jnp.zeros_like(a_ref)

    isel = i_ref[...]
    jsel = j_ref[...]
    acc = a_ref[...]
    nj = jsel.shape[1]
    h = h_ref[...]
    group = 4
    for g0 in range(0, ib, group):
        a_all = _dot(h, ut_ref[:, g0 * nj:(g0 + group) * nj])
        for ii in range(group):
            a = a_all[:, ii * nj:(ii + 1) * nj]
            acc = jnp.where(isel == step * ib + g0 + ii, jnp.take_along_axis(a, jsel, axis=1), acc)
    a_ref[...] = acc


def _peer_u(h, ut, isel, jsel, tt, ib):
    n, d = h.shape
    slots = isel.shape[1]
    nblk = ut.shape[1] // (slots * ib)
    tok = lambda c: pl.BlockSpec((tt, c), lambda t, e: (t, 0))
    return pl.pallas_call(
        functools.partial(_peer_u_kernel, ib=ib), grid=(n // tt, nblk),
        in_specs=[tok(d), pl.BlockSpec((d, slots * ib), lambda t, e: (0, e)), tok(slots), tok(slots)],
        out_specs=tok(slots), out_shape=jax.ShapeDtypeStruct((n, slots), F32),
        compiler_params=_params("parallel", "arbitrary"), name="peer_u",
    )(h, ut, isel, jsel)


def _peer_v_kernel(i_ref, j_ref, g_ref, a_ref, x_ref, v_ref, o_ref, z_sc, zs_sc, *, ib, stride):
    step = pl.program_id(1)
    tt, slots = i_ref.shape

    @pl.when(step == 0)
    def _():
        a = a_ref[...]
        z_sc[...] = g_ref[...] * (0.5 * a * (1.0 + lax.erf(a * (2.0 ** -0.5))))

        def scatter(t, carry):
            irow = i_ref[pl.ds(t, 1), :]
            jrow = j_ref[pl.ds(t, 1), :]
            zrow = z_sc[pl.ds(t, 1), :]
            io = _iota((slots, slots), 0)
            zit = jnp.where(irow == io, zrow, 0.0).astype(BF16)
            oht = jnp.where(jrow == io, 1.0, 0.0).astype(BF16)
            zs_sc[pl.ds(t, slots, stride=stride), :] = lax.dot_general(zit, oht, _NT, preferred_element_type=F32)
            return carry

        lax.fori_loop(0, tt, scatter, 0, unroll=8)
        o_ref[...] = x_ref[...]

    base = step * ib
    zblk = jnp.concatenate(
        [zs_sc[pl.ds(pl.multiple_of((base + ii) * stride, 8), tt), :].astype(BF16) for ii in range(ib)], axis=1)
    o_ref[...] += _dot(zblk, v_ref[...])


def _peer_v(isel, jsel, gates, act, x, v, tt, ib):
    n, d = x.shape
    slots = isel.shape[1]
    nblk = v.shape[0] // (slots * ib)
    stride = tt + 8
    tok = lambda c: pl.BlockSpec((tt, c), lambda t, e: (t, 0))
    return pl.pallas_call(
        functools.partial(_peer_v_kernel, ib=ib, stride=stride), grid=(n // tt, nblk),
        in_specs=[tok(slots)] * 4 + [tok(d), pl.BlockSpec((slots * ib, d), lambda t, e: (e, 0))],
        out_specs=tok(d), out_shape=jax.ShapeDtypeStruct((n, d), F32),
        scratch_shapes=[pltpu.VMEM((tt, slots), F32), pltpu.VMEM((slots * stride, slots), F32)],
        compiler_params=_params("parallel", "arbitrary"), name="peer_v",
    )(isel, jsel, gates, act, x, v)


def _final_kernel(x_ref, g_ref, o_ref):
    o_ref[...] = _rms(x_ref[...], g_ref[...])


def _final_norm(x, g, tm):
    n, d = x.shape
    return pl.pallas_call(
        _final_kernel, grid=(n // tm,),
        in_specs=[pl.BlockSpec((tm, d), lambda i: (i, 0)), pl.BlockSpec((1, d), lambda i: (0, 0))],
        out_specs=pl.BlockSpec((tm, d), lambda i: (i, 0)), out_shape=jax.ShapeDtypeStruct((n, d), F32),
        compiler_params=_params("parallel"), name="final_norm",
    )(x, g)


def _pick(n, pref):
    return pref if n % pref == 0 else n


def _peer(x2, h3, pq, lw):
    n = x2.shape[0]
    it, jt, gt = _route(pq, lw["subkeys"], _pick(n, 512))
    isel, jsel, gates = it.T, jt.T, gt.T
    act = _peer_u(h3, lw["ut"], isel, jsel, _pick(n, 512), 16)
    return _peer_v(isel, jsel, gates, act, x2, lw["v"], _pick(n, 512), 16)


def kernel(x_prompt, x_sample, cache_k, cache_v, cache_logf, state_hg, cache_mem_k, cache_mem_v, page_table, mem_prompt, ln_mix, w_in, b_fox_f, hg_lb, fox_gn, hg_gn, w_out, ln_x, ln_mem, w_xq, w_xk, w_xv, w_xo, ln_ffn, peer_wq, peer_subkeys, peer_u, peer_v, ln_final):
    depth, d_model, _ = w_in.shape
    bp, seq, _ = x_prompt.shape
    bs = x_sample.shape[0]
    fox_heads = b_fox_f.shape[1]
    fox_w = fox_gn.shape[1]
    hg_w = hg_gn.shape[1]
    hg_heads = hg_w // HG_DK
    n_phys, page = cache_k.shape[1], cache_k.shape[2]
    n_mem = mem_prompt.shape[1]
    x_w = w_xq.shape[2]
    x_heads = x_w // X_HEAD_DIM
    p_heads = peer_subkeys.shape[1]
    p_keys, p_half = peer_subkeys.shape[3], peer_subkeys.shape[4]
    pairs = fox_w // LANES

    lb_all = jnp.cumsum(jax.nn.softmax(hg_lb.astype(F32), axis=0), axis=0)
    row = lambda a: a.reshape(1, -1).astype(F32)

    xp = x_prompt.reshape(bp * seq, d_model)
    xs = x_sample.reshape(bs, d_model)
    mem = mem_prompt.reshape(bp * n_mem, d_model)
    ck_t = cache_k.transpose(0, 1, 3, 4, 2).reshape(depth, n_phys, fox_w, page)
    cv_t = cache_v.transpose(0, 1, 3, 4, 2).reshape(depth, n_phys, fox_w, page)
    clf_t = cache_logf.transpose(0, 1, 3, 2)
    memk = cache_mem_k.reshape(depth, bs, n_mem, x_w)
    memv = cache_mem_v.reshape(depth, bs, n_mem, x_w)

    outs = {k: [] for k in ("kp", "vp", "fp", "hp", "mkp", "mvp", "ks", "vs", "fs", "hs")}
    for l in range(depth):
        wl = w_in[l]
        c0 = 3 * fox_w
        lw = dict(
            w_main=jnp.concatenate([wl[:, :c0], wl[:, c0 + fox_heads:]], axis=1).astype(BF16),
            w_ff=jnp.pad(wl[:, c0:c0 + fox_heads], ((0, 0), (0, LANES - fox_heads))).astype(BF16),
            b_f=jnp.pad(row(b_fox_f[l]), ((0, 0), (0, LANES - fox_heads))),
            lb=row(lb_all[l] - lb_all[0]),
            subkeys=peer_subkeys[l].reshape(p_heads * 2, p_keys, p_half).astype(BF16),
            ut=peer_u[l].astype(BF16).T, v=peer_v[l].astype(BF16),
        )
        w_out_b, w_xq_b, w_xo_b = w_out[l].astype(BF16), w_xq[l].astype(BF16), w_xo[l].astype(BF16)
        peer_wq_b = peer_wq[l].astype(BF16)

        tm = _pick(bp * seq, 512)
        fq, fk, fv, flf, hq, hk, hv, hlf, hgate = _inproj(xp, row(ln_mix[l]), lw["w_main"], lw["w_ff"], lw["b_f"],
                                                           lw["lb"], tm, fox_heads)
        lf_t = flf.reshape(bp, seq, fox_heads).transpose(0, 2, 1).reshape(bp * fox_heads, seq)
        c_t = _cumsum_lanes(lf_t, _pick(seq, 512))
        ck = c_t.reshape(bp * pairs, 2, seq)
        cq = c_t.reshape(bp, pairs, 2, seq).transpose(1, 0, 3, 2).reshape(pairs, bp * seq, 2)
        fo = _fox_prompt(fq, fk, fv, cq, ck, bp, _pick(seq, 512))
        ho, st = _hgrn_prompt(hq, hk, hv, hlf, bp)
        x1, qx = _merge(fo, ho, hgate, xp, row(fox_gn[l]), row(hg_gn[l]), w_out_b, row(ln_x[l]), w_xq_b, tm)
        mk, mv = _memkv(mem, row(ln_mem[l]), w_xk[l].astype(BF16), w_xv[l].astype(BF16), n_mem)
        ctx = _xattn_prompt(qx, mk, mv, bp, tm)
        x2, h3, pq = _xo_peerq(x1, ctx, w_xo_b, row(ln_ffn[l]), peer_wq_b, tm)
        xp = _peer(x2, h3, pq, lw)
        outs["kp"].append(fk.reshape(bp, seq, fox_heads, HEAD_DIM))
        outs["vp"].append(fv.reshape(bp, seq, fox_heads, HEAD_DIM))
        outs["fp"].append(flf.reshape(bp, seq, fox_heads))
        st5 = st.reshape(bp, pairs, 2, HG_DK, 2, HG_DK)
        s_heads = jnp.stack([st5[:, :, 0, :, 0, :], st5[:, :, 1, :, 1, :]], axis=2)
        outs["hp"].append(s_heads.reshape(bp, hg_heads, HG_DK, HG_DK).transpose(0, 1, 3, 2))
        outs["mkp"].append(mk.reshape(bp, n_mem, x_heads, X_HEAD_DIM))
        outs["mvp"].append(mv.reshape(bp, n_mem, x_heads, X_HEAD_DIM))

        sq, sk, sv, slf, tq, tk, tv, tlf, tgate = _inproj(xs, row(ln_mix[l]), lw["w_main"], lw["w_ff"], lw["b_f"],
                                                           lw["lb"], bs, fox_heads)
        tok3 = lambda a: a.reshape(bs, 1, a.shape[1])
        colm = lambda a: a.reshape(bs, a.shape[1], 1)
        fo_s = _fox_decode(page_table, colm(sq), colm(sk), colm(sv), colm(slf), ck_t, cv_t, clf_t,
                           l).reshape(bs, fox_w)
        colv = lambda a: a.reshape(bs, hg_heads, HG_DK, 1)
        ho_s, s_new = _hgrn_step(colv(tq), colv(tk), colv(tlf), tv.reshape(bs, hg_heads, 1, HG_DK), state_hg, l,
                                 _pick(bs, 8))
        x1s, qxs = _merge(fo_s, ho_s.reshape(bs, hg_w), tgate, xs, row(fox_gn[l]), row(hg_gn[l]), w_out_b,
                          row(ln_x[l]), w_xq_b, bs)
        ctx_s = _xattn_sample(tok3(qxs), memk, memv, l, _pick(bs, 8)).reshape(bs, x_w)
        x2s, h3s, pqs = _xo_peerq(x1s, ctx_s, w_xo_b, row(ln_ffn[l]), peer_wq_b, bs)
        xs = _peer(x2s, h3s, pqs, lw)
        outs["ks"].append(sk.reshape(bs, 1, fox_heads, HEAD_DIM))
        outs["vs"].append(sv.reshape(bs, 1, fox_heads, HEAD_DIM))
        outs["fs"].append(slf.reshape(bs, 1, fox_heads))
        outs["hs"].append(s_new)

    y_prompt = _final_norm(xp, row(ln_final), _pick(bp * seq, 512)).reshape(bp, seq, d_model)
    y_sample = _final_norm(xs, row(ln_final), bs).reshape(bs, 1, d_model)
    st_ = lambda k: jnp.stack(outs[k])
    return (y_prompt, y_sample, st_("kp"), st_("vp"), st_("fp"), st_("hp"), st_("mkp"), st_("mvp"),
            st_("ks"), st_("vs"), st_("fs"), st_("hs"))
```

```python
import functools
import math

import jax
import jax.numpy as jnp
from jax import lax
from jax.experimental import pallas as pl
from jax.experimental.pallas import tpu as pltpu

F32 = jnp.float32
BF16 = jnp.bfloat16
I32 = jnp.int32
EPS = 1e-6
HIGHEST = lax.Precision.HIGHEST
NEG_INF = float("-inf")

HEAD_DIM = 64
HG_DK = 64
HG_CHUNK = 64
HG_SUB = 16
X_HEAD_DIM = 128
PEER_TOPK = 16
LANES = 128
VMEM_LIMIT = 56 * 1024 * 1024

_NT = (((1,), (1,)), ((), ()))
_TN = (((0,), (0,)), ((), ()))


def _params(*sem):
    return pltpu.CompilerParams(dimension_semantics=sem, vmem_limit_bytes=VMEM_LIMIT)


def _rms(x, g):
    ms = jnp.mean(x * x, axis=-1, keepdims=True)
    return x * lax.rsqrt(ms + EPS) * g


def _log_sigmoid(x):
    return jnp.minimum(x, 0.0) - jnp.log1p(jnp.exp(-jnp.abs(x)))


def _dot(a, b):
    return jnp.dot(a, b, preferred_element_type=F32)


def _split3(x):
    hi = x.astype(BF16)
    r = x - hi.astype(F32)
    mid = r.astype(BF16)
    lo = (r - mid.astype(F32)).astype(BF16)
    return hi, mid, lo


def _prefix_rows(tril, x):
    return _dot(jnp.concatenate([tril] * 3, axis=1), jnp.concatenate(_split3(x), axis=0))


def _prefix_lanes(x, triu):
    return _dot(jnp.concatenate(_split3(x), axis=1), jnp.concatenate([triu] * 3, axis=0))


def _iota(shape, axis):
    return lax.broadcasted_iota(I32, shape, axis)


def _inproj_kernel(x_ref, g_ref, w_ref, wff_ref, bf_ref, lb_ref,
                   fq_ref, fk_ref, fv_ref, lf_ref, hq_ref, hk_ref, hv_ref, hlf_ref, gate_ref, *, fox_heads):
    h = _rms(x_ref[...], g_ref[...]).astype(BF16)
    w = w_ref.shape[1] // 7

    def mm(j):
        return _dot(h, w_ref[:, j * w:(j + 1) * w])

    fq_ref[...] = mm(0)
    fk_ref[...] = mm(1)
    fv_ref[...] = mm(2)
    ff = _dot(h, wff_ref[...])
    lf_ref[...] = _log_sigmoid(ff + bf_ref[...])[:, :fox_heads]
    hq_ref[...] = mm(3)
    z = mm(4)
    lb = lb_ref[...]
    a = jnp.log(lb)
    b = jnp.log1p(-lb) + _log_sigmoid(z)
    hlf_ref[...] = jnp.maximum(a, b) + jnp.log1p(jnp.exp(-jnp.abs(a - b)))
    hk_ref[...] = (1.0 - lb) * (1.0 / (1.0 + jnp.exp(z)))
    hv_ref[...] = mm(5)
    gate_ref[...] = mm(6)


def _inproj(x, g, w_main, w_ff, b_f, lb, tm, fox_heads):
    n, d = x.shape
    w = w_main.shape[1] // 7
    full = lambda a: pl.BlockSpec(a.shape, lambda i: (0,) * a.ndim)
    row = lambda c: pl.BlockSpec((tm, c), lambda i: (i, 0))
    outs = [jax.ShapeDtypeStruct((n, w), F32)] * 3 + [jax.ShapeDtypeStruct((n, fox_heads), F32)] + \
           [jax.ShapeDtypeStruct((n, w), F32)] * 5
    out_specs = [row(w)] * 3 + [row(fox_heads)] + [row(w)] * 5
    return pl.pallas_call(
        functools.partial(_inproj_kernel, fox_heads=fox_heads),
        grid=(n // tm,),
        in_specs=[row(d), full(g), full(w_main), full(w_ff), full(b_f), full(lb)],
        out_specs=out_specs, out_shape=outs,
        compiler_params=_params("parallel"), name="inproj",
    )(x, g, w_main, w_ff, b_f, lb)


def _cumsum_kernel(x_ref, o_ref, carry_ref):
    @pl.when(pl.program_id(0) == 0)
    def _():
        carry_ref[...] = jnp.zeros_like(carry_ref)

    x = x_ref[...]
    n = x.shape[1]
    tri = (_iota((n, n), 0) <= _iota((n, n), 1)).astype(BF16)
    c = _prefix_lanes(x, tri) + carry_ref[:, :1]
    o_ref[...] = c
    carry_ref[...] = jnp.broadcast_to(c[:, n - 1:n], carry_ref.shape)


def _cumsum_lanes(x, tc):
    r, t = x.shape
    return pl.pallas_call(
        _cumsum_kernel, grid=(t // tc,),
        in_specs=[pl.BlockSpec((r, tc), lambda i: (0, i))],
        out_specs=pl.BlockSpec((r, tc), lambda i: (0, i)),
        out_shape=jax.ShapeDtypeStruct((r, t), F32),
        scratch_shapes=[pltpu.VMEM((r, LANES), F32)],
        compiler_params=_params("arbitrary"), name="logf_cumsum",
    )(x)


def _fox_kernel(qt_ref, kt_ref, q_ref, k_ref, v_ref, cq_ref, ck_ref, o_ref, m_ref, cqr_ref, acc_ref, qh_ref, *,
                scale):
    qi = qt_ref[pl.program_id(2)]
    ki = kt_ref[pl.program_id(2)]
    tq = q_ref.shape[0]
    tk = k_ref.shape[0]
    lane = _iota((1, LANES), 1)
    log2e = math.log2(math.e)

    @pl.when(ki == 0)
    def _():
        m_ref[...] = jnp.full_like(m_ref, NEG_INF)
        acc_ref[...] = jnp.zeros_like(acc_ref)
        q = q_ref[...] * (scale * log2e)
        for hh in range(2):
            cqr_ref[hh] = jnp.broadcast_to(cq_ref[:, hh:hh + 1] * log2e, (tq, LANES))
            qh_ref[hh] = jnp.where((lane // HEAD_DIM) == hh, q, 0.0).astype(BF16)

    def step(masked):
        k = k_ref[...].astype(BF16)
        v = v_ref[...]
        ck = ck_ref[...] * log2e
        if masked:
            causal = (ki * tk + _iota((1, tk), 1)) <= (qi * tq + _iota((tq, 1), 0))
        for hh in range(2):
            vh = jnp.where((lane // HEAD_DIM) == hh, v, 1.0).astype(BF16)
            s = lax.dot_general(qh_ref[hh], k, _NT, preferred_element_type=F32) - ck[hh:hh + 1, :]
            if masked:
                s = jnp.where(causal, s, NEG_INF)
            m_prev = m_ref[hh]
            cq = cqr_ref[hh]
            m_new = jnp.maximum(m_prev, jnp.max(s, axis=1, keepdims=True) + cq)
            p = jnp.exp2(s - jnp.tile(m_new - cq, (1, tk // LANES)))
            acc_ref[hh] = jnp.exp2(m_prev - m_new) * acc_ref[hh] + _dot(p.astype(BF16), vh)
            m_ref[hh] = m_new

    @pl.when(ki < qi)
    def _():
        step(False)

    @pl.when(ki == qi)
    def _():
        step(True)
        a0 = acc_ref[0]
        a1 = acc_ref[1]
        o0 = a0 / pltpu.roll(a0, HEAD_DIM, axis=1)
        o1 = a1 / pltpu.roll(a1, HEAD_DIM, axis=1)
        o_ref[...] = jnp.where(lane < HEAD_DIM, o0, o1)


def _fox_prompt(q, k, v, cq, ck, batch, tq):
    n, w = q.shape
    t = n // batch
    pairs = w // LANES
    nq = t // tq
    kern = functools.partial(_fox_kernel, scale=HEAD_DIM ** -0.5)
    steps = [(i, j) for i in range(nq) for j in range(i + 1)]
    qi_tab = jnp.asarray([s[0] for s in steps], I32)
    ki_tab = jnp.asarray([s[1] for s in steps], I32)
    grid_spec = pltpu.PrefetchScalarGridSpec(
        num_scalar_prefetch=2, grid=(batch, pairs, len(steps)),
        in_specs=[
            pl.BlockSpec((tq, LANES), lambda b, g, s, qt, kt: (b * nq + qt[s], g)),
            pl.BlockSpec((tq, LANES), lambda b, g, s, qt, kt: (b * nq + kt[s], g)),
            pl.BlockSpec((tq, LANES), lambda b, g, s, qt, kt: (b * nq + kt[s], g)),
            pl.BlockSpec((None, tq, 2), lambda b, g, s, qt, kt: (g, b * nq + qt[s], 0)),
            pl.BlockSpec((None, 2, tq), lambda b, g, s, qt, kt: (b * pairs + g, 0, kt[s])),
        ],
        out_specs=pl.BlockSpec((tq, LANES), lambda b, g, s, qt, kt: (b * nq + qt[s], g)),
        scratch_shapes=[pltpu.VMEM((2, tq, LANES), F32)] * 3 + [pltpu.VMEM((2, tq, LANES), BF16)],
    )
    return pl.pallas_call(
        kern, grid_spec=grid_spec, out_shape=jax.ShapeDtypeStruct((n, w), F32),
        compiler_params=_params("parallel", "parallel", "arbitrary"), name="fox_prompt",
    )(qi_tab, ki_tab, q, k, v, cq, ck)


def _hgrn_kernel(q_ref, k_ref, v_ref, lf_ref, o_ref, st_ref, st_sc):
    n = pl.program_id(0)
    nb, c, w = q_ref.shape
    pairs = w // LANES

    @pl.when(n == 0)
    def _():
        st_sc[...] = jnp.zeros_like(st_sc)

    tril = (_iota((c, c), 1) <= _iota((c, c), 0)).astype(BF16)
    same_head = (_iota((LANES, LANES), 0) // HG_DK) == (_iota((LANES, LANES), 1) // HG_DK)
    seg = same_head.astype(BF16)
    sc = HG_SUB
    ti = _iota((sc, sc, LANES), 0)
    si = _iota((sc, sc, LANES), 1)
    first_head = _iota((1, LANES), 1) < HG_DK
    for bg in range(nb * pairs):
        r, g = divmod(bg, pairs)
        sl = slice(g * LANES, (g + 1) * LANES)
        q = q_ref[r, :, sl]
        k = k_ref[r, :, sl]
        v = v_ref[r, :, sl]
        vb = v.astype(BF16)
        b = _prefix_rows(tril, lf_ref[r, :, sl])
        st = st_sc[r, g]
        qe = (q * jnp.exp(b)).astype(BF16)
        o = lax.dot_general(qe, st.astype(BF16), _NT, preferred_element_type=F32)
        rows = []
        for i in range(c // sc):
            lo, hi = i * sc, (i + 1) * sc
            qi, ki, bi = q[lo:hi], k[lo:hi], b[lo:hi]
            decay = jnp.exp(jnp.where(si <= ti, bi[:, None, :] - bi[None, :, :], NEG_INF))
            p = (qi[:, None, :] * ki[None, :, :]) * decay
            a = _dot(p.reshape(sc * sc, LANES).astype(BF16), seg)
            oi = jnp.sum(a.reshape(sc, sc, LANES) * v[None, lo:hi, :], axis=1)
            if i > 0:
                br = b[lo - 1:lo, :]
                qt = qi * jnp.exp(bi - br)
                q2 = jnp.concatenate([jnp.where(first_head, qt, 0.0), jnp.where(first_head, 0.0, qt)], axis=0)
                ks = (k[:lo] * jnp.exp(br - b[:lo])).astype(BF16)
                a2 = lax.dot_general(q2.astype(BF16), ks, _NT, preferred_element_type=F32)
                o2 = _dot(a2.astype(BF16), vb[:lo])
                oi = oi + jnp.where(first_head, o2[:sc], o2[sc:])
            rows.append(oi)
        o_ref[r, :, sl] = o + jnp.concatenate(rows, axis=0)
        b_last = b[c - 1:c, :]
        kd = (k * jnp.exp(b_last - b)).astype(BF16)
        upd = lax.dot_general(vb, kd, _TN, preferred_element_type=F32)
        st_sc[r, g] = st * jnp.exp(b_last) + jnp.where(same_head, upd, 0.0)

    @pl.when(n == pl.num_programs(0) - 1)
    def _():
        st_ref[...] = st_sc[...]


def _hgrn_prompt(q, k, v, lf, batch):
    n, w = q.shape
    t = n // batch
    pairs = w // LANES
    rows3 = lambda a: a.reshape(batch, t, w)
    blk = pl.BlockSpec((batch, HG_CHUNK, w), lambda i: (0, i, 0))
    st_shape = (batch, pairs, LANES, LANES)
    o, st = pl.pallas_call(
        _hgrn_kernel, grid=(t // HG_CHUNK,),
        in_specs=[blk] * 4,
        out_specs=[blk, pl.BlockSpec(st_shape, lambda i: (0, 0, 0, 0))],
        out_shape=[jax.ShapeDtypeStruct((batch, t, w), F32), jax.ShapeDtypeStruct(st_shape, F32)],
        scratch_shapes=[pltpu.VMEM(st_shape, F32)],
        compiler_params=_params("arbitrary"), name="hgrn_prompt",
    )(rows3(q), rows3(k), rows3(v), rows3(lf))
    return o.reshape(n, w), st


def _head_rows(x_row, heads, width):
    w = x_row.shape[1]
    keep = (_iota((8, w), 1) // width) == _iota((8, w), 0)
    return jnp.where(keep, jnp.broadcast_to(x_row, (8, w)), 0.0)


def _expand_matrix(heads_pad, w, width, dtype):
    return ((_iota((heads_pad, w), 1) // width) == _iota((heads_pad, w), 0)).astype(dtype)


def _fox_decode_kernel(pt_ref, q_ref, kn_ref, vn_ref, lfn_ref, *refs, scale, n_pages):
    k_refs = refs[:n_pages]
    v_refs = refs[n_pages:2 * n_pages]
    lf_refs = refs[2 * n_pages:3 * n_pages]
    o_ref = refs[3 * n_pages]
    w, ps = k_refs[0].shape
    nh = lf_refs[0].shape[0]
    hd = w // nh
    heads = lambda x: x.reshape(nh, hd, x.shape[1])
    q = q_ref[...] * scale
    qb = jnp.broadcast_to(q, (w, ps))
    triu = (_iota((ps, ps), 0) <= _iota((ps, ps), 1)).astype(BF16)

    carry = jnp.zeros((nh, 1), F32)
    scores = []
    for p in range(n_pages):
        s = jnp.sum(heads(k_refs[p][...] * qb), axis=1)
        within = _prefix_lanes(lf_refs[p][...], triu)
        scores.append(s - (within + carry))
        carry = carry + within[:, ps - 1:ps]
    s_new = jnp.sum(heads(kn_ref[...] * q), axis=1) - (carry + lfn_ref[...])

    m_tile = scores[0]
    for s in scores[1:]:
        m_tile = jnp.maximum(m_tile, s)
    m = jnp.maximum(s_new, jnp.max(m_tile, axis=1, keepdims=True))
    e_new = jnp.exp(s_new - m)
    exps = [jnp.exp(s - m) for s in scores]
    l_tile = exps[0]
    for e in exps[1:]:
        l_tile = l_tile + e
    inv = 1.0 / (e_new + jnp.sum(l_tile, axis=1, keepdims=True))
    acc = jnp.zeros((w, ps), F32)
    for p in range(n_pages):
        pb = jnp.broadcast_to(exps[p][:, None, :], (nh, hd, ps)).reshape(w, ps)
        acc = acc + v_refs[p][...] * pb
    rep = lambda x: jnp.broadcast_to(x[:, None, :], (nh, hd, 1)).reshape(w, 1)
    o_ref[...] = (jnp.sum(acc, axis=1, keepdims=True) + rep(e_new) * vn_ref[...]) * rep(inv)


def _fox_decode(page_table, q, k_new, v_new, lf_new, cache_kt, cache_vt, cache_lft, layer):
    r, w, _ = q.shape
    nh = lf_new.shape[1]
    n_pages = page_table.shape[1]
    ps = cache_kt.shape[3]
    pt = page_table.reshape(-1)
    tok = lambda c: pl.BlockSpec((None, c, 1), lambda i, pt: (i, 0, 0))

    def page(p, rows):
        return pl.BlockSpec((None, None, rows, ps), lambda i, pt: (layer, pt[i * n_pages + p], 0, 0))

    pages = lambda rows: [page(p, rows) for p in range(n_pages)]
    grid_spec = pltpu.PrefetchScalarGridSpec(
        num_scalar_prefetch=1, grid=(r,),
        in_specs=[tok(w), tok(w), tok(w), tok(nh)] + pages(w) + pages(w) + pages(nh),
        out_specs=tok(w),
    )
    return pl.pallas_call(
        functools.partial(_fox_decode_kernel, scale=HEAD_DIM ** -0.5, n_pages=n_pages),
        grid_spec=grid_spec, out_shape=jax.ShapeDtypeStruct((r, w, 1), F32),
        compiler_params=_params("parallel"), name="fox_decode",
    )(pt, q, k_new, v_new, lf_new, *([cache_kt] * n_pages), *([cache_vt] * n_pages), *([cache_lft] * n_pages))


def _hgrn_step_kernel(q_ref, k_ref, lf_ref, v_ref, s_ref, o_ref, so_ref):
    q = q_ref[...]
    k = k_ref[...]
    f = jnp.exp(lf_ref[...])
    v = v_ref[...]
    s = s_ref[...]
    o_ref[...] = jnp.sum((q * f) * s, axis=2, keepdims=True) + jnp.sum(q * k, axis=2, keepdims=True) * v
    so_ref[...] = f * s + k * v


def _hgrn_step(q, k, lf, v, state, layer, rb):
    r, h, dk, _ = q.shape
    dv = v.shape[3]
    col = pl.BlockSpec((rb, h, dk, 1), lambda i: (i, 0, 0, 0))
    rowv = pl.BlockSpec((rb, h, 1, dv), lambda i: (i, 0, 0, 0))
    return pl.pallas_call(
        _hgrn_step_kernel, grid=(r // rb,),
        in_specs=[col, col, col, rowv, pl.BlockSpec((None, rb, h, dk, dv), lambda i: (layer, i, 0, 0, 0))],
        out_specs=[rowv, pl.BlockSpec((rb, h, dk, dv), lambda i: (i, 0, 0, 0))],
        out_shape=[jax.ShapeDtypeStruct((r, h, 1, dv), F32), jax.ShapeDtypeStruct((r, h, dk, dv), F32)],
        compiler_params=_params("parallel"), name="hgrn_step",
    )(q, k, lf, v, state)


def _head_mean_sq(y, width):
    w = y.shape[1]
    seg = ((_iota((w, w), 0) // width) == (_iota((w, w), 1) // width)).astype(BF16)
    sq = y * y
    hi = sq.astype(BF16)
    lo = (sq - hi.astype(F32)).astype(BF16)
    return (_dot(hi, seg) + _dot(lo, seg)) * (1.0 / width)


def _merge_kernel(fo_ref, ho_ref, gate_ref, x_ref, fgn_ref, hgn_ref, wo_ref, lnx_ref, wxq_ref, x1_ref, qx_ref):
    fo = fo_ref[...]
    ho = ho_ref[...]
    fw = fo.shape[1]
    fn = fo * lax.rsqrt(_head_mean_sq(fo, HEAD_DIM) + EPS) * fgn_ref[...]
    gate = gate_ref[...]
    hn = ho * lax.rsqrt(_head_mean_sq(ho, HEAD_DIM) + EPS) * hgn_ref[...] * (gate / (1.0 + jnp.exp(-gate)))
    y = _dot(fn.astype(BF16), wo_ref[:fw, :]) + _dot(hn.astype(BF16), wo_ref[fw:, :])
    x1 = x_ref[...] + y
    x1_ref[...] = x1
    qx_ref[...] = _dot(_rms(x1, lnx_ref[...]).astype(BF16), wxq_ref[...])


def _merge(fo, ho, gate, x, fgn, hgn, w_out, ln_x, w_xq, tm):
    n, d = x.shape
    xw = w_xq.shape[1]
    full = lambda a: pl.BlockSpec(a.shape, lambda i: (0,) * a.ndim)
    row = lambda c: pl.BlockSpec((tm, c), lambda i: (i, 0))
    return pl.pallas_call(
        _merge_kernel, grid=(n // tm,),
        in_specs=[row(fo.shape[1]), row(ho.shape[1]), row(gate.shape[1]), row(d),
                  full(fgn), full(hgn), full(w_out), full(ln_x), full(w_xq)],
        out_specs=[row(d), row(xw)],
        out_shape=[jax.ShapeDtypeStruct((n, d), F32), jax.ShapeDtypeStruct((n, xw), F32)],
        compiler_params=_params("parallel"), name="merge",
    )(fo, ho, gate, x, fgn, hgn, w_out, ln_x, w_xq)


def _memkv_kernel(m_ref, g_ref, wk_ref, wv_ref, k_ref, v_ref):
    m = _rms(m_ref[...], g_ref[...]).astype(BF16)
    k_ref[...] = _dot(m, wk_ref[...])
    v_ref[...] = _dot(m, wv_ref[...])


def _memkv(mem, g, wk, wv, tm):
    n, d = mem.shape
    xw = wk.shape[1]
    full = lambda a: pl.BlockSpec(a.shape, lambda i: (0,) * a.ndim)
    row = lambda c: pl.BlockSpec((tm, c), lambda i: (i, 0))
    return pl.pallas_call(
        _memkv_kernel, grid=(n // tm,),
        in_specs=[row(d), full(g), full(wk), full(wv)],
        out_specs=[row(xw), row(xw)],
        out_shape=[jax.ShapeDtypeStruct((n, xw), F32)] * 2,
        compiler_params=_params("parallel"), name="memkv",
    )(mem, g, wk, wv)


def _xattn_prompt_kernel(q_ref, mk_ref, mv_ref, o_ref, *, scale):
    heads = q_ref.shape[1] // X_HEAD_DIM
    for h in range(heads):
        sl = slice(h * X_HEAD_DIM, (h + 1) * X_HEAD_DIM)
        q = (q_ref[:, sl] * scale).astype(BF16)
        s = lax.dot_general(q, mk_ref[:, sl].astype(BF16), _NT, preferred_element_type=F32)
        e = jnp.exp(s - jnp.max(s, axis=1, keepdims=True))
        p = e / jnp.sum(e, axis=1, keepdims=True)
        o_ref[:, sl] = _dot(p.astype(BF16), mv_ref[:, sl].astype(BF16))


def _xattn_prompt(q, mk, mv, batch, tm):
    n, xw = q.shape
    t = n // batch
    nm = mk.shape[0] // batch
    nt = t // tm
    return pl.pallas_call(
        functools.partial(_xattn_prompt_kernel, scale=X_HEAD_DIM ** -0.5), grid=(batch, nt),
        in_specs=[pl.BlockSpec((tm, xw), lambda b, i: (b * nt + i, 0)),
                  pl.BlockSpec((nm, xw), lambda b, i: (b, 0)),
                  pl.BlockSpec((nm, xw), lambda b, i: (b, 0))],
        out_specs=pl.BlockSpec((tm, xw), lambda b, i: (b * nt + i, 0)),
        out_shape=jax.ShapeDtypeStruct((n, xw), F32),
        compiler_params=_params("parallel", "parallel"), name="xattn_prompt",
    )(q, mk, mv)


def _xattn_sample_kernel(q_ref, mk_ref, mv_ref, o_ref, *, scale):
    rb, _, w = q_ref.shape
    heads = w // X_HEAD_DIM
    expand = _expand_matrix(8, w, X_HEAD_DIM, BF16)
    valid = _iota((1, 8), 1) < heads
    for r in range(rb):
        qrows = _head_rows(q_ref[r] * scale, heads, X_HEAD_DIM).astype(BF16)
        s = lax.dot_general(mk_ref[r].astype(BF16), qrows, _NT, preferred_element_type=F32)
        e = jnp.exp(s - jnp.max(s, axis=0, keepdims=True))
        p = jnp.where(valid, e / jnp.sum(e, axis=0, keepdims=True), 0.0)
        pe = _dot(p.astype(BF16), expand)
        o_ref[r] = jnp.sum(pe * mv_ref[r], axis=0, keepdims=True)


def _xattn_sample(q, mem_k, mem_v, layer, rb):
    r, _, w = q.shape
    nm = mem_k.shape[2]
    tok = pl.BlockSpec((rb, 1, w), lambda i: (i, 0, 0))
    mem = pl.BlockSpec((None, rb, nm, w), lambda i: (layer, i, 0, 0))
    return pl.pallas_call(
        functools.partial(_xattn_sample_kernel, scale=X_HEAD_DIM ** -0.5), grid=(r // rb,),
        in_specs=[tok, mem, mem], out_specs=tok,
        out_shape=jax.ShapeDtypeStruct((r, 1, w), F32),
        compiler_params=_params("parallel"), name="xattn_sample",
    )(q, mem_k, mem_v)


def _xo_peerq_kernel(x1_ref, ctx_ref, wxo_ref, lnf_ref, wq_ref, x2_ref, h_ref, pq_ref):
    x2 = x1_ref[...] + _dot(ctx_ref[...].astype(BF16), wxo_ref[...])
    x2_ref[...] = x2
    h = _rms(x2, lnf_ref[...]).astype(BF16)
    h_ref[...] = h
    pq_ref[...] = _dot(h, wq_ref[...])


def _xo_peerq(x1, ctx, w_xo, ln_ffn, peer_wq, tm):
    n, d = x1.shape
    qw = peer_wq.shape[1]
    full = lambda a: pl.BlockSpec(a.shape, lambda i: (0,) * a.ndim)
    row = lambda c: pl.BlockSpec((tm, c), lambda i: (i, 0))
    return pl.pallas_call(
        _xo_peerq_kernel, grid=(n // tm,),
        in_specs=[row(d), row(ctx.shape[1]), full(w_xo), full(ln_ffn), full(peer_wq)],
        out_specs=[row(d), row(d), row(qw)],
        out_shape=[jax.ShapeDtypeStruct((n, d), F32), jax.ShapeDtypeStruct((n, d), BF16),
                   jax.ShapeDtypeStruct((n, qw), F32)],
        compiler_params=_params("parallel"), name="xo_peerq",
    )(x1, ctx, w_xo, ln_ffn, peer_wq)


def _top16(s):
    kk = s.shape[0]
    kio = _iota(s.shape, 0)
    vals, idxs = [], []
    for _ in range(PEER_TOPK):
        m = jnp.max(s, axis=0, keepdims=True)
        idx = jnp.min(jnp.where(s == m, kio, kk), axis=0, keepdims=True)
        s = jnp.where(kio == idx, NEG_INF, s)
        vals.append(m)
        idxs.append(idx)
    return jnp.concatenate(vals, axis=0), jnp.concatenate(idxs, axis=0)


def _route_kernel(pq_ref, sk_ref, i_ref, j_ref, g_ref):
    half = sk_ref.shape[2]
    tt = min(2 * LANES, pq_ref.shape[0])
    k = PEER_TOPK

    def sub_tile(n, carry):
        r0 = pl.multiple_of(n * tt, tt)
        sv, si = [], []
        for p in range(2):
            q = pq_ref[pl.ds(r0, tt), p * half:(p + 1) * half].astype(BF16)
            st = lax.dot_general(sk_ref[p], q, _NT, preferred_element_type=F32)
            v, i = _top16(st)
            sv.append(v)
            si.append(i)
        sub = 8
        rows8 = _iota((sub, tt), 0)
        cands, ics, jcs = [sv[0][0:1, :] + sv[1]], [jnp.broadcast_to(si[0][0:1, :], (k, tt))], [si[1]]
        for a in range(1, sub):
            nb = k // (a + 1)
            cands.append(jnp.where(rows8 < nb, sv[0][a:a + 1, :] + sv[1][:sub, :], NEG_INF))
            ics.append(jnp.broadcast_to(si[0][a:a + 1, :], (sub, tt)))
            jcs.append(si[1][:sub, :])
        cands.append(sv[0][sub:, :] + sv[1][0:1, :])
        ics.append(si[0][sub:, :])
        jcs.append(jnp.broadcast_to(si[1][0:1, :], (k - sub, tt)))
        cand = jnp.concatenate(cands, axis=0)
        ic = jnp.concatenate(ics, axis=0)
        jc = jnp.concatenate(jcs, axis=0)
        pio = _iota(cand.shape, 0)
        tv, ti, tj = [], [], []
        for _ in range(k):
            m = jnp.max(cand, axis=0, keepdims=True)
            pos = jnp.min(jnp.where(cand == m, pio, cand.shape[0]), axis=0, keepdims=True)
            sel = pio == pos
            ti.append(jnp.sum(jnp.where(sel, ic, 0), axis=0, keepdims=True))
            tj.append(jnp.sum(jnp.where(sel, jc, 0), axis=0, keepdims=True))
            cand = jnp.where(sel, NEG_INF, cand)
            tv.append(m)
        top = jnp.concatenate(tv, axis=0)
        e = jnp.exp(top - top[0:1, :])
        g_ref[:, pl.ds(r0, tt)] = e / jnp.sum(e, axis=0, keepdims=True)
        i_ref[:, pl.ds(r0, tt)] = jnp.concatenate(ti, axis=0)
        j_ref[:, pl.ds(r0, tt)] = jnp.concatenate(tj, axis=0)
        return carry

    lax.fori_loop(0, pq_ref.shape[0] // tt, sub_tile, 0)


def _route(pq, subkeys, tt):
    n = pq.shape[0]
    hp, keys, half = subkeys.shape
    heads = hp // 2
    k = PEER_TOPK
    out = pl.BlockSpec((k, tt), lambda i, h: (h, i))
    return pl.pallas_call(
        _route_kernel, grid=(n // tt, heads),
        in_specs=[pl.BlockSpec((tt, 2 * half), lambda i, h: (i, h)),
                  pl.BlockSpec((2, keys, half), lambda i, h: (h, 0, 0))],
        out_specs=[out, out, out],
        out_shape=[jax.ShapeDtypeStruct((heads * k, n), I32), jax.ShapeDtypeStruct((heads * k, n), I32),
                   jax.ShapeDtypeStruct((heads * k, n), F32)],
        compiler_params=_params("parallel", "parallel"), name="peer_route",
    )(pq, subkeys)


def _peer_u_kernel(h_ref, ut_ref, i_ref, j_ref, a_ref, *, ib):
    step = pl.program_id(1)

    @pl.when(step == 0)
    def _():
        a_ref[...] = jnp.zeros_like(a_ref)

    isel = i_ref[...]
    jsel = j_ref[...]
    acc = a_ref[...]
    nj = jsel.shape[1]
    h = h_ref[...]
    group = 4
    for g0 in range(0, ib, group):
        a_all = _dot(h, ut_ref[:, g0 * nj:(g0 + group) * nj])
        for ii in range(group):
            a = a_all[:, ii * nj:(ii + 1) * nj]
            acc = jnp.where(isel == step * ib + g0 + ii, jnp.take_along_axis(a, jsel, axis=1), acc)
    a_ref[...] = acc


def _peer_u(h, ut, isel, jsel, tt, ib):
    n, d = h.shape
    slots = isel.shape[1]
    nblk = ut.shape[1] // (slots * ib)
    tok = lambda c: pl.BlockSpec((tt, c), lambda t, e: (t, 0))
    return pl.pallas_call(
        functools.partial(_peer_u_kernel, ib=ib), grid=(n // tt, nblk),
        in_specs=[tok(d), pl.BlockSpec((d, slots * ib), lambda t, e: (0, e)), tok(slots), tok(slots)],
        out_specs=tok(slots), out_shape=jax.ShapeDtypeStruct((n, slots), F32),
        compiler_params=_params("parallel", "arbitrary"), name="peer_u",
    )(h, ut, isel, jsel)


def _peer_v_kernel(i_ref, j_ref, g_ref, a_ref, x_ref, v_ref, o_ref, z_sc, zs_sc, *, ib, stride):
    step = pl.program_id(1)
    tt, slots = i_ref.shape

    @pl.when(step == 0)
    def _():
        a = a_ref[...]
        z_sc[...] = g_ref[...] * (0.5 * a * (1.0 + lax.erf(a * (2.0 ** -0.5))))

        def scatter(t, carry):
            irow = i_ref[pl.ds(t, 1), :]
            jrow = j_ref[pl.ds(t, 1), :]
            zrow = z_sc[pl.ds(t, 1), :]
            io = _iota((slots, slots), 0)
            zit = jnp.where(irow == io, zrow, 0.0).astype(BF16)
            oht = jnp.where(jrow == io, 1.0, 0.0).astype(BF16)
            zs_sc[pl.ds(pl.multiple_of(t * stride, 8), slots), :] = lax.dot_general(
                zit, oht, _NT, preferred_element_type=F32)
            return carry

        lax.fori_loop(0, tt, scatter, 0, unroll=32)
        o_ref[...] = x_ref[...]

    base = step * ib
    zblk = jnp.concatenate(
        [zs_sc[pl.ds(base + ii, tt, stride=stride), :].astype(BF16) for ii in range(ib)], axis=1)
    o_ref[...] += _dot(zblk, v_ref[...])


def _peer_v(isel, jsel, gates, act, x, v, tt, ib):
    n, d = x.shape
    slots = isel.shape[1]
    nblk = v.shape[0] // (slots * ib)
    stride = slots + 8
    tok = lambda c: pl.BlockSpec((tt, c), lambda t, e: (t, 0))
    return pl.pallas_call(
        functools.partial(_peer_v_kernel, ib=ib, stride=stride), grid=(n // tt, nblk),
        in_specs=[tok(slots)] * 4 + [tok(d), pl.BlockSpec((slots * ib, d), lambda t, e: (e, 0))],
        out_specs=tok(d), out_shape=jax.ShapeDtypeStruct((n, d), F32),
        scratch_shapes=[pltpu.VMEM((tt, slots), F32), pltpu.VMEM((tt * stride, slots), F32)],
        compiler_params=_params("parallel", "arbitrary"), name="peer_v",
    )(isel, jsel, gates, act, x, v)


def _final_kernel(x_ref, g_ref, o_ref):
    o_ref[...] = _rms(x_ref[...], g_ref[...])


def _final_norm(x, g, tm):
    n, d = x.shape
    return pl.pallas_call(
        _final_kernel, grid=(n // tm,),
        in_specs=[pl.BlockSpec((tm, d), lambda i: (i, 0)), pl.BlockSpec((1, d), lambda i: (0, 0))],
        out_specs=pl.BlockSpec((tm, d), lambda i: (i, 0)), out_shape=jax.ShapeDtypeStruct((n, d), F32),
        compiler_params=_params("parallel"), name="final_norm",
    )(x, g)


def _pick(n, pref):
    return pref if n % pref == 0 else n


def _peer(x2, h3, pq, lw):
    n = x2.shape[0]
    it, jt, gt = _route(pq, lw["subkeys"], _pick(n, 512))
    isel, jsel, gates = it.T, jt.T, gt.T
    act = _peer_u(h3, lw["ut"], isel, jsel, _pick(n, 512), 16)
    return _peer_v(isel, jsel, gates, act, x2, lw["v"], _pick(n, 512), 16)


def kernel(x_prompt, x_sample, cache_k, cache_v, cache_logf, state_hg, cache_mem_k, cache_mem_v, page_table, mem_prompt, ln_mix, w_in, b_fox_f, hg_lb, fox_gn, hg_gn, w_out, ln_x, ln_mem, w_xq, w_xk, w_xv, w_xo, ln_ffn, peer_wq, peer_subkeys, peer_u, peer_v, ln_final):
    depth, d_model, _ = w_in.shape
    bp, seq, _ = x_prompt.shape
    bs = x_sample.shape[0]
    fox_heads = b_fox_f.shape[1]
    fox_w = fox_gn.shape[1]
    hg_w = hg_gn.shape[1]
    hg_heads = hg_w // HG_DK
    n_phys, page = cache_k.shape[1], cache_k.shape[2]
    n_mem = mem_prompt.shape[1]
    x_w = w_xq.shape[2]
    x_heads = x_w // X_HEAD_DIM
    p_heads = peer_subkeys.shape[1]
    p_keys, p_half = peer_subkeys.shape[3], peer_subkeys.shape[4]
    pairs = fox_w // LANES

    lb_all = jnp.cumsum(jax.nn.softmax(hg_lb.astype(F32), axis=0), axis=0)
    row = lambda a: a.reshape(1, -1).astype(F32)

    xp = x_prompt.reshape(bp * seq, d_model)
    xs = x_sample.reshape(bs, d_model)
    mem = mem_prompt.reshape(bp * n_mem, d_model)
    ck_t = cache_k.transpose(0, 1, 3, 4, 2).reshape(depth, n_phys, fox_w, page)
    cv_t = cache_v.transpose(0, 1, 3, 4, 2).reshape(depth, n_phys, fox_w, page)
    clf_t = cache_logf.transpose(0, 1, 3, 2)
    memk = cache_mem_k.reshape(depth, bs, n_mem, x_w)
    memv = cache_mem_v.reshape(depth, bs, n_mem, x_w)

    outs = {k: [] for k in ("kp", "vp", "fp", "hp", "mkp", "mvp", "ks", "vs", "fs", "hs")}
    for l in range(depth):
        wl = w_in[l]
        c0 = 3 * fox_w
        lw = dict(
            w_main=jnp.concatenate([wl[:, :c0], wl[:, c0 + fox_heads:]], axis=1).astype(BF16),
            w_ff=jnp.pad(wl[:, c0:c0 + fox_heads], ((0, 0), (0, LANES - fox_heads))).astype(BF16),
            b_f=jnp.pad(row(b_fox_f[l]), ((0, 0), (0, LANES - fox_heads))),
            lb=row(lb_all[l] - lb_all[0]),
            subkeys=peer_subkeys[l].reshape(p_heads * 2, p_keys, p_half).astype(BF16),
            ut=peer_u[l].astype(BF16).T, v=peer_v[l].astype(BF16),
        )
        w_out_b, w_xq_b, w_xo_b = w_out[l].astype(BF16), w_xq[l].astype(BF16), w_xo[l].astype(BF16)
        peer_wq_b = peer_wq[l].astype(BF16)

        tm = _pick(bp * seq, 512)
        fq, fk, fv, flf, hq, hk, hv, hlf, hgate = _inproj(xp, row(ln_mix[l]), lw["w_main"], lw["w_ff"], lw["b_f"],
                                                           lw["lb"], tm, fox_heads)
        lf_t = flf.reshape(bp, seq, fox_heads).transpose(0, 2, 1).reshape(bp * fox_heads, seq)
        c_t = _cumsum_lanes(lf_t, _pick(seq, 512))
        ck = c_t.reshape(bp * pairs, 2, seq)
        cq = c_t.reshape(bp, pairs, 2, seq).transpose(1, 0, 3, 2).reshape(pairs, bp * seq, 2)
        fo = _fox_prompt(fq, fk, fv, cq, ck, bp, _pick(seq, 1024))
        ho, st = _hgrn_prompt(hq, hk, hv, hlf, bp)
        x1, qx = _merge(fo, ho, hgate, xp, row(fox_gn[l]), row(hg_gn[l]), w_out_b, row(ln_x[l]), w_xq_b, tm)
        mk, mv = _memkv(mem, row(ln_mem[l]), w_xk[l].astype(BF16), w_xv[l].astype(BF16), n_mem)
        ctx = _xattn_prompt(qx, mk, mv, bp, tm)
        x2, h3, pq = _xo_peerq(x1, ctx, w_xo_b, row(ln_ffn[l]), peer_wq_b, tm)
        xp = _peer(x2, h3, pq, lw)
        outs["kp"].append(fk.reshape(bp, seq, fox_heads, HEAD_DIM))
        outs["vp"].append(fv.reshape(bp, seq, fox_heads, HEAD_DIM))
        outs["fp"].append(flf.reshape(bp, seq, fox_heads))
        st5 = st.reshape(bp, pairs, 2, HG_DK, 2, HG_DK)
        s_heads = jnp.stack([st5[:, :, 0, :, 0, :], st5[:, :, 1, :, 1, :]], axis=2)
        outs["hp"].append(s_heads.reshape(bp, hg_heads, HG_DK, HG_DK).transpose(0, 1, 3, 2))
        outs["mkp"].append(mk.reshape(bp, n_mem, x_heads, X_HEAD_DIM))
        outs["mvp"].append(mv.reshape(bp, n_mem, x_heads, X_HEAD_DIM))

        sq, sk, sv, slf, tq, tk, tv, tlf, tgate = _inproj(xs, row(ln_mix[l]), lw["w_main"], lw["w_ff"], lw["b_f"],
                                                           lw["lb"], bs, fox_heads)
        tok3 = lambda a: a.reshape(bs, 1, a.shape[1])
        colm = lambda a: a.reshape(bs, a.shape[1], 1)
        fo_s = _fox_decode(page_table, colm(sq), colm(sk), colm(sv), colm(slf), ck_t, cv_t, clf_t,
                           l).reshape(bs, fox_w)
        colv = lambda a: a.reshape(bs, hg_heads, HG_DK, 1)
        ho_s, s_new = _hgrn_step(colv(tq), colv(tk), colv(tlf), tv.reshape(bs, hg_heads, 1, HG_DK), state_hg, l,
                                 _pick(bs, 8))
        x1s, qxs = _merge(fo_s, ho_s.reshape(bs, hg_w), tgate, xs, row(fox_gn[l]), row(hg_gn[l]), w_out_b,
                          row(ln_x[l]), w_xq_b, bs)
        ctx_s = _xattn_sample(tok3(qxs), memk, memv, l, _pick(bs, 8)).reshape(bs, x_w)
        x2s, h3s, pqs = _xo_peerq(x1s, ctx_s, w_xo_b, row(ln_ffn[l]), peer_wq_b, bs)
        xs = _peer(x2s, h3s, pqs, lw)
        outs["ks"].append(sk.reshape(bs, 1, fox_heads, HEAD_DIM))
        outs["vs"].append(sv.reshape(bs, 1, fox_heads, HEAD_DIM))
        outs["fs"].append(slf.reshape(bs, 1, fox_heads))
        outs["hs"].append(s_new)

    y_prompt = _final_norm(xp, row(ln_final), _pick(bp * seq, 512)).reshape(bp, seq, d_model)
    y_sample = _final_norm(xs, row(ln_final), bs).reshape(bs, 1, d_model)
    st_ = lambda k: jnp.stack(outs[k])
    return (y_prompt, y_sample, st_("kp"), st_("vp"), st_("fp"), st_("hp"), st_("mkp"), st_("mvp"),
            st_("ks"), st_("vs"), st_("fs"), st_("hs"))
```

```python
import functools
import math

import jax
import jax.numpy as jnp
from jax import lax
from jax.experimental import pallas as pl
from jax.experimental.pallas import tpu as pltpu

F32 = jnp.float32
BF16 = jnp.bfloat16
I32 = jnp.int32
EPS = 1e-6
HIGHEST = lax.Precision.HIGHEST
NEG_INF = float("-inf")

HEAD_DIM = 64
HG_DK = 64
HG_CHUNK = 64
HG_SUB = 16
X_HEAD_DIM = 128
PEER_TOPK = 16
LANES = 128
VMEM_LIMIT = 56 * 1024 * 1024

_NT = (((1,), (1,)), ((), ()))
_TN = (((0,), (0,)), ((), ()))


def _params(*sem):
    return pltpu.CompilerParams(dimension_semantics=sem, vmem_limit_bytes=VMEM_LIMIT)


def _rms(x, g):
    ms = jnp.mean(x * x, axis=-1, keepdims=True)
    return x * lax.rsqrt(ms + EPS) * g


def _log_sigmoid(x):
    return jnp.minimum(x, 0.0) - jnp.log1p(jnp.exp(-jnp.abs(x)))


def _dot(a, b):
    return jnp.dot(a, b, preferred_element_type=F32)


def _split3(x):
    hi = x.astype(BF16)
    r = x - hi.astype(F32)
    mid = r.astype(BF16)
    lo = (r - mid.astype(F32)).astype(BF16)
    return hi, mid, lo


def _prefix_rows(tril, x):
    return _dot(jnp.concatenate([tril] * 3, axis=1), jnp.concatenate(_split3(x), axis=0))


def _prefix_lanes(x, triu):
    return _dot(jnp.concatenate(_split3(x), axis=1), jnp.concatenate([triu] * 3, axis=0))


def _iota(shape, axis):
    return lax.broadcasted_iota(I32, shape, axis)


def _inproj_kernel(x_ref, g_ref, w_ref, wff_ref, bf_ref, lb_ref,
                   fq_ref, fk_ref, fv_ref, lf_ref, hq_ref, hk_ref, hv_ref, hlf_ref, gate_ref, *, fox_heads):
    h = _rms(x_ref[...], g_ref[...]).astype(BF16)
    w = w_ref.shape[1] // 7

    def mm(j):
        return _dot(h, w_ref[:, j * w:(j + 1) * w])

    fq_ref[...] = mm(0)
    fk_ref[...] = mm(1)
    fv_ref[...] = mm(2)
    ff = _dot(h, wff_ref[...])
    lf_ref[...] = _log_sigmoid(ff + bf_ref[...])[:, :fox_heads]
    hq_ref[...] = mm(3)
    z = mm(4)
    lb = lb_ref[...]
    a = jnp.log(lb)
    b = jnp.log1p(-lb) + _log_sigmoid(z)
    hlf_ref[...] = jnp.maximum(a, b) + jnp.log1p(jnp.exp(-jnp.abs(a - b)))
    hk_ref[...] = (1.0 - lb) * (1.0 / (1.0 + jnp.exp(z)))
    hv_ref[...] = mm(5)
    gate_ref[...] = mm(6)


def _inproj(x, g, w_main, w_ff, b_f, lb, tm, fox_heads):
    n, d = x.shape
    w = w_main.shape[1] // 7
    full = lambda a: pl.BlockSpec(a.shape, lambda i: (0,) * a.ndim)
    row = lambda c: pl.BlockSpec((tm, c), lambda i: (i, 0))
    outs = [jax.ShapeDtypeStruct((n, w), F32)] * 3 + [jax.ShapeDtypeStruct((n, fox_heads), F32)] + \
           [jax.ShapeDtypeStruct((n, w), F32)] * 5
    out_specs = [row(w)] * 3 + [row(fox_heads)] + [row(w)] * 5
    return pl.pallas_call(
        functools.partial(_inproj_kernel, fox_heads=fox_heads),
        grid=(n // tm,),
        in_specs=[row(d), full(g), full(w_main), full(w_ff), full(b_f), full(lb)],
        out_specs=out_specs, out_shape=outs,
        compiler_params=_params("parallel"), name="inproj",
    )(x, g, w_main, w_ff, b_f, lb)


def _cumsum_kernel(x_ref, o_ref, carry_ref):
    @pl.when(pl.program_id(0) == 0)
    def _():
        carry_ref[...] = jnp.zeros_like(carry_ref)

    x = x_ref[...]
    n = x.shape[1]
    tri = (_iota((n, n), 0) <= _iota((n, n), 1)).astype(BF16)
    c = _prefix_lanes(x, tri) + carry_ref[:, :1]
    o_ref[...] = c
    carry_ref[...] = jnp.broadcast_to(c[:, n - 1:n], carry_ref.shape)


def _cumsum_lanes(x, tc):
    r, t = x.shape
    return pl.pallas_call(
        _cumsum_kernel, grid=(t // tc,),
        in_specs=[pl.BlockSpec((r, tc), lambda i: (0, i))],
        out_specs=pl.BlockSpec((r, tc), lambda i: (0, i)),
        out_shape=jax.ShapeDtypeStruct((r, t), F32),
        scratch_shapes=[pltpu.VMEM((r, LANES), F32)],
        compiler_params=_params("arbitrary"), name="logf_cumsum",
    )(x)


def _fox_kernel(qt_ref, kt_ref, q_ref, k_ref, v_ref, cq_ref, ck_ref, o_ref, m_ref, cqr_ref, acc_ref, qh_ref, *,
                scale):
    qi = qt_ref[pl.program_id(2)]
    ki = kt_ref[pl.program_id(2)]
    tq = q_ref.shape[0]
    tk = k_ref.shape[0]
    lane = _iota((1, LANES), 1)
    log2e = math.log2(math.e)

    @pl.when(ki == 0)
    def _():
        m_ref[...] = jnp.full_like(m_ref, NEG_INF)
        acc_ref[...] = jnp.zeros_like(acc_ref)
        q = q_ref[...] * (scale * log2e)
        for hh in range(2):
            cqr_ref[hh] = jnp.broadcast_to(cq_ref[:, hh:hh + 1] * log2e, (tq, LANES))
            qh_ref[hh] = jnp.where((lane // HEAD_DIM) == hh, q, 0.0).astype(BF16)

    def step(masked):
        k = k_ref[...].astype(BF16)
        v = v_ref[...]
        ck = ck_ref[...] * log2e
        if masked:
            causal = (ki * tk + _iota((1, tk), 1)) <= (qi * tq + _iota((tq, 1), 0))
        for hh in range(2):
            vh = jnp.where((lane // HEAD_DIM) == hh, v, 1.0).astype(BF16)
            s = lax.dot_general(qh_ref[hh], k, _NT, preferred_element_type=F32) - ck[hh:hh + 1, :]
            if masked:
                s = jnp.where(causal, s, NEG_INF)
            m_prev = m_ref[hh]
            cq = cqr_ref[hh]
            m_new = jnp.maximum(m_prev, jnp.max(s, axis=1, keepdims=True) + cq)
            p = jnp.exp2(s - jnp.tile(m_new - cq, (1, tk // LANES)))
            acc_ref[hh] = jnp.exp2(m_prev - m_new) * acc_ref[hh] + _dot(p.astype(BF16), vh)
            m_ref[hh] = m_new

    @pl.when(ki < qi)
    def _():
        step(False)

    @pl.when(ki == qi)
    def _():
        step(True)
        a0 = acc_ref[0]
        a1 = acc_ref[1]
        o0 = a0 / pltpu.roll(a0, HEAD_DIM, axis=1)
        o1 = a1 / pltpu.roll(a1, HEAD_DIM, axis=1)
        o_ref[...] = jnp.where(lane < HEAD_DIM, o0, o1)


def _fox_prompt(q, k, v, cq, ck, batch, tq):
    n, w = q.shape
    t = n // batch
    pairs = w // LANES
    nq = t // tq
    kern = functools.partial(_fox_kernel, scale=HEAD_DIM ** -0.5)
    steps = [(i, j) for i in range(nq) for j in range(i + 1)]
    qi_tab = jnp.asarray([s[0] for s in steps], I32)
    ki_tab = jnp.asarray([s[1] for s in steps], I32)
    grid_spec = pltpu.PrefetchScalarGridSpec(
        num_scalar_prefetch=2, grid=(batch, pairs, len(steps)),
        in_specs=[
            pl.BlockSpec((tq, LANES), lambda b, g, s, qt, kt: (b * nq + qt[s], g)),
            pl.BlockSpec((tq, LANES), lambda b, g, s, qt, kt: (b * nq + kt[s], g)),
            pl.BlockSpec((tq, LANES), lambda b, g, s, qt, kt: (b * nq + kt[s], g)),
            pl.BlockSpec((None, tq, 2), lambda b, g, s, qt, kt: (g, b * nq + qt[s], 0)),
            pl.BlockSpec((None, 2, tq), lambda b, g, s, qt, kt: (b * pairs + g, 0, kt[s])),
        ],
        out_specs=pl.BlockSpec((tq, LANES), lambda b, g, s, qt, kt: (b * nq + qt[s], g)),
        scratch_shapes=[pltpu.VMEM((2, tq, LANES), F32)] * 3 + [pltpu.VMEM((2, tq, LANES), BF16)],
    )
    return pl.pallas_call(
        kern, grid_spec=grid_spec, out_shape=jax.ShapeDtypeStruct((n, w), F32),
        compiler_params=_params("parallel", "parallel", "arbitrary"), name="fox_prompt",
    )(qi_tab, ki_tab, q, k, v, cq, ck)


def _hgrn_kernel(q_ref, k_ref, v_ref, lf_ref, o_ref, st_ref, st_sc):
    n = pl.program_id(0)
    nb, c, w = q_ref.shape
    pairs = w // LANES

    @pl.when(n == 0)
    def _():
        st_sc[...] = jnp.zeros_like(st_sc)

    tril = (_iota((c, c), 1) <= _iota((c, c), 0)).astype(BF16)
    same_head = (_iota((LANES, LANES), 0) // HG_DK) == (_iota((LANES, LANES), 1) // HG_DK)
    seg = same_head.astype(BF16)
    sc = HG_SUB
    ti = _iota((sc, sc, LANES), 0)
    si = _iota((sc, sc, LANES), 1)
    first_head = _iota((1, LANES), 1) < HG_DK
    for bg in range(nb * pairs):
        r, g = divmod(bg, pairs)
        sl = slice(g * LANES, (g + 1) * LANES)
        q = q_ref[r, :, sl]
        k = k_ref[r, :, sl]
        v = v_ref[r, :, sl]
        vb = v.astype(BF16)
        b = _prefix_rows(tril, lf_ref[r, :, sl])
        st = st_sc[r, g]
        qe = (q * jnp.exp(b)).astype(BF16)
        o = lax.dot_general(qe, st.astype(BF16), _NT, preferred_element_type=F32)
        rows = []
        for i in range(c // sc):
            lo, hi = i * sc, (i + 1) * sc
            qi, ki, bi = q[lo:hi], k[lo:hi], b[lo:hi]
            decay = jnp.exp(jnp.where(si <= ti, bi[:, None, :] - bi[None, :, :], NEG_INF))
            p = (qi[:, None, :] * ki[None, :, :]) * decay
            a = _dot(p.reshape(sc * sc, LANES).astype(BF16), seg)
            oi = jnp.sum(a.reshape(sc, sc, LANES) * v[None, lo:hi, :], axis=1)
            if i > 0:
                br = b[lo - 1:lo, :]
                qt = qi * jnp.exp(bi - br)
                q2 = jnp.concatenate([jnp.where(first_head, qt, 0.0), jnp.where(first_head, 0.0, qt)], axis=0)
                ks = (k[:lo] * jnp.exp(br - b[:lo])).astype(BF16)
                a2 = lax.dot_general(q2.astype(BF16), ks, _NT, preferred_element_type=F32)
                o2 = _dot(a2.astype(BF16), vb[:lo])
                oi = oi + jnp.where(first_head, o2[:sc], o2[sc:])
            rows.append(oi)
        o_ref[r, :, sl] = o + jnp.concatenate(rows, axis=0)
        b_last = b[c - 1:c, :]
        kd = (k * jnp.exp(b_last - b)).astype(BF16)
        upd = lax.dot_general(vb, kd, _TN, preferred_element_type=F32)
        st_sc[r, g] = st * jnp.exp(b_last) + jnp.where(same_head, upd, 0.0)

    @pl.when(n == pl.num_programs(0) - 1)
    def _():
        st_ref[...] = st_sc[...]


def _hgrn_prompt(q, k, v, lf, batch):
    n, w = q.shape
    t = n // batch
    pairs = w // LANES
    rows3 = lambda a: a.reshape(batch, t, w)
    blk = pl.BlockSpec((batch, HG_CHUNK, w), lambda i: (0, i, 0))
    st_shape = (batch, pairs, LANES, LANES)
    o, st = pl.pallas_call(
        _hgrn_kernel, grid=(t // HG_CHUNK,),
        in_specs=[blk] * 4,
        out_specs=[blk, pl.BlockSpec(st_shape, lambda i: (0, 0, 0, 0))],
        out_shape=[jax.ShapeDtypeStruct((batch, t, w), F32), jax.ShapeDtypeStruct(st_shape, F32)],
        scratch_shapes=[pltpu.VMEM(st_shape, F32)],
        compiler_params=_params("arbitrary"), name="hgrn_prompt",
    )(rows3(q), rows3(k), rows3(v), rows3(lf))
    return o.reshape(n, w), st


def _head_rows(x_row, heads, width):
    w = x_row.shape[1]
    keep = (_iota((8, w), 1) // width) == _iota((8, w), 0)
    return jnp.where(keep, jnp.broadcast_to(x_row, (8, w)), 0.0)


def _expand_matrix(heads_pad, w, width, dtype):
    return ((_iota((heads_pad, w), 1) // width) == _iota((heads_pad, w), 0)).astype(dtype)


def _fox_decode_kernel(pt_ref, q_ref, kn_ref, vn_ref, lfn_ref, *refs, scale, n_pages):
    k_refs = refs[:n_pages]
    v_refs = refs[n_pages:2 * n_pages]
    lf_refs = refs[2 * n_pages:3 * n_pages]
    o_ref = refs[3 * n_pages]
    w, ps = k_refs[0].shape
    nh = lf_refs[0].shape[0]
    hd = w // nh
    heads = lambda x: x.reshape(nh, hd, x.shape[1])
    q = q_ref[...] * scale
    qb = jnp.broadcast_to(q, (w, ps))
    triu = (_iota((ps, ps), 0) <= _iota((ps, ps), 1)).astype(BF16)

    carry = jnp.zeros((nh, 1), F32)
    scores = []
    for p in range(n_pages):
        s = jnp.sum(heads(k_refs[p][...] * qb), axis=1)
        within = _prefix_lanes(lf_refs[p][...], triu)
        scores.append(s - (within + carry))
        carry = carry + within[:, ps - 1:ps]
    s_new = jnp.sum(heads(kn_ref[...] * q), axis=1) - (carry + lfn_ref[...])

    m_tile = scores[0]
    for s in scores[1:]:
        m_tile = jnp.maximum(m_tile, s)
    m = jnp.maximum(s_new, jnp.max(m_tile, axis=1, keepdims=True))
    e_new = jnp.exp(s_new - m)
    exps = [jnp.exp(s - m) for s in scores]
    l_tile = exps[0]
    for e in exps[1:]:
        l_tile = l_tile + e
    inv = 1.0 / (e_new + jnp.sum(l_tile, axis=1, keepdims=True))
    acc = jnp.zeros((w, ps), F32)
    for p in range(n_pages):
        pb = jnp.broadcast_to(exps[p][:, None, :], (nh, hd, ps)).reshape(w, ps)
        acc = acc + v_refs[p][...] * pb
    rep = lambda x: jnp.broadcast_to(x[:, None, :], (nh, hd, 1)).reshape(w, 1)
    o_ref[...] = (jnp.sum(acc, axis=1, keepdims=True) + rep(e_new) * vn_ref[...]) * rep(inv)


def _fox_decode(page_table, q, k_new, v_new, lf_new, cache_kt, cache_vt, cache_lft, layer):
    r, w, _ = q.shape
    nh = lf_new.shape[1]
    n_pages = page_table.shape[1]
    ps = cache_kt.shape[3]
    pt = page_table.reshape(-1)
    tok = lambda c: pl.BlockSpec((None, c, 1), lambda i, pt: (i, 0, 0))

    def page(p, rows):
        return pl.BlockSpec((None, None, rows, ps), lambda i, pt: (layer, pt[i * n_pages + p], 0, 0))

    pages = lambda rows: [page(p, rows) for p in range(n_pages)]
    grid_spec = pltpu.PrefetchScalarGridSpec(
        num_scalar_prefetch=1, grid=(r,),
        in_specs=[tok(w), tok(w), tok(w), tok(nh)] + pages(w) + pages(w) + pages(nh),
        out_specs=tok(w),
    )
    return pl.pallas_call(
        functools.partial(_fox_decode_kernel, scale=HEAD_DIM ** -0.5, n_pages=n_pages),
        grid_spec=grid_spec, out_shape=jax.ShapeDtypeStruct((r, w, 1), F32),
        compiler_params=_params("parallel"), name="fox_decode",
    )(pt, q, k_new, v_new, lf_new, *([cache_kt] * n_pages), *([cache_vt] * n_pages), *([cache_lft] * n_pages))


def _hgrn_step_kernel(q_ref, k_ref, lf_ref, v_ref, s_ref, o_ref, so_ref):
    q = q_ref[...]
    k = k_ref[...]
    f = jnp.exp(lf_ref[...])
    v = v_ref[...]
    s = s_ref[...]
    so_ref[...] = f * s + k * v[None]
    o_ref[...] = jnp.sum((q * f) * s, axis=0) + jnp.sum(q * k, axis=0) * v


def _hgrn_step(q, k, lf, v, state_t, layer):
    h, dk, _, r = q.shape
    dv = v.shape[1]
    col = pl.BlockSpec((None, dk, 1, r), lambda i: (i, 0, 0, 0))
    rowv = pl.BlockSpec((None, dv, r), lambda i: (i, 0, 0))
    return pl.pallas_call(
        _hgrn_step_kernel, grid=(h,),
        in_specs=[col, col, col, rowv, pl.BlockSpec((None, None, dk, dv, r), lambda i: (layer, i, 0, 0, 0))],
        out_specs=[rowv, pl.BlockSpec((None, dk, dv, r), lambda i: (i, 0, 0, 0))],
        out_shape=[jax.ShapeDtypeStruct((h, dv, r), F32), jax.ShapeDtypeStruct((h, dk, dv, r), F32)],
        compiler_params=_params("parallel"), name="hgrn_step",
    )(q, k, lf, v, state_t)


def _head_mean_sq(y, width):
    w = y.shape[1]
    seg = ((_iota((w, w), 0) // width) == (_iota((w, w), 1) // width)).astype(BF16)
    sq = y * y
    hi = sq.astype(BF16)
    lo = (sq - hi.astype(F32)).astype(BF16)
    return (_dot(hi, seg) + _dot(lo, seg)) * (1.0 / width)


def _merge_kernel(fo_ref, ho_ref, gate_ref, x_ref, fgn_ref, hgn_ref, wo_ref, lnx_ref, wxq_ref, x1_ref, qx_ref):
    fo = fo_ref[...]
    ho = ho_ref[...]
    fw = fo.shape[1]
    fn = fo * lax.rsqrt(_head_mean_sq(fo, HEAD_DIM) + EPS) * fgn_ref[...]
    gate = gate_ref[...]
    hn = ho * lax.rsqrt(_head_mean_sq(ho, HEAD_DIM) + EPS) * hgn_ref[...] * (gate / (1.0 + jnp.exp(-gate)))
    y = _dot(fn.astype(BF16), wo_ref[:fw, :]) + _dot(hn.astype(BF16), wo_ref[fw:, :])
    x1 = x_ref[...] + y
    x1_ref[...] = x1
    qx_ref[...] = _dot(_rms(x1, lnx_ref[...]).astype(BF16), wxq_ref[...])


def _merge(fo, ho, gate, x, fgn, hgn, w_out, ln_x, w_xq, tm):
    n, d = x.shape
    xw = w_xq.shape[1]
    full = lambda a: pl.BlockSpec(a.shape, lambda i: (0,) * a.ndim)
    row = lambda c: pl.BlockSpec((tm, c), lambda i: (i, 0))
    return pl.pallas_call(
        _merge_kernel, grid=(n // tm,),
        in_specs=[row(fo.shape[1]), row(ho.shape[1]), row(gate.shape[1]), row(d),
                  full(fgn), full(hgn), full(w_out), full(ln_x), full(w_xq)],
        out_specs=[row(d), row(xw)],
        out_shape=[jax.ShapeDtypeStruct((n, d), F32), jax.ShapeDtypeStruct((n, xw), F32)],
        compiler_params=_params("parallel"), name="merge",
    )(fo, ho, gate, x, fgn, hgn, w_out, ln_x, w_xq)


def _memkv_kernel(m_ref, g_ref, wk_ref, wv_ref, k_ref, v_ref):
    m = _rms(m_ref[...], g_ref[...]).astype(BF16)
    k_ref[...] = _dot(m, wk_ref[...])
    v_ref[...] = _dot(m, wv_ref[...])


def _memkv(mem, g, wk, wv, tm):
    n, d = mem.shape
    xw = wk.shape[1]
    full = lambda a: pl.BlockSpec(a.shape, lambda i: (0,) * a.ndim)
    row = lambda c: pl.BlockSpec((tm, c), lambda i: (i, 0))
    return pl.pallas_call(
        _memkv_kernel, grid=(n // tm,),
        in_specs=[row(d), full(g), full(wk), full(wv)],
        out_specs=[row(xw), row(xw)],
        out_shape=[jax.ShapeDtypeStruct((n, xw), F32)] * 2,
        compiler_params=_params("parallel"), name="memkv",
    )(mem, g, wk, wv)


def _xattn_prompt_kernel(q_ref, mk_ref, mv_ref, o_ref, *, scale):
    heads = q_ref.shape[1] // X_HEAD_DIM
    for h in range(heads):
        sl = slice(h * X_HEAD_DIM, (h + 1) * X_HEAD_DIM)
        q = (q_ref[:, sl] * scale).astype(BF16)
        s = lax.dot_general(q, mk_ref[:, sl].astype(BF16), _NT, preferred_element_type=F32)
        e = jnp.exp(s - jnp.max(s, axis=1, keepdims=True))
        p = e / jnp.sum(e, axis=1, keepdims=True)
        o_ref[:, sl] = _dot(p.astype(BF16), mv_ref[:, sl].astype(BF16))


def _xattn_prompt(q, mk, mv, batch, tm):
    n, xw = q.shape
    t = n // batch
    nm = mk.shape[0] // batch
    nt = t // tm
    return pl.pallas_call(
        functools.partial(_xattn_prompt_kernel, scale=X_HEAD_DIM ** -0.5), grid=(batch, nt),
        in_specs=[pl.BlockSpec((tm, xw), lambda b, i: (b * nt + i, 0)),
                  pl.BlockSpec((nm, xw), lambda b, i: (b, 0)),
                  pl.BlockSpec((nm, xw), lambda b, i: (b, 0))],
        out_specs=pl.BlockSpec((tm, xw), lambda b, i: (b * nt + i, 0)),
        out_shape=jax.ShapeDtypeStruct((n, xw), F32),
        compiler_params=_params("parallel", "parallel"), name="xattn_prompt",
    )(q, mk, mv)


def _xattn_sample_kernel(qt_ref, mk_ref, mv_ref, o_ref, *, scale, heads):
    rb, hd, cols = qt_ref.shape
    nm = mk_ref.shape[1] // heads
    col = _iota((hd, cols), 1)
    erow = _iota((cols, hd), 0)
    valid = _iota((1, cols), 1) < heads
    for r in range(rb):
        qt = qt_ref[r] * scale
        s = jnp.zeros((nm, cols), F32)
        for h in range(heads):
            kh = mk_ref[r, pl.ds(h, nm, stride=heads), :].astype(BF16)
            s = s + _dot(kh, jnp.where(col == h, qt, 0.0).astype(BF16))
        e = jnp.exp(s - jnp.max(s, axis=0, keepdims=True))
        p = jnp.where(valid, e / jnp.sum(e, axis=0, keepdims=True), 0.0).astype(BF16)
        outs = []
        for h in range(heads):
            pe = _dot(p, (erow == h).astype(BF16))
            vh = mv_ref[r, pl.ds(h, nm, stride=heads), :]
            outs.append(jnp.sum((pe * vh).reshape(nm // 8, 8, hd), axis=0))
        o_ref[r] = jnp.sum(jnp.concatenate(outs, axis=1), axis=0, keepdims=True)


def _xattn_sample(q_t, mem_k, mem_v, layer, heads, rb):
    r, hd, cols = q_t.shape
    rows = mem_k.shape[2]
    mem = pl.BlockSpec((None, rb, rows, hd), lambda i: (layer, i, 0, 0))
    return pl.pallas_call(
        functools.partial(_xattn_sample_kernel, scale=X_HEAD_DIM ** -0.5, heads=heads), grid=(r // rb,),
        in_specs=[pl.BlockSpec((rb, hd, cols), lambda i: (i, 0, 0)), mem, mem],
        out_specs=pl.BlockSpec((rb, 1, heads * hd), lambda i: (i, 0, 0)),
        out_shape=jax.ShapeDtypeStruct((r, 1, heads * hd), F32),
        compiler_params=_params("parallel"), name="xattn_sample",
    )(q_t, mem_k, mem_v)


def _xo_peerq_kernel(x1_ref, ctx_ref, wxo_ref, lnf_ref, wq_ref, x2_ref, h_ref, pq_ref):
    x2 = x1_ref[...] + _dot(ctx_ref[...].astype(BF16), wxo_ref[...])
    x2_ref[...] = x2
    h = _rms(x2, lnf_ref[...]).astype(BF16)
    h_ref[...] = h
    pq_ref[...] = _dot(h, wq_ref[...])


def _xo_peerq(x1, ctx, w_xo, ln_ffn, peer_wq, tm):
    n, d = x1.shape
    qw = peer_wq.shape[1]
    full = lambda a: pl.BlockSpec(a.shape, lambda i: (0,) * a.ndim)
    row = lambda c: pl.BlockSpec((tm, c), lambda i: (i, 0))
    return pl.pallas_call(
        _xo_peerq_kernel, grid=(n // tm,),
        in_specs=[row(d), row(ctx.shape[1]), full(w_xo), full(ln_ffn), full(peer_wq)],
        out_specs=[row(d), row(d), row(qw)],
        out_shape=[jax.ShapeDtypeStruct((n, d), F32), jax.ShapeDtypeStruct((n, d), BF16),
                   jax.ShapeDtypeStruct((n, qw), F32)],
        compiler_params=_params("parallel"), name="xo_peerq",
    )(x1, ctx, w_xo, ln_ffn, peer_wq)


def _top16(s):
    kk = s.shape[0]
    kio = _iota(s.shape, 0)
    vals, idxs = [], []
    for _ in range(PEER_TOPK):
        m = jnp.max(s, axis=0, keepdims=True)
        idx = jnp.min(jnp.where(s == m, kio, kk), axis=0, keepdims=True)
        s = jnp.where(kio == idx, NEG_INF, s)
        vals.append(m)
        idxs.append(idx)
    return jnp.concatenate(vals, axis=0), jnp.concatenate(idxs, axis=0)


def _route_kernel(pq_ref, sk_ref, i_ref, j_ref, g_ref):
    half = sk_ref.shape[2]
    tt = min(2 * LANES, pq_ref.shape[0])
    k = PEER_TOPK

    def sub_tile(n, carry):
        r0 = pl.multiple_of(n * tt, tt)
        sv, si = [], []
        for p in range(2):
            q = pq_ref[pl.ds(r0, tt), p * half:(p + 1) * half].astype(BF16)
            st = lax.dot_general(sk_ref[p], q, _NT, preferred_element_type=F32)
            v, i = _top16(st)
            sv.append(v)
            si.append(i)
        sub = 8
        rows8 = _iota((sub, tt), 0)
        cands, ics, jcs = [sv[0][0:1, :] + sv[1]], [jnp.broadcast_to(si[0][0:1, :], (k, tt))], [si[1]]
        for a in range(1, sub):
            nb = k // (a + 1)
            cands.append(jnp.where(rows8 < nb, sv[0][a:a + 1, :] + sv[1][:sub, :], NEG_INF))
            ics.append(jnp.broadcast_to(si[0][a:a + 1, :], (sub, tt)))
            jcs.append(si[1][:sub, :])
        cands.append(sv[0][sub:, :] + sv[1][0:1, :])
        ics.append(si[0][sub:, :])
        jcs.append(jnp.broadcast_to(si[1][0:1, :], (k - sub, tt)))
        cand = jnp.concatenate(cands, axis=0)
        ic = jnp.concatenate(ics, axis=0)
        jc = jnp.concatenate(jcs, axis=0)
        pio = _iota(cand.shape, 0)
        tv, ti, tj = [], [], []
        for _ in range(k):
            m = jnp.max(cand, axis=0, keepdims=True)
            pos = jnp.min(jnp.where(cand == m, pio, cand.shape[0]), axis=0, keepdims=True)
            sel = pio == pos
            ti.append(jnp.sum(jnp.where(sel, ic, 0), axis=0, keepdims=True))
            tj.append(jnp.sum(jnp.where(sel, jc, 0), axis=0, keepdims=True))
            cand = jnp.where(sel, NEG_INF, cand)
            tv.append(m)
        top = jnp.concatenate(tv, axis=0)
        e = jnp.exp(top - top[0:1, :])
        g_ref[:, pl.ds(r0, tt)] = e / jnp.sum(e, axis=0, keepdims=True)
        i_ref[:, pl.ds(r0, tt)] = jnp.concatenate(ti, axis=0)
        j_ref[:, pl.ds(r0, tt)] = jnp.concatenate(tj, axis=0)
        return carry

    lax.fori_loop(0, pq_ref.shape[0] // tt, sub_tile, 0)


def _route(pq, subkeys, tt):
    n = pq.shape[0]
    hp, keys, half = subkeys.shape
    heads = hp // 2
    k = PEER_TOPK
    out = pl.BlockSpec((k, tt), lambda i, h: (h, i))
    return pl.pallas_call(
        _route_kernel, grid=(n // tt, heads),
        in_specs=[pl.BlockSpec((tt, 2 * half), lambda i, h: (i, h)),
                  pl.BlockSpec((2, keys, half), lambda i, h: (h, 0, 0))],
        out_specs=[out, out, out],
        out_shape=[jax.ShapeDtypeStruct((heads * k, n), I32), jax.ShapeDtypeStruct((heads * k, n), I32),
                   jax.ShapeDtypeStruct((heads * k, n), F32)],
        compiler_params=_params("parallel", "parallel"), name="peer_route",
    )(pq, subkeys)


def _peer_u_kernel(h_ref, ut_ref, i_ref, j_ref, a_ref, *, ib):
    step = pl.program_id(1)

    @pl.when(step == 0)
    def _():
        a_ref[...] = jnp.zeros_like(a_ref)

    isel = i_ref[...]
    jsel = j_ref[...]
    acc = a_ref[...]
    nj = jsel.shape[1]
    h = h_ref[...]
    group = 4
    for g0 in range(0, ib, group):
        a_all = _dot(h, ut_ref[:, g0 * nj:(g0 + group) * nj])
        for ii in range(group):
            a = a_all[:, ii * nj:(ii + 1) * nj]
            acc = jnp.where(isel == step * ib + g0 + ii, jnp.take_along_axis(a, jsel, axis=1), acc)
    a_ref[...] = acc


def _peer_u(h, ut, isel, jsel, tt, ib):
    n, d = h.shape
    slots = isel.shape[1]
    nblk = ut.shape[1] // (slots * ib)
    tok = lambda c: pl.BlockSpec((tt, c), lambda t, e: (t, 0))
    return pl.pallas_call(
        functools.partial(_peer_u_kernel, ib=ib), grid=(n // tt, nblk),
        in_specs=[tok(d), pl.BlockSpec((d, slots * ib), lambda t, e: (0, e)), tok(slots), tok(slots)],
        out_specs=tok(slots), out_shape=jax.ShapeDtypeStruct((n, slots), F32),
        compiler_params=_params("parallel", "arbitrary"), name="peer_u",
    )(h, ut, isel, jsel)


def _peer_v_kernel(i_ref, j_ref, g_ref, a_ref, x_ref, v_ref, o_ref, z_sc, zs_sc, *, ib, stride):
    step = pl.program_id(1)
    tt, slots = i_ref.shape

    @pl.when(step == 0)
    def _():
        a = a_ref[...]
        z_sc[...] = g_ref[...] * (0.5 * a * (1.0 + lax.erf(a * (2.0 ** -0.5))))

        def scatter(t, carry):
            irow = i_ref[pl.ds(t, 1), :]
            jrow = j_ref[pl.ds(t, 1), :]
            zrow = z_sc[pl.ds(t, 1), :]
            io = _iota((slots, slots), 0)
            zit = jnp.where(irow == io, zrow, 0.0).astype(BF16)
            oht = jnp.where(jrow == io, 1.0, 0.0).astype(BF16)
            zs_sc[pl.ds(pl.multiple_of(t * stride, 8), slots), :] = lax.dot_general(
                zit, oht, _NT, preferred_element_type=F32)
            return carry

        lax.fori_loop(0, tt, scatter, 0, unroll=32)
        o_ref[...] = x_ref[...]

    base = step * ib
    zblk = jnp.concatenate(
        [zs_sc[pl.ds(base + ii, tt, stride=stride), :].astype(BF16) for ii in range(ib)], axis=1)
    o_ref[...] += _dot(zblk, v_ref[...])


def _peer_v(isel, jsel, gates, act, x, v, tt, ib):
    n, d = x.shape
    slots = isel.shape[1]
    nblk = v.shape[0] // (slots * ib)
    stride = slots + 8
    tok = lambda c: pl.BlockSpec((tt, c), lambda t, e: (t, 0))
    return pl.pallas_call(
        functools.partial(_peer_v_kernel, ib=ib, stride=stride), grid=(n // tt, nblk),
        in_specs=[tok(slots)] * 4 + [tok(d), pl.BlockSpec((slots * ib, d), lambda t, e: (e, 0))],
        out_specs=tok(d), out_shape=jax.ShapeDtypeStruct((n, d), F32),
        scratch_shapes=[pltpu.VMEM((tt, slots), F32), pltpu.VMEM((tt * stride, slots), F32)],
        compiler_params=_params("parallel", "arbitrary"), name="peer_v",
    )(isel, jsel, gates, act, x, v)


def _final_kernel(x_ref, g_ref, o_ref):
    o_ref[...] = _rms(x_ref[...], g_ref[...])


def _final_norm(x, g, tm):
    n, d = x.shape
    return pl.pallas_call(
        _final_kernel, grid=(n // tm,),
        in_specs=[pl.BlockSpec((tm, d), lambda i: (i, 0)), pl.BlockSpec((1, d), lambda i: (0, 0))],
        out_specs=pl.BlockSpec((tm, d), lambda i: (i, 0)), out_shape=jax.ShapeDtypeStruct((n, d), F32),
        compiler_params=_params("parallel"), name="final_norm",
    )(x, g)


def _pick(n, pref):
    return pref if n % pref == 0 else n


def _peer(x2, h3, pq, lw):
    n = x2.shape[0]
    it, jt, gt = _route(pq, lw["subkeys"], _pick(n, 512))
    isel, jsel, gates = it.T, jt.T, gt.T
    act = _peer_u(h3, lw["ut"], isel, jsel, _pick(n, 512), 16)
    return _peer_v(isel, jsel, gates, act, x2, lw["v"], _pick(n, 512), 16)


def kernel(x_prompt, x_sample, cache_k, cache_v, cache_logf, state_hg, cache_mem_k, cache_mem_v, page_table, mem_prompt, ln_mix, w_in, b_fox_f, hg_lb, fox_gn, hg_gn, w_out, ln_x, ln_mem, w_xq, w_xk, w_xv, w_xo, ln_ffn, peer_wq, peer_subkeys, peer_u, peer_v, ln_final):
    depth, d_model, _ = w_in.shape
    bp, seq, _ = x_prompt.shape
    bs = x_sample.shape[0]
    fox_heads = b_fox_f.shape[1]
    fox_w = fox_gn.shape[1]
    hg_w = hg_gn.shape[1]
    hg_heads = hg_w // HG_DK
    n_phys, page = cache_k.shape[1], cache_k.shape[2]
    n_mem = mem_prompt.shape[1]
    x_w = w_xq.shape[2]
    x_heads = x_w // X_HEAD_DIM
    p_heads = peer_subkeys.shape[1]
    p_keys, p_half = peer_subkeys.shape[3], peer_subkeys.shape[4]
    pairs = fox_w // LANES

    lb_all = jnp.cumsum(jax.nn.softmax(hg_lb.astype(F32), axis=0), axis=0)
    row = lambda a: a.reshape(1, -1).astype(F32)

    xp = x_prompt.reshape(bp * seq, d_model)
    xs = x_sample.reshape(bs, d_model)
    mem = mem_prompt.reshape(bp * n_mem, d_model)
    ck_t = cache_k.transpose(0, 1, 3, 4, 2).reshape(depth, n_phys, fox_w, page)
    cv_t = cache_v.transpose(0, 1, 3, 4, 2).reshape(depth, n_phys, fox_w, page)
    clf_t = cache_logf.transpose(0, 1, 3, 2)
    state_t = state_hg.transpose(0, 2, 3, 4, 1)
    memk = cache_mem_k.reshape(depth, bs, n_mem * x_heads, X_HEAD_DIM)
    memv = cache_mem_v.reshape(depth, bs, n_mem * x_heads, X_HEAD_DIM)

    outs = {k: [] for k in ("kp", "vp", "fp", "hp", "mkp", "mvp", "ks", "vs", "fs", "hs")}
    for l in range(depth):
        wl = w_in[l]
        c0 = 3 * fox_w
        lw = dict(
            w_main=jnp.concatenate([wl[:, :c0], wl[:, c0 + fox_heads:]], axis=1).astype(BF16),
            w_ff=jnp.pad(wl[:, c0:c0 + fox_heads], ((0, 0), (0, LANES - fox_heads))).astype(BF16),
            b_f=jnp.pad(row(b_fox_f[l]), ((0, 0), (0, LANES - fox_heads))),
            lb=row(lb_all[l] - lb_all[0]),
            subkeys=peer_subkeys[l].reshape(p_heads * 2, p_keys, p_half).astype(BF16),
            ut=peer_u[l].astype(BF16).T, v=peer_v[l].astype(BF16),
        )
        w_out_b, w_xq_b, w_xo_b = w_out[l].astype(BF16), w_xq[l].astype(BF16), w_xo[l].astype(BF16)
        peer_wq_b = peer_wq[l].astype(BF16)

        tm = _pick(bp * seq, 512)
        fq, fk, fv, flf, hq, hk, hv, hlf, hgate = _inproj(xp, row(ln_mix[l]), lw["w_main"], lw["w_ff"], lw["b_f"],
                                                           lw["lb"], tm, fox_heads)
        lf_t = flf.reshape(bp, seq, fox_heads).transpose(0, 2, 1).reshape(bp * fox_heads, seq)
        c_t = _cumsum_lanes(lf_t, _pick(seq, 512))
        ck = c_t.reshape(bp * pairs, 2, seq)
        cq = c_t.reshape(bp, pairs, 2, seq).transpose(1, 0, 3, 2).reshape(pairs, bp * seq, 2)
        fo = _fox_prompt(fq, fk, fv, cq, ck, bp, _pick(seq, 1024))
        ho, st = _hgrn_prompt(hq, hk, hv, hlf, bp)
        x1, qx = _merge(fo, ho, hgate, xp, row(fox_gn[l]), row(hg_gn[l]), w_out_b, row(ln_x[l]), w_xq_b, tm)
        mk, mv = _memkv(mem, row(ln_mem[l]), w_xk[l].astype(BF16), w_xv[l].astype(BF16), n_mem)
        ctx = _xattn_prompt(qx, mk, mv, bp, tm)
        x2, h3, pq = _xo_peerq(x1, ctx, w_xo_b, row(ln_ffn[l]), peer_wq_b, tm)
        xp = _peer(x2, h3, pq, lw)
        outs["kp"].append(fk.reshape(bp, seq, fox_heads, HEAD_DIM))
        outs["vp"].append(fv.reshape(bp, seq, fox_heads, HEAD_DIM))
        outs["fp"].append(flf.reshape(bp, seq, fox_heads))
        st5 = st.reshape(bp, pairs, 2, HG_DK, 2, HG_DK)
        s_heads = jnp.stack([st5[:, :, 0, :, 0, :], st5[:, :, 1, :, 1, :]], axis=2)
        outs["hp"].append(s_heads.reshape(bp, hg_heads, HG_DK, HG_DK).transpose(0, 1, 3, 2))
        outs["mkp"].append(mk.reshape(bp, n_mem, x_heads, X_HEAD_DIM))
        outs["mvp"].append(mv.reshape(bp, n_mem, x_heads, X_HEAD_DIM))

        sq, sk, sv, slf, tq, tk, tv, tlf, tgate = _inproj(xs, row(ln_mix[l]), lw["w_main"], lw["w_ff"], lw["b_f"],
                                                           lw["lb"], bs, fox_heads)
        tok3 = lambda a: a.reshape(bs, 1, a.shape[1])
        colm = lambda a: a.reshape(bs, a.shape[1], 1)
        fo_s = _fox_decode(page_table, colm(sq), colm(sk), colm(sv), colm(slf), ck_t, cv_t, clf_t,
                           l).reshape(bs, fox_w)
        lanes_r = lambda a: a.T.reshape(hg_heads, HG_DK, 1, bs)
        ho_t, s_new = _hgrn_step(lanes_r(tq), lanes_r(tk), lanes_r(tlf), tv.T.reshape(hg_heads, HG_DK, bs),
                                 state_t, l)
        x1s, qxs = _merge(fo_s, ho_t.reshape(hg_w, bs).T, tgate, xs, row(fox_gn[l]), row(hg_gn[l]), w_out_b,
                          row(ln_x[l]), w_xq_b, bs)
        qxs_t = jnp.pad(qxs.reshape(bs, x_heads, X_HEAD_DIM).transpose(0, 2, 1), ((0, 0), (0, 0), (0, 8 - x_heads)))
        ctx_s = _xattn_sample(qxs_t, memk, memv, l, x_heads, _pick(bs, 8)).reshape(bs, x_w)
        x2s, h3s, pqs = _xo_peerq(x1s, ctx_s, w_xo_b, row(ln_ffn[l]), peer_wq_b, bs)
        xs = _peer(x2s, h3s, pqs, lw)
        outs["ks"].append(sk.reshape(bs, 1, fox_heads, HEAD_DIM))
        outs["vs"].append(sv.reshape(bs, 1, fox_heads, HEAD_DIM))
        outs["fs"].append(slf.reshape(bs, 1, fox_heads))
        outs["hs"].append(s_new)

    y_prompt = _final_norm(xp, row(ln_final), _pick(bp * seq, 512)).reshape(bp, seq, d_model)
    y_sample = _final_norm(xs, row(ln_final), bs).reshape(bs, 1, d_model)
    st_ = lambda k: jnp.stack(outs[k])
    return (y_prompt, y_sample, st_("kp"), st_("vp"), st_("fp"), st_("hp"), st_("mkp"), st_("mvp"),
            st_("ks"), st_("vs"), st_("fs"), st_("hs").transpose(0, 4, 1, 2, 3))
```

```python
import functools
import math

import jax
import jax.numpy as jnp
from jax import lax
from jax.experimental import pallas as pl
from jax.experimental.pallas import tpu as pltpu

F32 = jnp.float32
BF16 = jnp.bfloat16
I32 = jnp.int32
EPS = 1e-6
NEG_INF = float("-inf")

HEAD_DIM = 64
HG_DK = 64
HG_CHUNK = 64
HG_SUB = 16
X_HEAD_DIM = 128
PEER_TOPK = 16
LANES = 128
V7X_VMEM_BYTES = 64 * 1024 * 1024
VMEM_LIMIT = V7X_VMEM_BYTES - 8 * 1024 * 1024

TILE = dict(
    tokenwise=512,
    cumsum=512,
    fox=1024,
    route=512,
    peer_u=1024,
    peer_v=512,
    peer_blocks=16,
    xattn_sample=8,
)

_NT = (((1,), (1,)), ((), ()))
_TN = (((0,), (0,)), ((), ()))


def _params(*sem):
    return pltpu.CompilerParams(dimension_semantics=sem, vmem_limit_bytes=VMEM_LIMIT)


def _rms(x, g):
    ms = jnp.mean(x * x, axis=-1, keepdims=True)
    return x * lax.rsqrt(ms + EPS) * g


def _log_sigmoid(x):
    return jnp.minimum(x, 0.0) - jnp.log1p(jnp.exp(-jnp.abs(x)))


def _dot(a, b):
    return jnp.dot(a, b, preferred_element_type=F32)


def _split3(x):
    hi = x.astype(BF16)
    r = x - hi.astype(F32)
    mid = r.astype(BF16)
    lo = (r - mid.astype(F32)).astype(BF16)
    return hi, mid, lo


def _prefix_rows(tril, x):
    return _dot(jnp.concatenate([tril] * 3, axis=1), jnp.concatenate(_split3(x), axis=0))


def _prefix_lanes(x, triu):
    return _dot(jnp.concatenate(_split3(x), axis=1), jnp.concatenate([triu] * 3, axis=0))


def _iota(shape, axis):
    return lax.broadcasted_iota(I32, shape, axis)


def _inproj_kernel(x_ref, g_ref, w_ref, wff_ref, bf_ref, lb_ref,
                   fq_ref, fk_ref, fv_ref, lf_ref, hq_ref, hk_ref, hv_ref, hlf_ref, gate_ref, *, fox_heads):
    h = _rms(x_ref[...], g_ref[...]).astype(BF16)
    w = w_ref.shape[1] // 7

    def mm(j):
        return _dot(h, w_ref[:, j * w:(j + 1) * w])

    fq_ref[...] = mm(0)
    fk_ref[...] = mm(1)
    fv_ref[...] = mm(2)
    ff = _dot(h, wff_ref[...])
    lf_ref[...] = _log_sigmoid(ff + bf_ref[...])[:, :fox_heads]
    hq_ref[...] = mm(3)
    z = mm(4)
    lb = lb_ref[...]
    a = jnp.log(lb)
    b = jnp.log1p(-lb) + _log_sigmoid(z)
    hlf_ref[...] = jnp.maximum(a, b) + jnp.log1p(jnp.exp(-jnp.abs(a - b)))
    hk_ref[...] = (1.0 - lb) * (1.0 / (1.0 + jnp.exp(z)))
    hv_ref[...] = mm(5)
    gate_ref[...] = mm(6)


def _inproj(x, g, w_main, w_ff, b_f, lb, tm, fox_heads):
    n, d = x.shape
    w = w_main.shape[1] // 7
    full = lambda a: pl.BlockSpec(a.shape, lambda i: (0,) * a.ndim)
    row = lambda c: pl.BlockSpec((tm, c), lambda i: (i, 0))
    outs = [jax.ShapeDtypeStruct((n, w), F32)] * 3 + [jax.ShapeDtypeStruct((n, fox_heads), F32)] + \
           [jax.ShapeDtypeStruct((n, w), F32)] * 5
    out_specs = [row(w)] * 3 + [row(fox_heads)] + [row(w)] * 5
    return pl.pallas_call(
        functools.partial(_inproj_kernel, fox_heads=fox_heads),
        grid=(n // tm,),
        in_specs=[row(d), full(g), full(w_main), full(w_ff), full(b_f), full(lb)],
        out_specs=out_specs, out_shape=outs,
        compiler_params=_params("parallel"), name="inproj",
    )(x, g, w_main, w_ff, b_f, lb)


def _cumsum_kernel(x_ref, o_ref, carry_ref):
    @pl.when(pl.program_id(0) == 0)
    def _():
        carry_ref[...] = jnp.zeros_like(carry_ref)

    x = x_ref[...]
    n = x.shape[1]
    tri = (_iota((n, n), 0) <= _iota((n, n), 1)).astype(BF16)
    c = _prefix_lanes(x, tri) + carry_ref[:, :1]
    o_ref[...] = c
    carry_ref[...] = jnp.broadcast_to(c[:, n - 1:n], carry_ref.shape)


def _cumsum_lanes(x, tc):
    r, t = x.shape
    return pl.pallas_call(
        _cumsum_kernel, grid=(t // tc,),
        in_specs=[pl.BlockSpec((r, tc), lambda i: (0, i))],
        out_specs=pl.BlockSpec((r, tc), lambda i: (0, i)),
        out_shape=jax.ShapeDtypeStruct((r, t), F32),
        scratch_shapes=[pltpu.VMEM((r, LANES), F32)],
        compiler_params=_params("arbitrary"), name="logf_cumsum",
    )(x)


def _fox_kernel(qt_ref, kt_ref, q_ref, k_ref, v_ref, cq_ref, ck_ref, o_ref, m_ref, cqr_ref, acc_ref, qh_ref, *,
                scale):
    qi = qt_ref[pl.program_id(2)]
    ki = kt_ref[pl.program_id(2)]
    tq = q_ref.shape[0]
    tk = k_ref.shape[0]
    lane = _iota((1, LANES), 1)
    log2e = math.log2(math.e)

    @pl.when(ki == 0)
    def _():
        m_ref[...] = jnp.full_like(m_ref, NEG_INF)
        acc_ref[...] = jnp.zeros_like(acc_ref)
        q = q_ref[...] * (scale * log2e)
        for hh in range(2):
            cqr_ref[hh] = jnp.broadcast_to(cq_ref[:, hh:hh + 1] * log2e, (tq, LANES))
            qh_ref[hh] = jnp.where((lane // HEAD_DIM) == hh, q, 0.0).astype(BF16)

    def step(masked):
        k = k_ref[...].astype(BF16)
        v = v_ref[...]
        ck = ck_ref[...] * log2e
        if masked:
            causal = (ki * tk + _iota((1, tk), 1)) <= (qi * tq + _iota((tq, 1), 0))
        for hh in range(2):
            vh = jnp.where((lane // HEAD_DIM) == hh, v, 1.0).astype(BF16)
            s = lax.dot_general(qh_ref[hh], k, _NT, preferred_element_type=F32) - ck[hh:hh + 1, :]
            if masked:
                s = jnp.where(causal, s, NEG_INF)
            m_prev = m_ref[hh]
            cq = cqr_ref[hh]
            m_new = jnp.maximum(m_prev, jnp.max(s, axis=1, keepdims=True) + cq)
            p = jnp.exp2(s - jnp.tile(m_new - cq, (1, tk // LANES)))
            acc_ref[hh] = jnp.exp2(m_prev - m_new) * acc_ref[hh] + _dot(p.astype(BF16), vh)
            m_ref[hh] = m_new

    @pl.when(ki < qi)
    def _():
        step(False)

    @pl.when(ki == qi)
    def _():
        step(True)
        a0 = acc_ref[0]
        a1 = acc_ref[1]
        o0 = a0 / pltpu.roll(a0, HEAD_DIM, axis=1)
        o1 = a1 / pltpu.roll(a1, HEAD_DIM, axis=1)
        o_ref[...] = jnp.where(lane < HEAD_DIM, o0, o1)


def _fox_prompt(q, k, v, cq, ck, batch, tq):
    n, w = q.shape
    t = n // batch
    pairs = w // LANES
    nq = t // tq
    kern = functools.partial(_fox_kernel, scale=HEAD_DIM ** -0.5)
    steps = [(i, j) for i in range(nq) for j in range(i + 1)]
    qi_tab = jnp.asarray([s[0] for s in steps], I32)
    ki_tab = jnp.asarray([s[1] for s in steps], I32)
    grid_spec = pltpu.PrefetchScalarGridSpec(
        num_scalar_prefetch=2, grid=(batch, pairs, len(steps)),
        in_specs=[
            pl.BlockSpec((tq, LANES), lambda b, g, s, qt, kt: (b * nq + qt[s], g)),
            pl.BlockSpec((tq, LANES), lambda b, g, s, qt, kt: (b * nq + kt[s], g)),
            pl.BlockSpec((tq, LANES), lambda b, g, s, qt, kt: (b * nq + kt[s], g)),
            pl.BlockSpec((None, tq, 2), lambda b, g, s, qt, kt: (g, b * nq + qt[s], 0)),
            pl.BlockSpec((None, 2, tq), lambda b, g, s, qt, kt: (b * pairs + g, 0, kt[s])),
        ],
        out_specs=pl.BlockSpec((tq, LANES), lambda b, g, s, qt, kt: (b * nq + qt[s], g)),
        scratch_shapes=[pltpu.VMEM((2, tq, LANES), F32)] * 3 + [pltpu.VMEM((2, tq, LANES), BF16)],
    )
    return pl.pallas_call(
        kern, grid_spec=grid_spec, out_shape=jax.ShapeDtypeStruct((n, w), F32),
        compiler_params=_params("parallel", "parallel", "arbitrary"), name="fox_prompt",
    )(qi_tab, ki_tab, q, k, v, cq, ck)


def _hgrn_kernel(q_ref, k_ref, v_ref, lf_ref, o_ref, st_ref, st_sc):
    n = pl.program_id(0)
    nb, c, w = q_ref.shape
    pairs = w // LANES

    @pl.when(n == 0)
    def _():
        st_sc[...] = jnp.zeros_like(st_sc)

    tril = (_iota((c, c), 1) <= _iota((c, c), 0)).astype(BF16)
    same_head = (_iota((LANES, LANES), 0) // HG_DK) == (_iota((LANES, LANES), 1) // HG_DK)
    seg = same_head.astype(BF16)
    sc = HG_SUB
    ti = _iota((sc, sc, LANES), 0)
    si = _iota((sc, sc, LANES), 1)
    first_head = _iota((1, LANES), 1) < HG_DK
    for bg in range(nb * pairs):
        r, g = divmod(bg, pairs)
        sl = slice(g * LANES, (g + 1) * LANES)
        q = q_ref[r, :, sl]
        k = k_ref[r, :, sl]
        v = v_ref[r, :, sl]
        vb = v.astype(BF16)
        b = _prefix_rows(tril, lf_ref[r, :, sl])
        st = st_sc[r, g]
        qe = (q * jnp.exp(b)).astype(BF16)
        o = lax.dot_general(qe, st.astype(BF16), _NT, preferred_element_type=F32)
        rows = []
        for i in range(c // sc):
            lo, hi = i * sc, (i + 1) * sc
            qi, ki, bi = q[lo:hi], k[lo:hi], b[lo:hi]
            decay = jnp.exp(jnp.where(si <= ti, bi[:, None, :] - bi[None, :, :], NEG_INF))
            p = (qi[:, None, :] * ki[None, :, :]) * decay
            a = _dot(p.reshape(sc * sc, LANES).astype(BF16), seg)
            oi = jnp.sum(a.reshape(sc, sc, LANES) * v[None, lo:hi, :], axis=1)
            if i > 0:
                br = b[lo - 1:lo, :]
                qt = qi * jnp.exp(bi - br)
                q2 = jnp.concatenate([jnp.where(first_head, qt, 0.0), jnp.where(first_head, 0.0, qt)], axis=0)
                ks = (k[:lo] * jnp.exp(br - b[:lo])).astype(BF16)
                a2 = lax.dot_general(q2.astype(BF16), ks, _NT, preferred_element_type=F32)
                o2 = _dot(a2.astype(BF16), vb[:lo])
                oi = oi + jnp.where(first_head, o2[:sc], o2[sc:])
            rows.append(oi)
        o_ref[r, :, sl] = o + jnp.concatenate(rows, axis=0)
        b_last = b[c - 1:c, :]
        kd = (k * jnp.exp(b_last - b)).astype(BF16)
        upd = lax.dot_general(vb, kd, _TN, preferred_element_type=F32)
        st_sc[r, g] = st * jnp.exp(b_last) + jnp.where(same_head, upd, 0.0)

    @pl.when(n == pl.num_programs(0) - 1)
    def _():
        st_ref[...] = st_sc[...]


def _hgrn_prompt(q, k, v, lf, batch):
    n, w = q.shape
    t = n // batch
    pairs = w // LANES
    rows3 = lambda a: a.reshape(batch, t, w)
    blk = pl.BlockSpec((batch, HG_CHUNK, w), lambda i: (0, i, 0))
    st_shape = (batch, pairs, LANES, LANES)
    o, st = pl.pallas_call(
        _hgrn_kernel, grid=(t // HG_CHUNK,),
        in_specs=[blk] * 4,
        out_specs=[blk, pl.BlockSpec(st_shape, lambda i: (0, 0, 0, 0))],
        out_shape=[jax.ShapeDtypeStruct((batch, t, w), F32), jax.ShapeDtypeStruct(st_shape, F32)],
        scratch_shapes=[pltpu.VMEM(st_shape, F32)],
        compiler_params=_params("arbitrary"), name="hgrn_prompt",
    )(rows3(q), rows3(k), rows3(v), rows3(lf))
    return o.reshape(n, w), st


def _fox_decode_kernel(pt_ref, q_ref, kn_ref, vn_ref, lfn_ref, *refs, scale, n_pages):
    k_refs = refs[:n_pages]
    v_refs = refs[n_pages:2 * n_pages]
    lf_refs = refs[2 * n_pages:3 * n_pages]
    o_ref = refs[3 * n_pages]
    w, ps = k_refs[0].shape
    nh = lf_refs[0].shape[0]
    hd = w // nh
    heads = lambda x: x.reshape(nh, hd, x.shape[1])
    q = q_ref[...] * scale
    qb = jnp.broadcast_to(q, (w, ps))
    triu = (_iota((ps, ps), 0) <= _iota((ps, ps), 1)).astype(BF16)

    carry = jnp.zeros((nh, 1), F32)
    scores = []
    for p in range(n_pages):
        s = jnp.sum(heads(k_refs[p][...] * qb), axis=1)
        within = _prefix_lanes(lf_refs[p][...], triu)
        scores.append(s - (within + carry))
        carry = carry + within[:, ps - 1:ps]
    s_new = jnp.sum(heads(kn_ref[...] * q), axis=1) - (carry + lfn_ref[...])

    m_tile = scores[0]
    for s in scores[1:]:
        m_tile = jnp.maximum(m_tile, s)
    m = jnp.maximum(s_new, jnp.max(m_tile, axis=1, keepdims=True))
    e_new = jnp.exp(s_new - m)
    exps = [jnp.exp(s - m) for s in scores]
    l_tile = exps[0]
    for e in exps[1:]:
        l_tile = l_tile + e
    inv = 1.0 / (e_new + jnp.sum(l_tile, axis=1, keepdims=True))
    acc = jnp.zeros((w, ps), F32)
    for p in range(n_pages):
        pb = jnp.broadcast_to(exps[p][:, None, :], (nh, hd, ps)).reshape(w, ps)
        acc = acc + v_refs[p][...] * pb
    rep = lambda x: jnp.broadcast_to(x[:, None, :], (nh, hd, 1)).reshape(w, 1)
    o_ref[...] = (jnp.sum(acc, axis=1, keepdims=True) + rep(e_new) * vn_ref[...]) * rep(inv)


def _fox_decode(page_table, q, k_new, v_new, lf_new, cache_kt, cache_vt, cache_lft, layer):
    r, w, _ = q.shape
    nh = lf_new.shape[1]
    n_pages = page_table.shape[1]
    ps = cache_kt.shape[3]
    pt = page_table.reshape(-1)
    tok = lambda c: pl.BlockSpec((None, c, 1), lambda i, pt: (i, 0, 0))

    def page(p, rows):
        return pl.BlockSpec((None, None, rows, ps), lambda i, pt: (layer, pt[i * n_pages + p], 0, 0))

    pages = lambda rows: [page(p, rows) for p in range(n_pages)]
    grid_spec = pltpu.PrefetchScalarGridSpec(
        num_scalar_prefetch=1, grid=(r,),
        in_specs=[tok(w), tok(w), tok(w), tok(nh)] + pages(w) + pages(w) + pages(nh),
        out_specs=tok(w),
    )
    return pl.pallas_call(
        functools.partial(_fox_decode_kernel, scale=HEAD_DIM ** -0.5, n_pages=n_pages),
        grid_spec=grid_spec, out_shape=jax.ShapeDtypeStruct((r, w, 1), F32),
        compiler_params=_params("parallel"), name="fox_decode",
    )(pt, q, k_new, v_new, lf_new, *([cache_kt] * n_pages), *([cache_vt] * n_pages), *([cache_lft] * n_pages))


def _hgrn_step_kernel(q_ref, k_ref, lf_ref, v_ref, s_ref, o_ref, so_ref):
    q = q_ref[...]
    k = k_ref[...]
    f = jnp.exp(lf_ref[...])
    v = v_ref[...]
    s = s_ref[...]
    so_ref[...] = f * s + k * v[None]
    o_ref[...] = jnp.sum((q * f) * s, axis=0) + jnp.sum(q * k, axis=0) * v


def _hgrn_step(q, k, lf, v, state_t, layer):
    h, dk, _, r = q.shape
    dv = v.shape[1]
    col = pl.BlockSpec((None, dk, 1, r), lambda i: (i, 0, 0, 0))
    rowv = pl.BlockSpec((None, dv, r), lambda i: (i, 0, 0))
    return pl.pallas_call(
        _hgrn_step_kernel, grid=(h,),
        in_specs=[col, col, col, rowv, pl.BlockSpec((None, None, dk, dv, r), lambda i: (layer, i, 0, 0, 0))],
        out_specs=[rowv, pl.BlockSpec((None, dk, dv, r), lambda i: (i, 0, 0, 0))],
        out_shape=[jax.ShapeDtypeStruct((h, dv, r), F32), jax.ShapeDtypeStruct((h, dk, dv, r), F32)],
        compiler_params=_params("parallel"), name="hgrn_step",
    )(q, k, lf, v, state_t)


def _head_mean_sq(y, width):
    w = y.shape[1]
    seg = ((_iota((w, w), 0) // width) == (_iota((w, w), 1) // width)).astype(BF16)
    sq = y * y
    hi = sq.astype(BF16)
    lo = (sq - hi.astype(F32)).astype(BF16)
    return (_dot(hi, seg) + _dot(lo, seg)) * (1.0 / width)


def _merge_kernel(fo_ref, ho_ref, gate_ref, x_ref, fgn_ref, hgn_ref, wo_ref, lnx_ref, wxq_ref, x1_ref, qx_ref):
    fo = fo_ref[...]
    ho = ho_ref[...]
    fw = fo.shape[1]
    fn = fo * lax.rsqrt(_head_mean_sq(fo, HEAD_DIM) + EPS) * fgn_ref[...]
    gate = gate_ref[...]
    hn = ho * lax.rsqrt(_head_mean_sq(ho, HEAD_DIM) + EPS) * hgn_ref[...] * (gate / (1.0 + jnp.exp(-gate)))
    y = _dot(fn.astype(BF16), wo_ref[:fw, :]) + _dot(hn.astype(BF16), wo_ref[fw:, :])
    x1 = x_ref[...] + y
    x1_ref[...] = x1
    qx_ref[...] = _dot(_rms(x1, lnx_ref[...]).astype(BF16), wxq_ref[...])


def _merge(fo, ho, gate, x, fgn, hgn, w_out, ln_x, w_xq, tm):
    n, d = x.shape
    xw = w_xq.shape[1]
    full = lambda a: pl.BlockSpec(a.shape, lambda i: (0,) * a.ndim)
    row = lambda c: pl.BlockSpec((tm, c), lambda i: (i, 0))
    return pl.pallas_call(
        _merge_kernel, grid=(n // tm,),
        in_specs=[row(fo.shape[1]), row(ho.shape[1]), row(gate.shape[1]), row(d),
                  full(fgn), full(hgn), full(w_out), full(ln_x), full(w_xq)],
        out_specs=[row(d), row(xw)],
        out_shape=[jax.ShapeDtypeStruct((n, d), F32), jax.ShapeDtypeStruct((n, xw), F32)],
        compiler_params=_params("parallel"), name="merge",
    )(fo, ho, gate, x, fgn, hgn, w_out, ln_x, w_xq)


def _memkv_kernel(m_ref, g_ref, wk_ref, wv_ref, k_ref, v_ref):
    m = _rms(m_ref[...], g_ref[...]).astype(BF16)
    k_ref[...] = _dot(m, wk_ref[...])
    v_ref[...] = _dot(m, wv_ref[...])


def _memkv(mem, g, wk, wv, tm):
    n, d = mem.shape
    xw = wk.shape[1]
    full = lambda a: pl.BlockSpec(a.shape, lambda i: (0,) * a.ndim)
    row = lambda c: pl.BlockSpec((tm, c), lambda i: (i, 0))
    return pl.pallas_call(
        _memkv_kernel, grid=(n // tm,),
        in_specs=[row(d), full(g), full(wk), full(wv)],
        out_specs=[row(xw), row(xw)],
        out_shape=[jax.ShapeDtypeStruct((n, xw), F32)] * 2,
        compiler_params=_params("parallel"), name="memkv",
    )(mem, g, wk, wv)


def _xattn_prompt_kernel(q_ref, mk_ref, mv_ref, o_ref, *, scale):
    heads = q_ref.shape[1] // X_HEAD_DIM
    for h in range(heads):
        sl = slice(h * X_HEAD_DIM, (h + 1) * X_HEAD_DIM)
        q = (q_ref[:, sl] * scale).astype(BF16)
        s = lax.dot_general(q, mk_ref[:, sl].astype(BF16), _NT, preferred_element_type=F32)
        e = jnp.exp(s - jnp.max(s, axis=1, keepdims=True))
        p = e / jnp.sum(e, axis=1, keepdims=True)
        o_ref[:, sl] = _dot(p.astype(BF16), mv_ref[:, sl].astype(BF16))


def _xattn_prompt(q, mk, mv, batch, tm):
    n, xw = q.shape
    t = n // batch
    nm = mk.shape[0] // batch
    nt = t // tm
    return pl.pallas_call(
        functools.partial(_xattn_prompt_kernel, scale=X_HEAD_DIM ** -0.5), grid=(batch, nt),
        in_specs=[pl.BlockSpec((tm, xw), lambda b, i: (b * nt + i, 0)),
                  pl.BlockSpec((nm, xw), lambda b, i: (b, 0)),
                  pl.BlockSpec((nm, xw), lambda b, i: (b, 0))],
        out_specs=pl.BlockSpec((tm, xw), lambda b, i: (b * nt + i, 0)),
        out_shape=jax.ShapeDtypeStruct((n, xw), F32),
        compiler_params=_params("parallel", "parallel"), name="xattn_prompt",
    )(q, mk, mv)


def _xattn_sample_kernel(qt_ref, mk_ref, mv_ref, o_ref, *, scale, heads):
    rb, hd, cols = qt_ref.shape
    nm = mk_ref.shape[1] // heads
    col = _iota((hd, cols), 1)
    erow = _iota((cols, hd), 0)
    valid = _iota((1, cols), 1) < heads
    for r in range(rb):
        qt = qt_ref[r] * scale
        s = jnp.zeros((nm, cols), F32)
        for h in range(heads):
            kh = mk_ref[r, pl.ds(h, nm, stride=heads), :].astype(BF16)
            s = s + _dot(kh, jnp.where(col == h, qt, 0.0).astype(BF16))
        e = jnp.exp(s - jnp.max(s, axis=0, keepdims=True))
        p = jnp.where(valid, e / jnp.sum(e, axis=0, keepdims=True), 0.0).astype(BF16)
        outs = []
        for h in range(heads):
            pe = _dot(p, (erow == h).astype(BF16))
            vh = mv_ref[r, pl.ds(h, nm, stride=heads), :]
            outs.append(jnp.sum((pe * vh).reshape(nm // 8, 8, hd), axis=0))
        o_ref[r] = jnp.sum(jnp.concatenate(outs, axis=1), axis=0, keepdims=True)


def _xattn_sample(q_t, mem_k, mem_v, layer, heads, rb):
    r, hd, cols = q_t.shape
    rows = mem_k.shape[2]
    mem = pl.BlockSpec((None, rb, rows, hd), lambda i: (layer, i, 0, 0))
    return pl.pallas_call(
        functools.partial(_xattn_sample_kernel, scale=X_HEAD_DIM ** -0.5, heads=heads), grid=(r // rb,),
        in_specs=[pl.BlockSpec((rb, hd, cols), lambda i: (i, 0, 0)), mem, mem],
        out_specs=pl.BlockSpec((rb, 1, heads * hd), lambda i: (i, 0, 0)),
        out_shape=jax.ShapeDtypeStruct((r, 1, heads * hd), F32),
        compiler_params=_params("parallel"), name="xattn_sample",
    )(q_t, mem_k, mem_v)


def _xo_peerq_kernel(x1_ref, ctx_ref, wxo_ref, lnf_ref, wq_ref, x2_ref, h_ref, pq_ref):
    x2 = x1_ref[...] + _dot(ctx_ref[...].astype(BF16), wxo_ref[...])
    x2_ref[...] = x2
    h = _rms(x2, lnf_ref[...]).astype(BF16)
    h_ref[...] = h
    pq_ref[...] = _dot(h, wq_ref[...])


def _xo_peerq(x1, ctx, w_xo, ln_ffn, peer_wq, tm):
    n, d = x1.shape
    qw = peer_wq.shape[1]
    full = lambda a: pl.BlockSpec(a.shape, lambda i: (0,) * a.ndim)
    row = lambda c: pl.BlockSpec((tm, c), lambda i: (i, 0))
    return pl.pallas_call(
        _xo_peerq_kernel, grid=(n // tm,),
        in_specs=[row(d), row(ctx.shape[1]), full(w_xo), full(ln_ffn), full(peer_wq)],
        out_specs=[row(d), row(d), row(qw)],
        out_shape=[jax.ShapeDtypeStruct((n, d), F32), jax.ShapeDtypeStruct((n, d), BF16),
                   jax.ShapeDtypeStruct((n, qw), F32)],
        compiler_params=_params("parallel"), name="xo_peerq",
    )(x1, ctx, w_xo, ln_ffn, peer_wq)


def _top16(s):
    kk = s.shape[0]
    kio = _iota(s.shape, 0).astype(F32)
    vals, idxs = [], []
    for _ in range(PEER_TOPK):
        m = jnp.max(s, axis=0, keepdims=True)
        idx = jnp.min(jnp.where(s == m, kio, float(kk)), axis=0, keepdims=True)
        s = jnp.where(kio == idx, NEG_INF, s)
        vals.append(m)
        idxs.append(idx)
    return jnp.concatenate(vals, axis=0), jnp.concatenate(idxs, axis=0).astype(I32)


def _gather16(x, idx):
    lo, hi = x[:8], x[8:]
    out = []
    for part in (idx[:8], idx[8:]):
        low3 = part & 7
        out.append(jnp.where(part < 8, jnp.take_along_axis(lo, low3, axis=0), jnp.take_along_axis(hi, low3, axis=0)))
    return jnp.concatenate(out, axis=0)


def _route_kernel(pq_ref, sk_ref, i_ref, j_ref, g_ref):
    half = sk_ref.shape[2]
    tt = min(2 * LANES, pq_ref.shape[0])
    k = PEER_TOPK

    def sub_tile(n, carry):
        r0 = pl.multiple_of(n * tt, tt)
        sv, si = [], []
        for p in range(2):
            q = pq_ref[pl.ds(r0, tt), p * half:(p + 1) * half].astype(BF16)
            st = lax.dot_general(sk_ref[p], q, _NT, preferred_element_type=F32)
            v, i = _top16(st)
            sv.append(v)
            si.append(i)
        sub = 8
        rows8 = _iota((sub, tt), 0)
        cands = [sv[0][0:1, :] + sv[1]]
        for a in range(1, sub):
            cands.append(jnp.where(rows8 < k // (a + 1), sv[0][a:a + 1, :] + sv[1][:sub, :], NEG_INF))
        cands.append(sv[0][sub:, :] + sv[1][0:1, :])
        cand = jnp.concatenate(cands, axis=0)
        nrows = cand.shape[0]
        pio = _iota(cand.shape, 0).astype(F32)
        tv, tp = [], []
        for _ in range(k):
            m = jnp.max(cand, axis=0, keepdims=True)
            pos = jnp.min(jnp.where(cand == m, pio, float(nrows)), axis=0, keepdims=True)
            cand = jnp.where(pio == pos, NEG_INF, cand)
            tv.append(m)
            tp.append(pos)
        top = jnp.concatenate(tv, axis=0)
        e = jnp.exp(top - top[0:1, :])
        g_ref[:, pl.ds(r0, tt)] = e / jnp.sum(e, axis=0, keepdims=True)
        pos = jnp.concatenate(tp, axis=0).astype(I32)
        mid = pos - k
        first, last = pos < k, pos >= nrows - (k - sub)
        a_sel = jnp.where(first, 0, jnp.where(last, pos - (nrows - k), 1 + (mid >> 3)))
        b_sel = jnp.where(first, pos, jnp.where(last, 0, mid & (sub - 1)))
        i_ref[:, pl.ds(r0, tt)] = _gather16(si[0], a_sel)
        j_ref[:, pl.ds(r0, tt)] = _gather16(si[1], b_sel)
        return carry

    lax.fori_loop(0, pq_ref.shape[0] // tt, sub_tile, 0)


def _route(pq, subkeys, tt):
    n = pq.shape[0]
    hp, keys, half = subkeys.shape
    heads = hp // 2
    k = PEER_TOPK
    out = pl.BlockSpec((k, tt), lambda i, h: (h, i))
    return pl.pallas_call(
        _route_kernel, grid=(n // tt, heads),
        in_specs=[pl.BlockSpec((tt, 2 * half), lambda i, h: (i, h)),
                  pl.BlockSpec((2, keys, half), lambda i, h: (h, 0, 0))],
        out_specs=[out, out, out],
        out_shape=[jax.ShapeDtypeStruct((heads * k, n), I32), jax.ShapeDtypeStruct((heads * k, n), I32),
                   jax.ShapeDtypeStruct((heads * k, n), F32)],
        compiler_params=_params("parallel", "parallel"), name="peer_route",
    )(pq, subkeys)


def _peer_u_kernel(h_ref, ut_ref, i_ref, j_ref, a_ref, *, ib):
    step = pl.program_id(1)

    @pl.when(step == 0)
    def _():
        a_ref[...] = jnp.zeros_like(a_ref)

    isel = i_ref[...]
    jsel = j_ref[...]
    acc = a_ref[...]
    nj = jsel.shape[1]
    h = h_ref[...]
    group = 4
    for g0 in range(0, ib, group):
        a_all = _dot(h, ut_ref[:, g0 * nj:(g0 + group) * nj])
        for ii in range(group):
            a = a_all[:, ii * nj:(ii + 1) * nj]
            acc = jnp.where(isel == step * ib + g0 + ii, jnp.take_along_axis(a, jsel, axis=1), acc)
    a_ref[...] = acc


def _peer_u(h, ut, isel, jsel, tt, ib):
    n, d = h.shape
    slots = isel.shape[1]
    nblk = ut.shape[1] // (slots * ib)
    tok = lambda c: pl.BlockSpec((tt, c), lambda t, e: (t, 0))
    return pl.pallas_call(
        functools.partial(_peer_u_kernel, ib=ib), grid=(n // tt, nblk),
        in_specs=[tok(d), pl.BlockSpec((d, slots * ib), lambda t, e: (0, e)), tok(slots), tok(slots)],
        out_specs=tok(slots), out_shape=jax.ShapeDtypeStruct((n, slots), F32),
        compiler_params=_params("parallel", "arbitrary"), name="peer_u",
    )(h, ut, isel, jsel)


def _peer_v_kernel(i_ref, j_ref, g_ref, a_ref, x_ref, v_ref, o_ref, z_sc, zs_sc, *, ib, stride):
    step = pl.program_id(1)
    tt, slots = i_ref.shape

    @pl.when(step == 0)
    def _():
        a = a_ref[...]
        z_sc[...] = g_ref[...] * (0.5 * a * (1.0 + lax.erf(a * (2.0 ** -0.5))))

        def scatter(t, carry):
            irow = i_ref[pl.ds(t, 1), :]
            jrow = j_ref[pl.ds(t, 1), :]
            zrow = z_sc[pl.ds(t, 1), :]
            io = _iota((slots, slots), 0)
            zit = jnp.where(irow == io, zrow, 0.0).astype(BF16)
            oht = jnp.where(jrow == io, 1.0, 0.0).astype(BF16)
            zs_sc[pl.ds(pl.multiple_of(t * stride, 8), slots), :] = lax.dot_general(
                zit, oht, _NT, preferred_element_type=F32)
            return carry

        lax.fori_loop(0, tt, scatter, 0, unroll=32)
        o_ref[...] = x_ref[...]

    base = step * ib
    zblk = jnp.concatenate(
        [zs_sc[pl.ds(base + ii, tt, stride=stride), :].astype(BF16) for ii in range(ib)], axis=1)
    o_ref[...] += _dot(zblk, v_ref[...])


def _peer_v(isel, jsel, gates, act, x, v, tt, ib):
    n, d = x.shape
    slots = isel.shape[1]
    nblk = v.shape[0] // (slots * ib)
    stride = slots + 8
    tok = lambda c: pl.BlockSpec((tt, c), lambda t, e: (t, 0))
    return pl.pallas_call(
        functools.partial(_peer_v_kernel, ib=ib, stride=stride), grid=(n // tt, nblk),
        in_specs=[tok(slots)] * 4 + [tok(d), pl.BlockSpec((slots * ib, d), lambda t, e: (e, 0))],
        out_specs=tok(d), out_shape=jax.ShapeDtypeStruct((n, d), F32),
        scratch_shapes=[pltpu.VMEM((tt, slots), F32), pltpu.VMEM((tt * stride, slots), F32)],
        compiler_params=_params("parallel", "arbitrary"), name="peer_v",
    )(isel, jsel, gates, act, x, v)


def _final_kernel(x_ref, g_ref, o_ref):
    o_ref[...] = _rms(x_ref[...], g_ref[...])


def _final_norm(x, g, tm):
    n, d = x.shape
    return pl.pallas_call(
        _final_kernel, grid=(n // tm,),
        in_specs=[pl.BlockSpec((tm, d), lambda i: (i, 0)), pl.BlockSpec((1, d), lambda i: (0, 0))],
        out_specs=pl.BlockSpec((tm, d), lambda i: (i, 0)), out_shape=jax.ShapeDtypeStruct((n, d), F32),
        compiler_params=_params("parallel"), name="final_norm",
    )(x, g)


def _tile(n, name):
    pref = TILE[name]
    return pref if n % pref == 0 else n


def _peer(x2, h3, pq, lw):
    n = x2.shape[0]
    it, jt, gt = _route(pq, lw["subkeys"], _tile(n, "route"))
    isel, jsel, gates = it.T, jt.T, gt.T
    act = _peer_u(h3, lw["ut"], isel, jsel, _tile(n, "peer_u"), TILE["peer_blocks"])
    return _peer_v(isel, jsel, gates, act, x2, lw["v"], _tile(n, "peer_v"), TILE["peer_blocks"])


def kernel(x_prompt, x_sample, cache_k, cache_v, cache_logf, state_hg, cache_mem_k, cache_mem_v, page_table, mem_prompt, ln_mix, w_in, b_fox_f, hg_lb, fox_gn, hg_gn, w_out, ln_x, ln_mem, w_xq, w_xk, w_xv, w_xo, ln_ffn, peer_wq, peer_subkeys, peer_u, peer_v, ln_final):
    depth, d_model, _ = w_in.shape
    bp, seq, _ = x_prompt.shape
    bs = x_sample.shape[0]
    fox_heads = b_fox_f.shape[1]
    fox_w = fox_gn.shape[1]
    hg_w = hg_gn.shape[1]
    hg_heads = hg_w // HG_DK
    n_phys, page = cache_k.shape[1], cache_k.shape[2]
    n_mem = mem_prompt.shape[1]
    x_w = w_xq.shape[2]
    x_heads = x_w // X_HEAD_DIM
    p_heads = peer_subkeys.shape[1]
    p_keys, p_half = peer_subkeys.shape[3], peer_subkeys.shape[4]
    pairs = fox_w // LANES

    lb_all = jnp.cumsum(jax.nn.softmax(hg_lb.astype(F32), axis=0), axis=0)
    row = lambda a: a.reshape(1, -1).astype(F32)

    xp = x_prompt.reshape(bp * seq, d_model)
    xs = x_sample.reshape(bs, d_model)
    mem = mem_prompt.reshape(bp * n_mem, d_model)
    ck_t = cache_k.transpose(0, 1, 3, 4, 2).reshape(depth, n_phys, fox_w, page)
    cv_t = cache_v.transpose(0, 1, 3, 4, 2).reshape(depth, n_phys, fox_w, page)
    clf_t = cache_logf.transpose(0, 1, 3, 2)
    state_t = state_hg.transpose(0, 2, 3, 4, 1)
    memk = cache_mem_k.reshape(depth, bs, n_mem * x_heads, X_HEAD_DIM)
    memv = cache_mem_v.reshape(depth, bs, n_mem * x_heads, X_HEAD_DIM)

    outs = {k: [] for k in ("kp", "vp", "fp", "hp", "mkp", "mvp", "ks", "vs", "fs", "hs")}
    for l in range(depth):
        wl = w_in[l]
        c0 = 3 * fox_w
        lw = dict(
            w_main=jnp.concatenate([wl[:, :c0], wl[:, c0 + fox_heads:]], axis=1).astype(BF16),
            w_ff=jnp.pad(wl[:, c0:c0 + fox_heads], ((0, 0), (0, LANES - fox_heads))).astype(BF16),
            b_f=jnp.pad(row(b_fox_f[l]), ((0, 0), (0, LANES - fox_heads))),
            lb=row(lb_all[l] - lb_all[0]),
            subkeys=peer_subkeys[l].reshape(p_heads * 2, p_keys, p_half).astype(BF16),
            ut=peer_u[l].astype(BF16).T, v=peer_v[l].astype(BF16),
        )
        w_out_b, w_xq_b, w_xo_b = w_out[l].astype(BF16), w_xq[l].astype(BF16), w_xo[l].astype(BF16)
        peer_wq_b = peer_wq[l].astype(BF16)

        tm = _tile(bp * seq, "tokenwise")
        fq, fk, fv, flf, hq, hk, hv, hlf, hgate = _inproj(xp, row(ln_mix[l]), lw["w_main"], lw["w_ff"], lw["b_f"],
                                                           lw["lb"], tm, fox_heads)
        lf_t = flf.reshape(bp, seq, fox_heads).transpose(0, 2, 1).reshape(bp * fox_heads, seq)
        c_t = _cumsum_lanes(lf_t, _tile(seq, "cumsum"))
        ck = c_t.reshape(bp * pairs, 2, seq)
        cq = c_t.reshape(bp, pairs, 2, seq).transpose(1, 0, 3, 2).reshape(pairs, bp * seq, 2)
        fo = _fox_prompt(fq, fk, fv, cq, ck, bp, _tile(seq, "fox"))
        ho, st = _hgrn_prompt(hq, hk, hv, hlf, bp)
        x1, qx = _merge(fo, ho, hgate, xp, row(fox_gn[l]), row(hg_gn[l]), w_out_b, row(ln_x[l]), w_xq_b, tm)
        mk, mv = _memkv(mem, row(ln_mem[l]), w_xk[l].astype(BF16), w_xv[l].astype(BF16), n_mem)
        ctx = _xattn_prompt(qx, mk, mv, bp, tm)
        x2, h3, pq = _xo_peerq(x1, ctx, w_xo_b, row(ln_ffn[l]), peer_wq_b, tm)
        xp = _peer(x2, h3, pq, lw)
        outs["kp"].append(fk.reshape(bp, seq, fox_heads, HEAD_DIM))
        outs["vp"].append(fv.reshape(bp, seq, fox_heads, HEAD_DIM))
        outs["fp"].append(flf.reshape(bp, seq, fox_heads))
        st5 = st.reshape(bp, pairs, 2, HG_DK, 2, HG_DK)
        s_heads = jnp.stack([st5[:, :, 0, :, 0, :], st5[:, :, 1, :, 1, :]], axis=2)
        outs["hp"].append(s_heads.reshape(bp, hg_heads, HG_DK, HG_DK).transpose(0, 1, 3, 2))
        outs["mkp"].append(mk.reshape(bp, n_mem, x_heads, X_HEAD_DIM))
        outs["mvp"].append(mv.reshape(bp, n_mem, x_heads, X_HEAD_DIM))

        sq, sk, sv, slf, tq, tk, tv, tlf, tgate = _inproj(xs, row(ln_mix[l]), lw["w_main"], lw["w_ff"], lw["b_f"],
                                                           lw["lb"], bs, fox_heads)
        tok3 = lambda a: a.reshape(bs, 1, a.shape[1])
        colm = lambda a: a.reshape(bs, a.shape[1], 1)
        fo_s = _fox_decode(page_table, colm(sq), colm(sk), colm(sv), colm(slf), ck_t, cv_t, clf_t,
                           l).reshape(bs, fox_w)
        lanes_r = lambda a: a.T.reshape(hg_heads, HG_DK, 1, bs)
        ho_t, s_new = _hgrn_step(lanes_r(tq), lanes_r(tk), lanes_r(tlf), tv.T.reshape(hg_heads, HG_DK, bs),
                                 state_t, l)
        x1s, qxs = _merge(fo_s, ho_t.reshape(hg_w, bs).T, tgate, xs, row(fox_gn[l]), row(hg_gn[l]), w_out_b,
                          row(ln_x[l]), w_xq_b, bs)
        qxs_t = jnp.pad(qxs.reshape(bs, x_heads, X_HEAD_DIM).transpose(0, 2, 1), ((0, 0), (0, 0), (0, 8 - x_heads)))
        ctx_s = _xattn_sample(qxs_t, memk, memv, l, x_heads, _tile(bs, "xattn_sample")).reshape(bs, x_w)
        x2s, h3s, pqs = _xo_peerq(x1s, ctx_s, w_xo_b, row(ln_ffn[l]), peer_wq_b, bs)
        xs = _peer(x2s, h3s, pqs, lw)
        outs["ks"].append(sk.reshape(bs, 1, fox_heads, HEAD_DIM))
        outs["vs"].append(sv.reshape(bs, 1, fox_heads, HEAD_DIM))
        outs["fs"].append(slf.reshape(bs, 1, fox_heads))
        outs["hs"].append(s_new)

    y_prompt = _final_norm(xp, row(ln_final), _tile(bp * seq, "tokenwise")).reshape(bp, seq, d_model)
    y_sample = _final_norm(xs, row(ln_final), bs).reshape(bs, 1, d_model)
    st_ = lambda k: jnp.stack(outs[k])
    return (y_prompt, y_sample, st_("kp"), st_("vp"), st_("fp"), st_("hp"), st_("mkp"), st_("mvp"),
            st_("ks"), st_("vs"), st_("fs"), st_("hs").transpose(0, 4, 1, 2, 3))
```

```python
import functools
import math

import jax
import jax.numpy as jnp
from jax import lax
from jax.experimental import pallas as pl
from jax.experimental.pallas import tpu as pltpu

F32 = jnp.float32
BF16 = jnp.bfloat16
I32 = jnp.int32
EPS = 1e-6
NEG_INF = float("-inf")

HEAD_DIM = 64
HG_DK = 64
HG_CHUNK = 64
HG_SUB = 16
X_HEAD_DIM = 128
PEER_TOPK = 16
LANES = 128
V7X_VMEM_BYTES = 64 * 1024 * 1024
VMEM_LIMIT = V7X_VMEM_BYTES - 8 * 1024 * 1024

TILE = dict(
    tokenwise=512,
    cumsum=512,
    fox=1024,
    route=1024,
    peer_u=1024,
    peer_v=512,
    peer_blocks=16,
    xattn_sample=8,
)

_NT = (((1,), (1,)), ((), ()))
_TN = (((0,), (0,)), ((), ()))


def _params(*sem):
    return pltpu.CompilerParams(dimension_semantics=sem, vmem_limit_bytes=VMEM_LIMIT)


def _rms(x, g):
    ms = jnp.mean(x * x, axis=-1, keepdims=True)
    return x * lax.rsqrt(ms + EPS) * g


def _log_sigmoid(x):
    return jnp.minimum(x, 0.0) - jnp.log1p(jnp.exp(-jnp.abs(x)))


def _dot(a, b):
    return jnp.dot(a, b, preferred_element_type=F32)


def _split3(x):
    hi = x.astype(BF16)
    r = x - hi.astype(F32)
    mid = r.astype(BF16)
    lo = (r - mid.astype(F32)).astype(BF16)
    return hi, mid, lo


def _prefix_rows(tril, x):
    return _dot(jnp.concatenate([tril] * 3, axis=1), jnp.concatenate(_split3(x), axis=0))


def _prefix_lanes(x, triu):
    return _dot(jnp.concatenate(_split3(x), axis=1), jnp.concatenate([triu] * 3, axis=0))


def _iota(shape, axis):
    return lax.broadcasted_iota(I32, shape, axis)


def _inproj_kernel(x_ref, g_ref, w_ref, wff_ref, bf_ref, lb_ref,
                   fq_ref, fk_ref, fv_ref, lf_ref, hq_ref, hk_ref, hv_ref, hlf_ref, gate_ref, *, fox_heads):
    h = _rms(x_ref[...], g_ref[...]).astype(BF16)
    w = w_ref.shape[1] // 7

    def mm(j):
        return _dot(h, w_ref[:, j * w:(j + 1) * w])

    fq_ref[...] = mm(0)
    fk_ref[...] = mm(1)
    fv_ref[...] = mm(2)
    ff = _dot(h, wff_ref[...])
    lf_ref[...] = _log_sigmoid(ff + bf_ref[...])[:, :fox_heads]
    hq_ref[...] = mm(3)
    z = mm(4)
    lb = lb_ref[...]
    a = jnp.log(lb)
    b = jnp.log1p(-lb) + _log_sigmoid(z)
    hlf_ref[...] = jnp.maximum(a, b) + jnp.log1p(jnp.exp(-jnp.abs(a - b)))
    hk_ref[...] = (1.0 - lb) * (1.0 / (1.0 + jnp.exp(z)))
    hv_ref[...] = mm(5)
    gate_ref[...] = mm(6)


def _inproj(x, g, w_main, w_ff, b_f, lb, tm, fox_heads):
    n, d = x.shape
    w = w_main.shape[1] // 7
    full = lambda a: pl.BlockSpec(a.shape, lambda i: (0,) * a.ndim)
    row = lambda c: pl.BlockSpec((tm, c), lambda i: (i, 0))
    outs = [jax.ShapeDtypeStruct((n, w), F32)] * 3 + [jax.ShapeDtypeStruct((n, fox_heads), F32)] + \
           [jax.ShapeDtypeStruct((n, w), F32)] * 5
    out_specs = [row(w)] * 3 + [row(fox_heads)] + [row(w)] * 5
    return pl.pallas_call(
        functools.partial(_inproj_kernel, fox_heads=fox_heads),
        grid=(n // tm,),
        in_specs=[row(d), full(g), full(w_main), full(w_ff), full(b_f), full(lb)],
        out_specs=out_specs, out_shape=outs,
        compiler_params=_params("parallel"), name="inproj",
    )(x, g, w_main, w_ff, b_f, lb)


def _cumsum_kernel(x_ref, o_ref, carry_ref):
    @pl.when(pl.program_id(0) == 0)
    def _():
        carry_ref[...] = jnp.zeros_like(carry_ref)

    x = x_ref[...]
    n = x.shape[1]
    tri = (_iota((n, n), 0) <= _iota((n, n), 1)).astype(BF16)
    c = _prefix_lanes(x, tri) + carry_ref[:, :1]
    o_ref[...] = c
    carry_ref[...] = jnp.broadcast_to(c[:, n - 1:n], carry_ref.shape)


def _cumsum_lanes(x, tc):
    r, t = x.shape
    return pl.pallas_call(
        _cumsum_kernel, grid=(t // tc,),
        in_specs=[pl.BlockSpec((r, tc), lambda i: (0, i))],
        out_specs=pl.BlockSpec((r, tc), lambda i: (0, i)),
        out_shape=jax.ShapeDtypeStruct((r, t), F32),
        scratch_shapes=[pltpu.VMEM((r, LANES), F32)],
        compiler_params=_params("arbitrary"), name="logf_cumsum",
    )(x)


def _fox_kernel(qt_ref, kt_ref, q_ref, k_ref, v_ref, cq_ref, ck_ref, o_ref, m_ref, cqr_ref, acc_ref, qh_ref, *,
                scale):
    qi = qt_ref[pl.program_id(2)]
    ki = kt_ref[pl.program_id(2)]
    tq = q_ref.shape[0]
    tk = k_ref.shape[0]
    lane = _iota((1, LANES), 1)
    log2e = math.log2(math.e)

    @pl.when(ki == 0)
    def _():
        m_ref[...] = jnp.full_like(m_ref, NEG_INF)
        acc_ref[...] = jnp.zeros_like(acc_ref)
        q = q_ref[...] * (scale * log2e)
        for hh in range(2):
            cqr_ref[hh] = jnp.broadcast_to(cq_ref[:, hh:hh + 1] * log2e, (tq, LANES))
            qh_ref[hh] = jnp.where((lane // HEAD_DIM) == hh, q, 0.0).astype(BF16)

    def step(masked, blocks):
        k = k_ref[...].astype(BF16)
        v = v_ref[...]
        ck = ck_ref[...] * log2e
        for hh in range(2):
            vh = jnp.where((lane // HEAD_DIM) == hh, v, 1.0).astype(BF16)
            for r0, r1, nk in blocks:
                s = lax.dot_general(qh_ref[hh, r0:r1], k[:nk], _NT, preferred_element_type=F32) - ck[hh:hh + 1, :nk]
                if masked:
                    causal = (ki * tk + _iota((1, nk), 1)) <= (qi * tq + r0 + _iota((r1 - r0, 1), 0))
                    s = jnp.where(causal, s, NEG_INF)
                m_prev = m_ref[hh, r0:r1]
                cq = cqr_ref[hh, r0:r1]
                m_new = jnp.maximum(m_prev, jnp.max(s, axis=1, keepdims=True) + cq)
                p = jnp.exp2(s - jnp.tile(m_new - cq, (1, nk // LANES)))
                acc_ref[hh, r0:r1] = jnp.exp2(m_prev - m_new) * acc_ref[hh, r0:r1] + _dot(p.astype(BF16), vh[:nk])
                m_ref[hh, r0:r1] = m_new

    @pl.when(ki < qi)
    def _():
        step(False, [(0, tq, tk)])

    @pl.when(ki == qi)
    def _():
        half = tq // 2
        step(True, [(0, half, tk // 2), (half, tq, tk)] if half % LANES == 0 else [(0, tq, tk)])
        a0 = acc_ref[0]
        a1 = acc_ref[1]
        o0 = a0 / pltpu.roll(a0, HEAD_DIM, axis=1)
        o1 = a1 / pltpu.roll(a1, HEAD_DIM, axis=1)
        o_ref[...] = jnp.where(lane < HEAD_DIM, o0, o1)


def _fox_prompt(q, k, v, cq, ck, batch, tq):
    n, w = q.shape
    t = n // batch
    pairs = w // LANES
    nq = t // tq
    kern = functools.partial(_fox_kernel, scale=HEAD_DIM ** -0.5)
    steps = [(i, j) for i in range(nq) for j in range(i + 1)]
    qi_tab = jnp.asarray([s[0] for s in steps], I32)
    ki_tab = jnp.asarray([s[1] for s in steps], I32)
    grid_spec = pltpu.PrefetchScalarGridSpec(
        num_scalar_prefetch=2, grid=(batch, pairs, len(steps)),
        in_specs=[
            pl.BlockSpec((tq, LANES), lambda b, g, s, qt, kt: (b * nq + qt[s], g)),
            pl.BlockSpec((tq, LANES), lambda b, g, s, qt, kt: (b * nq + kt[s], g)),
            pl.BlockSpec((tq, LANES), lambda b, g, s, qt, kt: (b * nq + kt[s], g)),
            pl.BlockSpec((None, tq, 2), lambda b, g, s, qt, kt: (g, b * nq + qt[s], 0)),
            pl.BlockSpec((None, 2, tq), lambda b, g, s, qt, kt: (b * pairs + g, 0, kt[s])),
        ],
        out_specs=pl.BlockSpec((tq, LANES), lambda b, g, s, qt, kt: (b * nq + qt[s], g)),
        scratch_shapes=[pltpu.VMEM((2, tq, LANES), F32)] * 3 + [pltpu.VMEM((2, tq, LANES), BF16)],
    )
    return pl.pallas_call(
        kern, grid_spec=grid_spec, out_shape=jax.ShapeDtypeStruct((n, w), F32),
        compiler_params=_params("parallel", "parallel", "arbitrary"), name="fox_prompt",
    )(qi_tab, ki_tab, q, k, v, cq, ck)


def _hgrn_kernel(q_ref, k_ref, v_ref, lf_ref, o_ref, st_ref, st_sc):
    n = pl.program_id(0)
    nb, c, w = q_ref.shape
    pairs = w // LANES

    @pl.when(n == 0)
    def _():
        st_sc[...] = jnp.zeros_like(st_sc)

    tril = (_iota((c, c), 1) <= _iota((c, c), 0)).astype(BF16)
    same_head = (_iota((LANES, LANES), 0) // HG_DK) == (_iota((LANES, LANES), 1) // HG_DK)
    seg = same_head.astype(BF16)
    sc = HG_SUB
    ti = _iota((sc, sc, LANES), 0)
    si = _iota((sc, sc, LANES), 1)
    first_head = _iota((1, LANES), 1) < HG_DK
    for bg in range(nb * pairs):
        r, g = divmod(bg, pairs)
        sl = slice(g * LANES, (g + 1) * LANES)
        q = q_ref[r, :, sl]
        k = k_ref[r, :, sl]
        v = v_ref[r, :, sl]
        vb = v.astype(BF16)
        b = _prefix_rows(tril, lf_ref[r, :, sl])
        st = st_sc[r, g]
        qe = (q * jnp.exp(b)).astype(BF16)
        o = lax.dot_general(qe, st.astype(BF16), _NT, preferred_element_type=F32)
        rows = []
        for i in range(c // sc):
            lo, hi = i * sc, (i + 1) * sc
            qi, ki, bi = q[lo:hi], k[lo:hi], b[lo:hi]
            decay = jnp.exp(jnp.where(si <= ti, bi[:, None, :] - bi[None, :, :], NEG_INF))
            p = (qi[:, None, :] * ki[None, :, :]) * decay
            a = _dot(p.reshape(sc * sc, LANES).astype(BF16), seg)
            oi = jnp.sum(a.reshape(sc, sc, LANES) * v[None, lo:hi, :], axis=1)
            if i > 0:
                br = b[lo - 1:lo, :]
                qt = qi * jnp.exp(bi - br)
                q2 = jnp.concatenate([jnp.where(first_head, qt, 0.0), jnp.where(first_head, 0.0, qt)], axis=0)
                ks = (k[:lo] * jnp.exp(br - b[:lo])).astype(BF16)
                a2 = lax.dot_general(q2.astype(BF16), ks, _NT, preferred_element_type=F32)
                o2 = _dot(a2.astype(BF16), vb[:lo])
                oi = oi + jnp.where(first_head, o2[:sc], o2[sc:])
            rows.append(oi)
        o_ref[r, :, sl] = o + jnp.concatenate(rows, axis=0)
        b_last = b[c - 1:c, :]
        kd = (k * jnp.exp(b_last - b)).astype(BF16)
        upd = lax.dot_general(vb, kd, _TN, preferred_element_type=F32)
        st_sc[r, g] = st * jnp.exp(b_last) + jnp.where(same_head, upd, 0.0)

    @pl.when(n == pl.num_programs(0) - 1)
    def _():
        st_ref[...] = st_sc[...]


def _hgrn_prompt(q, k, v, lf, batch):
    n, w = q.shape
    t = n // batch
    pairs = w // LANES
    rows3 = lambda a: a.reshape(batch, t, w)
    blk = pl.BlockSpec((batch, HG_CHUNK, w), lambda i: (0, i, 0))
    st_shape = (batch, pairs, LANES, LANES)
    o, st = pl.pallas_call(
        _hgrn_kernel, grid=(t // HG_CHUNK,),
        in_specs=[blk] * 4,
        out_specs=[blk, pl.BlockSpec(st_shape, lambda i: (0, 0, 0, 0))],
        out_shape=[jax.ShapeDtypeStruct((batch, t, w), F32), jax.ShapeDtypeStruct(st_shape, F32)],
        scratch_shapes=[pltpu.VMEM(st_shape, F32)],
        compiler_params=_params("arbitrary"), name="hgrn_prompt",
    )(rows3(q), rows3(k), rows3(v), rows3(lf))
    return o.reshape(n, w), st


def _fox_decode_kernel(pt_ref, q_ref, kn_ref, vn_ref, lfn_ref, *refs, scale, n_pages):
    k_refs = refs[:n_pages]
    v_refs = refs[n_pages:2 * n_pages]
    lf_refs = refs[2 * n_pages:3 * n_pages]
    o_ref = refs[3 * n_pages]
    w, ps = k_refs[0].shape
    nh = lf_refs[0].shape[0]
    hd = w // nh
    heads = lambda x: x.reshape(nh, hd, x.shape[1])
    q = q_ref[...] * scale
    qb = jnp.broadcast_to(q, (w, ps))
    triu = (_iota((ps, ps), 0) <= _iota((ps, ps), 1)).astype(BF16)

    carry = jnp.zeros((nh, 1), F32)
    scores = []
    for p in range(n_pages):
        s = jnp.sum(heads(k_refs[p][...] * qb), axis=1)
        within = _prefix_lanes(lf_refs[p][...], triu)
        scores.append(s - (within + carry))
        carry = carry + within[:, ps - 1:ps]
    s_new = jnp.sum(heads(kn_ref[...] * q), axis=1) - (carry + lfn_ref[...])

    m_tile = scores[0]
    for s in scores[1:]:
        m_tile = jnp.maximum(m_tile, s)
    m = jnp.maximum(s_new, jnp.max(m_tile, axis=1, keepdims=True))
    e_new = jnp.exp(s_new - m)
    exps = [jnp.exp(s - m) for s in scores]
    l_tile = exps[0]
    for e in exps[1:]:
        l_tile = l_tile + e
    inv = 1.0 / (e_new + jnp.sum(l_tile, axis=1, keepdims=True))
    acc = jnp.zeros((w, ps), F32)
    for p in range(n_pages):
        pb = jnp.broadcast_to(exps[p][:, None, :], (nh, hd, ps)).reshape(w, ps)
        acc = acc + v_refs[p][...] * pb
    rep = lambda x: jnp.broadcast_to(x[:, None, :], (nh, hd, 1)).reshape(w, 1)
    o_ref[...] = (jnp.sum(acc, axis=1, keepdims=True) + rep(e_new) * vn_ref[...]) * rep(inv)


def _fox_decode(page_table, q, k_new, v_new, lf_new, cache_kt, cache_vt, cache_lft, layer):
    r, w, _ = q.shape
    nh = lf_new.shape[1]
    n_pages = page_table.shape[1]
    ps = cache_kt.shape[3]
    pt = page_table.reshape(-1)
    tok = lambda c: pl.BlockSpec((None, c, 1), lambda i, pt: (i, 0, 0))

    def page(p, rows):
        return pl.BlockSpec((None, None, rows, ps), lambda i, pt: (layer, pt[i * n_pages + p], 0, 0))

    pages = lambda rows: [page(p, rows) for p in range(n_pages)]
    grid_spec = pltpu.PrefetchScalarGridSpec(
        num_scalar_prefetch=1, grid=(r,),
        in_specs=[tok(w), tok(w), tok(w), tok(nh)] + pages(w) + pages(w) + pages(nh),
        out_specs=tok(w),
    )
    return pl.pallas_call(
        functools.partial(_fox_decode_kernel, scale=HEAD_DIM ** -0.5, n_pages=n_pages),
        grid_spec=grid_spec, out_shape=jax.ShapeDtypeStruct((r, w, 1), F32),
        compiler_params=_params("parallel"), name="fox_decode",
    )(pt, q, k_new, v_new, lf_new, *([cache_kt] * n_pages), *([cache_vt] * n_pages), *([cache_lft] * n_pages))


def _hgrn_step_kernel(q_ref, k_ref, lf_ref, v_ref, s_ref, o_ref, so_ref):
    q = q_ref[...]
    k = k_ref[...]
    f = jnp.exp(lf_ref[...])
    v = v_ref[...]
    s = s_ref[...]
    so_ref[...] = f * s + k * v[None]
    o_ref[...] = jnp.sum((q * f) * s, axis=0) + jnp.sum(q * k, axis=0) * v


def _hgrn_step(q, k, lf, v, state_t, layer):
    h, dk, _, r = q.shape
    dv = v.shape[1]
    col = pl.BlockSpec((None, dk, 1, r), lambda i: (i, 0, 0, 0))
    rowv = pl.BlockSpec((None, dv, r), lambda i: (i, 0, 0))
    return pl.pallas_call(
        _hgrn_step_kernel, grid=(h,),
        in_specs=[col, col, col, rowv, pl.BlockSpec((None, None, dk, dv, r), lambda i: (layer, i, 0, 0, 0))],
        out_specs=[rowv, pl.BlockSpec((None, dk, dv, r), lambda i: (i, 0, 0, 0))],
        out_shape=[jax.ShapeDtypeStruct((h, dv, r), F32), jax.ShapeDtypeStruct((h, dk, dv, r), F32)],
        compiler_params=_params("parallel"), name="hgrn_step",
    )(q, k, lf, v, state_t)


def _head_mean_sq(y, width):
    w = y.shape[1]
    seg = ((_iota((w, w), 0) // width) == (_iota((w, w), 1) // width)).astype(BF16)
    sq = y * y
    hi = sq.astype(BF16)
    lo = (sq - hi.astype(F32)).astype(BF16)
    return (_dot(hi, seg) + _dot(lo, seg)) * (1.0 / width)


def _merge_kernel(fo_ref, ho_ref, gate_ref, x_ref, fgn_ref, hgn_ref, wo_ref, lnx_ref, wxq_ref, x1_ref, qx_ref):
    fo = fo_ref[...]
    ho = ho_ref[...]
    fw = fo.shape[1]
    fn = fo * lax.rsqrt(_head_mean_sq(fo, HEAD_DIM) + EPS) * fgn_ref[...]
    gate = gate_ref[...]
    hn = ho * lax.rsqrt(_head_mean_sq(ho, HEAD_DIM) + EPS) * hgn_ref[...] * (gate / (1.0 + jnp.exp(-gate)))
    y = _dot(fn.astype(BF16), wo_ref[:fw, :]) + _dot(hn.astype(BF16), wo_ref[fw:, :])
    x1 = x_ref[...] + y
    x1_ref[...] = x1
    qx_ref[...] = _dot(_rms(x1, lnx_ref[...]).astype(BF16), wxq_ref[...])


def _merge(fo, ho, gate, x, fgn, hgn, w_out, ln_x, w_xq, tm):
    n, d = x.shape
    xw = w_xq.shape[1]
    full = lambda a: pl.BlockSpec(a.shape, lambda i: (0,) * a.ndim)
    row = lambda c: pl.BlockSpec((tm, c), lambda i: (i, 0))
    return pl.pallas_call(
        _merge_kernel, grid=(n // tm,),
        in_specs=[row(fo.shape[1]), row(ho.shape[1]), row(gate.shape[1]), row(d),
                  full(fgn), full(hgn), full(w_out), full(ln_x), full(w_xq)],
        out_specs=[row(d), row(xw)],
        out_shape=[jax.ShapeDtypeStruct((n, d), F32), jax.ShapeDtypeStruct((n, xw), F32)],
        compiler_params=_params("parallel"), name="merge",
    )(fo, ho, gate, x, fgn, hgn, w_out, ln_x, w_xq)


def _memkv_kernel(m_ref, g_ref, wk_ref, wv_ref, k_ref, v_ref):
    m = _rms(m_ref[...], g_ref[...]).astype(BF16)
    k_ref[...] = _dot(m, wk_ref[...])
    v_ref[...] = _dot(m, wv_ref[...])


def _memkv(mem, g, wk, wv, tm):
    n, d = mem.shape
    xw = wk.shape[1]
    full = lambda a: pl.BlockSpec(a.shape, lambda i: (0,) * a.ndim)
    row = lambda c: pl.BlockSpec((tm, c), lambda i: (i, 0))
    return pl.pallas_call(
        _memkv_kernel, grid=(n // tm,),
        in_specs=[row(d), full(g), full(wk), full(wv)],
        out_specs=[row(xw), row(xw)],
        out_shape=[jax.ShapeDtypeStruct((n, xw), F32)] * 2,
        compiler_params=_params("parallel"), name="memkv",
    )(mem, g, wk, wv)


def _xattn_prompt_kernel(q_ref, mk_ref, mv_ref, o_ref, *, scale):
    heads = q_ref.shape[1] // X_HEAD_DIM
    for h in range(heads):
        sl = slice(h * X_HEAD_DIM, (h + 1) * X_HEAD_DIM)
        q = (q_ref[:, sl] * scale).astype(BF16)
        s = lax.dot_general(q, mk_ref[:, sl].astype(BF16), _NT, preferred_element_type=F32)
        e = jnp.exp(s - jnp.max(s, axis=1, keepdims=True))
        p = e / jnp.sum(e, axis=1, keepdims=True)
        o_ref[:, sl] = _dot(p.astype(BF16), mv_ref[:, sl].astype(BF16))


def _xattn_prompt(q, mk, mv, batch, tm):
    n, xw = q.shape
    t = n // batch
    nm = mk.shape[0] // batch
    nt = t // tm
    return pl.pallas_call(
        functools.partial(_xattn_prompt_kernel, scale=X_HEAD_DIM ** -0.5), grid=(batch, nt),
        in_specs=[pl.BlockSpec((tm, xw), lambda b, i: (b * nt + i, 0)),
                  pl.BlockSpec((nm, xw), lambda b, i: (b, 0)),
                  pl.BlockSpec((nm, xw), lambda b, i: (b, 0))],
        out_specs=pl.BlockSpec((tm, xw), lambda b, i: (b * nt + i, 0)),
        out_shape=jax.ShapeDtypeStruct((n, xw), F32),
        compiler_params=_params("parallel", "parallel"), name="xattn_prompt",
    )(q, mk, mv)


def _xattn_sample_kernel(qt_ref, mk_ref, mv_ref, o_ref, *, scale, heads):
    rb, hd, cols = qt_ref.shape
    nm = mk_ref.shape[1] // heads
    col = _iota((hd, cols), 1)
    erow = _iota((cols, hd), 0)
    valid = _iota((1, cols), 1) < heads
    for r in range(rb):
        qt = qt_ref[r] * scale
        s = jnp.zeros((nm, cols), F32)
        for h in range(heads):
            kh = mk_ref[r, pl.ds(h, nm, stride=heads), :].astype(BF16)
            s = s + _dot(kh, jnp.where(col == h, qt, 0.0).astype(BF16))
        e = jnp.exp(s - jnp.max(s, axis=0, keepdims=True))
        p = jnp.where(valid, e / jnp.sum(e, axis=0, keepdims=True), 0.0).astype(BF16)
        outs = []
        for h in range(heads):
            pe = _dot(p, (erow == h).astype(BF16))
            vh = mv_ref[r, pl.ds(h, nm, stride=heads), :]
            outs.append(jnp.sum((pe * vh).reshape(nm // 8, 8, hd), axis=0))
        o_ref[r] = jnp.sum(jnp.concatenate(outs, axis=1), axis=0, keepdims=True)


def _xattn_sample(q_t, mem_k, mem_v, layer, heads, rb):
    r, hd, cols = q_t.shape
    rows = mem_k.shape[2]
    mem = pl.BlockSpec((None, rb, rows, hd), lambda i: (layer, i, 0, 0))
    return pl.pallas_call(
        functools.partial(_xattn_sample_kernel, scale=X_HEAD_DIM ** -0.5, heads=heads), grid=(r // rb,),
        in_specs=[pl.BlockSpec((rb, hd, cols), lambda i: (i, 0, 0)), mem, mem],
        out_specs=pl.BlockSpec((rb, 1, heads * hd), lambda i: (i, 0, 0)),
        out_shape=jax.ShapeDtypeStruct((r, 1, heads * hd), F32),
        compiler_params=_params("parallel"), name="xattn_sample",
    )(q_t, mem_k, mem_v)


def _xo_peerq_kernel(x1_ref, ctx_ref, wxo_ref, lnf_ref, wq_ref, x2_ref, h_ref, pq_ref):
    x2 = x1_ref[...] + _dot(ctx_ref[...].astype(BF16), wxo_ref[...])
    x2_ref[...] = x2
    h = _rms(x2, lnf_ref[...]).astype(BF16)
    h_ref[...] = h
    pq_ref[...] = _dot(h, wq_ref[...]).astype(BF16)


def _xo_peerq(x1, ctx, w_xo, ln_ffn, peer_wq, tm):
    n, d = x1.shape
    qw = peer_wq.shape[1]
    full = lambda a: pl.BlockSpec(a.shape, lambda i: (0,) * a.ndim)
    row = lambda c: pl.BlockSpec((tm, c), lambda i: (i, 0))
    return pl.pallas_call(
        _xo_peerq_kernel, grid=(n // tm,),
        in_specs=[row(d), row(ctx.shape[1]), full(w_xo), full(ln_ffn), full(peer_wq)],
        out_specs=[row(d), row(d), row(qw)],
        out_shape=[jax.ShapeDtypeStruct((n, d), F32), jax.ShapeDtypeStruct((n, d), BF16),
                   jax.ShapeDtypeStruct((n, qw), BF16)],
        compiler_params=_params("parallel"), name="xo_peerq",
    )(x1, ctx, w_xo, ln_ffn, peer_wq)


def _top16(s):
    kk = s.shape[0]
    kio = _iota(s.shape, 0).astype(F32)
    vals, idxs = [], []
    for _ in range(PEER_TOPK):
        m = jnp.max(s, axis=0, keepdims=True)
        idx = jnp.min(jnp.where(s == m, kio, float(kk)), axis=0, keepdims=True)
        s = jnp.where(kio == idx, NEG_INF, s)
        vals.append(m)
        idxs.append(idx)
    return jnp.concatenate(vals, axis=0), jnp.concatenate(idxs, axis=0).astype(I32)


def _gather16(x, idx):
    lo, hi = x[:8], x[8:]
    out = []
    for part in (idx[:8], idx[8:]):
        low3 = part & 7
        out.append(jnp.where(part < 8, jnp.take_along_axis(lo, low3, axis=0), jnp.take_along_axis(hi, low3, axis=0)))
    return jnp.concatenate(out, axis=0)


def _route_kernel(pq_ref, sk_ref, i_ref, j_ref, g_ref):
    half = sk_ref.shape[2]
    tt = min(2 * LANES, pq_ref.shape[0])
    k = PEER_TOPK

    def sub_tile(n, carry):
        r0 = pl.multiple_of(n * tt, tt)
        sv, si = [], []
        for p in range(2):
            q = pq_ref[pl.ds(r0, tt), p * half:(p + 1) * half]
            st = lax.dot_general(sk_ref[p], q, _NT, preferred_element_type=F32)
            v, i = _top16(st)
            sv.append(v)
            si.append(i)
        sub = 8
        rows8 = _iota((sub, tt), 0)
        cands = [sv[0][0:1, :] + sv[1]]
        for a in range(1, sub):
            cands.append(jnp.where(rows8 < k // (a + 1), sv[0][a:a + 1, :] + sv[1][:sub, :], NEG_INF))
        cands.append(sv[0][sub:, :] + sv[1][0:1, :])
        cand = jnp.concatenate(cands, axis=0)
        nrows = cand.shape[0]
        pio = _iota(cand.shape, 0).astype(F32)
        tv, tp = [], []
        for _ in range(k):
            m = jnp.max(cand, axis=0, keepdims=True)
            pos = jnp.min(jnp.where(cand == m, pio, float(nrows)), axis=0, keepdims=True)
            cand = jnp.where(pio == pos, NEG_INF, cand)
            tv.append(m)
            tp.append(pos)
        top = jnp.concatenate(tv, axis=0)
        e = jnp.exp(top - top[0:1, :])
        g_ref[:, pl.ds(r0, tt)] = e / jnp.sum(e, axis=0, keepdims=True)
        pos = jnp.concatenate(tp, axis=0).astype(I32)
        mid = pos - k
        first, last = pos < k, pos >= nrows - (k - sub)
        a_sel = jnp.where(first, 0, jnp.where(last, pos - (nrows - k), 1 + (mid >> 3)))
        b_sel = jnp.where(first, pos, jnp.where(last, 0, mid & (sub - 1)))
        i_ref[:, pl.ds(r0, tt)] = _gather16(si[0], a_sel)
        j_ref[:, pl.ds(r0, tt)] = _gather16(si[1], b_sel)
        return carry

    lax.fori_loop(0, pq_ref.shape[0] // tt, sub_tile, 0)


def _route(pq, subkeys, tt):
    n = pq.shape[0]
    hp, keys, half = subkeys.shape
    heads = hp // 2
    k = PEER_TOPK
    out = pl.BlockSpec((k, tt), lambda i, h: (h, i))
    return pl.pallas_call(
        _route_kernel, grid=(n // tt, heads),
        in_specs=[pl.BlockSpec((tt, 2 * half), lambda i, h: (i, h)),
                  pl.BlockSpec((2, keys, half), lambda i, h: (h, 0, 0))],
        out_specs=[out, out, out],
        out_shape=[jax.ShapeDtypeStruct((heads * k, n), I32), jax.ShapeDtypeStruct((heads * k, n), I32),
                   jax.ShapeDtypeStruct((heads * k, n), F32)],
        compiler_params=_params("parallel", "parallel"), name="peer_route",
    )(pq, subkeys)


def _peer_u_kernel(h_ref, ut_ref, i_ref, j_ref, a_ref, *, ib):
    step = pl.program_id(1)

    @pl.when(step == 0)
    def _():
        a_ref[...] = jnp.zeros_like(a_ref)

    isel = i_ref[...]
    jsel = j_ref[...]
    acc = a_ref[...]
    nj = jsel.shape[1]
    h = h_ref[...]
    group = 4
    for g0 in range(0, ib, group):
        a_all = _dot(h, ut_ref[:, g0 * nj:(g0 + group) * nj])
        for ii in range(group):
            a = a_all[:, ii * nj:(ii + 1) * nj]
            acc = jnp.where(isel == step * ib + g0 + ii, jnp.take_along_axis(a, jsel, axis=1), acc)
    a_ref[...] = acc


def _peer_u(h, ut, isel, jsel, tt, ib):
    n, d = h.shape
    slots = isel.shape[1]
    nblk = ut.shape[1] // (slots * ib)
    tok = lambda c: pl.BlockSpec((tt, c), lambda t, e: (t, 0))
    return pl.pallas_call(
        functools.partial(_peer_u_kernel, ib=ib), grid=(n // tt, nblk),
        in_specs=[tok(d), pl.BlockSpec((d, slots * ib), lambda t, e: (0, e)), tok(slots), tok(slots)],
        out_specs=tok(slots), out_shape=jax.ShapeDtypeStruct((n, slots), F32),
        compiler_params=_params("parallel", "arbitrary"), name="peer_u",
    )(h, ut, isel, jsel)


def _peer_v_kernel(i_ref, j_ref, g_ref, a_ref, x_ref, v_ref, o_ref, z_sc, zs_sc, *, ib, stride):
    step = pl.program_id(1)
    tt, slots = i_ref.shape

    @pl.when(step == 0)
    def _():
        a = a_ref[...]
        z_sc[...] = g_ref[...] * (0.5 * a * (1.0 + lax.erf(a * (2.0 ** -0.5))))

        def scatter(t, carry):
            irow = i_ref[pl.ds(t, 1), :]
            jrow = j_ref[pl.ds(t, 1), :]
            zrow = z_sc[pl.ds(t, 1), :]
            io = _iota((slots, slots), 0)
            zit = jnp.where(irow == io, zrow, 0.0).astype(BF16)
            oht = jnp.where(jrow == io, 1.0, 0.0).astype(BF16)
            zs_sc[pl.ds(pl.multiple_of(t * stride, 8), slots), :] = lax.dot_general(
                zit, oht, _NT, preferred_element_type=F32)
            return carry

        lax.fori_loop(0, tt, scatter, 0, unroll=32)
        o_ref[...] = x_ref[...]

    base = step * ib
    zblk = jnp.concatenate(
        [zs_sc[pl.ds(base + ii, tt, stride=stride), :].astype(BF16) for ii in range(ib)], axis=1)
    o_ref[...] += _dot(zblk, v_ref[...])


def _peer_v(isel, jsel, gates, act, x, v, tt, ib):
    n, d = x.shape
    slots = isel.shape[1]
    nblk = v.shape[0] // (slots * ib)
    stride = slots + 8
    tok = lambda c: pl.BlockSpec((tt, c), lambda t, e: (t, 0))
    return pl.pallas_call(
        functools.partial(_peer_v_kernel, ib=ib, stride=stride), grid=(n // tt, nblk),
        in_specs=[tok(slots)] * 4 + [tok(d), pl.BlockSpec((slots * ib, d), lambda t, e: (e, 0))],
        out_specs=tok(d), out_shape=jax.ShapeDtypeStruct((n, d), F32),
        scratch_shapes=[pltpu.VMEM((tt, slots), F32), pltpu.VMEM((tt * stride, slots), F32)],
        compiler_params=_params("parallel", "arbitrary"), name="peer_v",
    )(isel, jsel, gates, act, x, v)


def _final_kernel(x_ref, g_ref, o_ref):
    o_ref[...] = _rms(x_ref[...], g_ref[...])


def _final_norm(x, g, tm):
    n, d = x.shape
    return pl.pallas_call(
        _final_kernel, grid=(n // tm,),
        in_specs=[pl.BlockSpec((tm, d), lambda i: (i, 0)), pl.BlockSpec((1, d), lambda i: (0, 0))],
        out_specs=pl.BlockSpec((tm, d), lambda i: (i, 0)), out_shape=jax.ShapeDtypeStruct((n, d), F32),
        compiler_params=_params("parallel"), name="final_norm",
    )(x, g)


def _tile(n, name):
    pref = TILE[name]
    return pref if n % pref == 0 else n


def _peer(x2, h3, pq, lw):
    n = x2.shape[0]
    it, jt, gt = _route(pq, lw["subkeys"], _tile(n, "route"))
    isel, jsel, gates = it.T, jt.T, gt.T
    act = _peer_u(h3, lw["ut"], isel, jsel, _tile(n, "peer_u"), TILE["peer_blocks"])
    return _peer_v(isel, jsel, gates, act, x2, lw["v"], _tile(n, "peer_v"), TILE["peer_blocks"])


def kernel(x_prompt, x_sample, cache_k, cache_v, cache_logf, state_hg, cache_mem_k, cache_mem_v, page_table, mem_prompt, ln_mix, w_in, b_fox_f, hg_lb, fox_gn, hg_gn, w_out, ln_x, ln_mem, w_xq, w_xk, w_xv, w_xo, ln_ffn, peer_wq, peer_subkeys, peer_u, peer_v, ln_final):
    depth, d_model, _ = w_in.shape
    bp, seq, _ = x_prompt.shape
    bs = x_sample.shape[0]
    fox_heads = b_fox_f.shape[1]
    fox_w = fox_gn.shape[1]
    hg_w = hg_gn.shape[1]
    hg_heads = hg_w // HG_DK
    n_phys, page = cache_k.shape[1], cache_k.shape[2]
    n_mem = mem_prompt.shape[1]
    x_w = w_xq.shape[2]
    x_heads = x_w // X_HEAD_DIM
    p_heads = peer_subkeys.shape[1]
    p_keys, p_half = peer_subkeys.shape[3], peer_subkeys.shape[4]
    pairs = fox_w // LANES

    lb_all = jnp.cumsum(jax.nn.softmax(hg_lb.astype(F32), axis=0), axis=0)
    row = lambda a: a.reshape(1, -1).astype(F32)

    xp = x_prompt.reshape(bp * seq, d_model)
    xs = x_sample.reshape(bs, d_model)
    mem = mem_prompt.reshape(bp * n_mem, d_model)
    ck_t = cache_k.transpose(0, 1, 3, 4, 2).reshape(depth, n_phys, fox_w, page)
    cv_t = cache_v.transpose(0, 1, 3, 4, 2).reshape(depth, n_phys, fox_w, page)
    clf_t = cache_logf.transpose(0, 1, 3, 2)
    state_t = state_hg.transpose(0, 2, 3, 4, 1)
    memk = cache_mem_k.reshape(depth, bs, n_mem * x_heads, X_HEAD_DIM)
    memv = cache_mem_v.reshape(depth, bs, n_mem * x_heads, X_HEAD_DIM)

    outs = {k: [] for k in ("kp", "vp", "fp", "hp", "mkp", "mvp", "ks", "vs", "fs", "hs")}
    for l in range(depth):
        wl = w_in[l]
        c0 = 3 * fox_w
        lw = dict(
            w_main=jnp.concatenate([wl[:, :c0], wl[:, c0 + fox_heads:]], axis=1).astype(BF16),
            w_ff=jnp.pad(wl[:, c0:c0 + fox_heads], ((0, 0), (0, LANES - fox_heads))).astype(BF16),
            b_f=jnp.pad(row(b_fox_f[l]), ((0, 0), (0, LANES - fox_heads))),
            lb=row(lb_all[l] - lb_all[0]),
            subkeys=peer_subkeys[l].reshape(p_heads * 2, p_keys, p_half).astype(BF16),
            ut=peer_u[l].astype(BF16).T, v=peer_v[l].astype(BF16),
        )
        w_out_b, w_xq_b, w_xo_b = w_out[l].astype(BF16), w_xq[l].astype(BF16), w_xo[l].astype(BF16)
        peer_wq_b = peer_wq[l].astype(BF16)

        tm = _tile(bp * seq, "tokenwise")
        fq, fk, fv, flf, hq, hk, hv, hlf, hgate = _inproj(xp, row(ln_mix[l]), lw["w_main"], lw["w_ff"], lw["b_f"],
                                                           lw["lb"], tm, fox_heads)
        lf_t = flf.reshape(bp, seq, fox_heads).transpose(0, 2, 1).reshape(bp * fox_heads, seq)
        c_t = _cumsum_lanes(lf_t, _tile(seq, "cumsum"))
        ck = c_t.reshape(bp * pairs, 2, seq)
        cq = c_t.reshape(bp, pairs, 2, seq).transpose(1, 0, 3, 2).reshape(pairs, bp * seq, 2)
        fo = _fox_prompt(fq, fk, fv, cq, ck, bp, _tile(seq, "fox"))
        ho, st = _hgrn_prompt(hq, hk, hv, hlf, bp)
        x1, qx = _merge(fo, ho, hgate, xp, row(fox_gn[l]), row(hg_gn[l]), w_out_b, row(ln_x[l]), w_xq_b, tm)
        mk, mv = _memkv(mem, row(ln_mem[l]), w_xk[l].astype(BF16), w_xv[l].astype(BF16), n_mem)
        ctx = _xattn_prompt(qx, mk, mv, bp, tm)
        x2, h3, pq = _xo_peerq(x1, ctx, w_xo_b, row(ln_ffn[l]), peer_wq_b, tm)
        xp = _peer(x2, h3, pq, lw)
        outs["kp"].append(fk.reshape(bp, seq, fox_heads, HEAD_DIM))
        outs["vp"].append(fv.reshape(bp, seq, fox_heads, HEAD_DIM))
        outs["fp"].append(flf.reshape(bp, seq, fox_heads))
        st5 = st.reshape(bp, pairs, 2, HG_DK, 2, HG_DK)
        s_heads = jnp.stack([st5[:, :, 0, :, 0, :], st5[:, :, 1, :, 1, :]], axis=2)
        outs["hp"].append(s_heads.reshape(bp, hg_heads, HG_DK, HG_DK).transpose(0, 1, 3, 2))
        outs["mkp"].append(mk.reshape(bp, n_mem, x_heads, X_HEAD_DIM))
        outs["mvp"].append(mv.reshape(bp, n_mem, x_heads, X_HEAD_DIM))

        sq, sk, sv, slf, tq, tk, tv, tlf, tgate = _inproj(xs, row(ln_mix[l]), lw["w_main"], lw["w_ff"], lw["b_f"],
                                                           lw["lb"], bs, fox_heads)
        tok3 = lambda a: a.reshape(bs, 1, a.shape[1])
        colm = lambda a: a.reshape(bs, a.shape[1], 1)
        fo_s = _fox_decode(page_table, colm(sq), colm(sk), colm(sv), colm(slf), ck_t, cv_t, clf_t,
                           l).reshape(bs, fox_w)
        lanes_r = lambda a: a.T.reshape(hg_heads, HG_DK, 1, bs)
        ho_t, s_new = _hgrn_step(lanes_r(tq), lanes_r(tk), lanes_r(tlf), tv.T.reshape(hg_heads, HG_DK, bs),
                                 state_t, l)
        x1s, qxs = _merge(fo_s, ho_t.reshape(hg_w, bs).T, tgate, xs, row(fox_gn[l]), row(hg_gn[l]), w_out_b,
                          row(ln_x[l]), w_xq_b, bs)
        qxs_t = jnp.pad(qxs.reshape(bs, x_heads, X_HEAD_DIM).transpose(0, 2, 1), ((0, 0), (0, 0), (0, 8 - x_heads)))
        ctx_s = _xattn_sample(qxs_t, memk, memv, l, x_heads, _tile(bs, "xattn_sample")).reshape(bs, x_w)
        x2s, h3s, pqs = _xo_peerq(x1s, ctx_s, w_xo_b, row(ln_ffn[l]), peer_wq_b, bs)
        xs = _peer(x2s, h3s, pqs, lw)
        outs["ks"].append(sk.reshape(bs, 1, fox_heads, HEAD_DIM))
        outs["vs"].append(sv.reshape(bs, 1, fox_heads, HEAD_DIM))
        outs["fs"].append(slf.reshape(bs, 1, fox_heads))
        outs["hs"].append(s_new)

    y_prompt = _final_norm(xp, row(ln_final), _tile(bp * seq, "tokenwise")).reshape(bp, seq, d_model)
    y_sample = _final_norm(xs, row(ln_final), bs).reshape(bs, 1, d_model)
    st_ = lambda k: jnp.stack(outs[k])
    return (y_prompt, y_sample, st_("kp"), st_("vp"), st_("fp"), st_("hp"), st_("mkp"), st_("mvp"),
            st_("ks"), st_("vs"), st_("fs"), st_("hs").transpose(0, 4, 1, 2, 3))
```

```python
import functools
import math

import jax
import jax.numpy as jnp
from jax import lax
from jax.experimental import pallas as pl
from jax.experimental.pallas import tpu as pltpu

F32 = jnp.float32
BF16 = jnp.bfloat16
I32 = jnp.int32
EPS = 1e-6
NEG_INF = float("-inf")

HEAD_DIM = 64
HG_DK = 64
HG_CHUNK = 64
HG_SUB = 16
X_HEAD_DIM = 128
PEER_TOPK = 16
LANES = 128
V7X_VMEM_BYTES = 64 * 1024 * 1024
VMEM_LIMIT = V7X_VMEM_BYTES - 8 * 1024 * 1024

TILE = dict(
    tokenwise=512,
    cumsum=512,
    fox=1024,
    route=1024,
    peer_groups=8,
    peer_u=1024,
    peer_v=512,
    peer_blocks=16,
    xattn_sample=8,
)

_NT = (((1,), (1,)), ((), ()))
_TN = (((0,), (0,)), ((), ()))


def _params(*sem):
    return pltpu.CompilerParams(dimension_semantics=sem, vmem_limit_bytes=VMEM_LIMIT)


def _rms(x, g):
    ms = jnp.mean(x * x, axis=-1, keepdims=True)
    return x * lax.rsqrt(ms + EPS) * g


def _log_sigmoid(x):
    return jnp.minimum(x, 0.0) - jnp.log1p(jnp.exp(-jnp.abs(x)))


def _dot(a, b):
    return jnp.dot(a, b, preferred_element_type=F32)


def _split3(x):
    hi = x.astype(BF16)
    r = x - hi.astype(F32)
    mid = r.astype(BF16)
    lo = (r - mid.astype(F32)).astype(BF16)
    return hi, mid, lo


def _prefix_rows(tril, x):
    return _dot(jnp.concatenate([tril] * 3, axis=1), jnp.concatenate(_split3(x), axis=0))


def _prefix_lanes(x, triu):
    return _dot(jnp.concatenate(_split3(x), axis=1), jnp.concatenate([triu] * 3, axis=0))


def _iota(shape, axis):
    return lax.broadcasted_iota(I32, shape, axis)


def _inproj_kernel(x_ref, g_ref, w_ref, wff_ref, bf_ref, lb_ref,
                   fq_ref, fk_ref, fv_ref, lf_ref, hq_ref, hk_ref, hv_ref, hlf_ref, gate_ref, *, fox_heads):
    h = _rms(x_ref[...], g_ref[...]).astype(BF16)
    w = w_ref.shape[1] // 7

    def mm(j):
        return _dot(h, w_ref[:, j * w:(j + 1) * w])

    fq_ref[...] = mm(0)
    fk_ref[...] = mm(1)
    fv_ref[...] = mm(2)
    ff = _dot(h, wff_ref[...])
    lf_ref[...] = _log_sigmoid(ff + bf_ref[...])[:, :fox_heads]
    hq_ref[...] = mm(3)
    z = mm(4)
    lb = lb_ref[...]
    a = jnp.log(lb)
    b = jnp.log1p(-lb) + _log_sigmoid(z)
    hlf_ref[...] = jnp.maximum(a, b) + jnp.log1p(jnp.exp(-jnp.abs(a - b)))
    hk_ref[...] = (1.0 - lb) * (1.0 / (1.0 + jnp.exp(z)))
    hv_ref[...] = mm(5)
    gate_ref[...] = mm(6)


def _inproj(x, g, w_main, w_ff, b_f, lb, tm, fox_heads):
    n, d = x.shape
    w = w_main.shape[1] // 7
    full = lambda a: pl.BlockSpec(a.shape, lambda i: (0,) * a.ndim)
    row = lambda c: pl.BlockSpec((tm, c), lambda i: (i, 0))
    outs = [jax.ShapeDtypeStruct((n, w), F32)] * 3 + [jax.ShapeDtypeStruct((n, fox_heads), F32)] + \
           [jax.ShapeDtypeStruct((n, w), F32)] * 5
    out_specs = [row(w)] * 3 + [row(fox_heads)] + [row(w)] * 5
    return pl.pallas_call(
        functools.partial(_inproj_kernel, fox_heads=fox_heads),
        grid=(n // tm,),
        in_specs=[row(d), full(g), full(w_main), full(w_ff), full(b_f), full(lb)],
        out_specs=out_specs, out_shape=outs,
        compiler_params=_params("parallel"), name="inproj",
    )(x, g, w_main, w_ff, b_f, lb)


def _cumsum_kernel(x_ref, o_ref, carry_ref):
    @pl.when(pl.program_id(0) == 0)
    def _():
        carry_ref[...] = jnp.zeros_like(carry_ref)

    x = x_ref[...]
    n = x.shape[1]
    tri = (_iota((n, n), 0) <= _iota((n, n), 1)).astype(BF16)
    c = _prefix_lanes(x, tri) + carry_ref[:, :1]
    o_ref[...] = c
    carry_ref[...] = jnp.broadcast_to(c[:, n - 1:n], carry_ref.shape)


def _cumsum_lanes(x, tc):
    r, t = x.shape
    return pl.pallas_call(
        _cumsum_kernel, grid=(t // tc,),
        in_specs=[pl.BlockSpec((r, tc), lambda i: (0, i))],
        out_specs=pl.BlockSpec((r, tc), lambda i: (0, i)),
        out_shape=jax.ShapeDtypeStruct((r, t), F32),
        scratch_shapes=[pltpu.VMEM((r, LANES), F32)],
        compiler_params=_params("arbitrary"), name="logf_cumsum",
    )(x)


def _fox_kernel(qt_ref, kt_ref, q_ref, k_ref, v_ref, cq_ref, ck_ref, o_ref, m_ref, cqr_ref, acc_ref, qh_ref, *,
                scale):
    qi = qt_ref[pl.program_id(2)]
    ki = kt_ref[pl.program_id(2)]
    tq = q_ref.shape[0]
    tk = k_ref.shape[0]
    lane = _iota((1, LANES), 1)
    log2e = math.log2(math.e)

    @pl.when(ki == 0)
    def _():
        m_ref[...] = jnp.full_like(m_ref, NEG_INF)
        acc_ref[...] = jnp.zeros_like(acc_ref)
        q = q_ref[...] * (scale * log2e)
        for hh in range(2):
            cqr_ref[hh] = jnp.broadcast_to(cq_ref[:, hh:hh + 1] * log2e, (tq, LANES))
            qh_ref[hh] = jnp.where((lane // HEAD_DIM) == hh, q, 0.0).astype(BF16)

    def step(masked, blocks):
        k = k_ref[...].astype(BF16)
        v = v_ref[...]
        ck = ck_ref[...] * log2e
        for hh in range(2):
            vh = jnp.where((lane // HEAD_DIM) == hh, v, 1.0).astype(BF16)
            for r0, r1, nk in blocks:
                s = lax.dot_general(qh_ref[hh, r0:r1], k[:nk], _NT, preferred_element_type=F32) - ck[hh:hh + 1, :nk]
                if masked:
                    causal = (ki * tk + _iota((1, nk), 1)) <= (qi * tq + r0 + _iota((r1 - r0, 1), 0))
                    s = jnp.where(causal, s, NEG_INF)
                m_prev = m_ref[hh, r0:r1]
                cq = cqr_ref[hh, r0:r1]
                m_new = jnp.maximum(m_prev, jnp.max(s, axis=1, keepdims=True) + cq)
                p = jnp.exp2(s - jnp.tile(m_new - cq, (1, nk // LANES)))
                acc_ref[hh, r0:r1] = jnp.exp2(m_prev - m_new) * acc_ref[hh, r0:r1] + _dot(p.astype(BF16), vh[:nk])
                m_ref[hh, r0:r1] = m_new

    @pl.when(ki < qi)
    def _():
        step(False, [(0, tq, tk)])

    @pl.when(ki == qi)
    def _():
        half = tq // 2
        step(True, [(0, half, tk // 2), (half, tq, tk)] if half % LANES == 0 else [(0, tq, tk)])
        a0 = acc_ref[0]
        a1 = acc_ref[1]
        o0 = a0 / pltpu.roll(a0, HEAD_DIM, axis=1)
        o1 = a1 / pltpu.roll(a1, HEAD_DIM, axis=1)
        o_ref[...] = jnp.where(lane < HEAD_DIM, o0, o1)


def _fox_prompt(q, k, v, cq, ck, batch, tq):
    n, w = q.shape
    t = n // batch
    pairs = w // LANES
    nq = t // tq
    kern = functools.partial(_fox_kernel, scale=HEAD_DIM ** -0.5)
    steps = [(i, j) for i in range(nq) for j in range(i + 1)]
    qi_tab = jnp.asarray([s[0] for s in steps], I32)
    ki_tab = jnp.asarray([s[1] for s in steps], I32)
    grid_spec = pltpu.PrefetchScalarGridSpec(
        num_scalar_prefetch=2, grid=(batch, pairs, len(steps)),
        in_specs=[
            pl.BlockSpec((tq, LANES), lambda b, g, s, qt, kt: (b * nq + qt[s], g)),
            pl.BlockSpec((tq, LANES), lambda b, g, s, qt, kt: (b * nq + kt[s], g)),
            pl.BlockSpec((tq, LANES), lambda b, g, s, qt, kt: (b * nq + kt[s], g)),
            pl.BlockSpec((None, tq, 2), lambda b, g, s, qt, kt: (g, b * nq + qt[s], 0)),
            pl.BlockSpec((None, 2, tq), lambda b, g, s, qt, kt: (b * pairs + g, 0, kt[s])),
        ],
        out_specs=pl.BlockSpec((tq, LANES), lambda b, g, s, qt, kt: (b * nq + qt[s], g)),
        scratch_shapes=[pltpu.VMEM((2, tq, LANES), F32)] * 3 + [pltpu.VMEM((2, tq, LANES), BF16)],
    )
    return pl.pallas_call(
        kern, grid_spec=grid_spec, out_shape=jax.ShapeDtypeStruct((n, w), F32),
        compiler_params=_params("parallel", "parallel", "arbitrary"), name="fox_prompt",
    )(qi_tab, ki_tab, q, k, v, cq, ck)


def _hgrn_kernel(q_ref, k_ref, v_ref, lf_ref, o_ref, st_ref, st_sc):
    n = pl.program_id(0)
    nb, c, w = q_ref.shape
    pairs = w // LANES

    @pl.when(n == 0)
    def _():
        st_sc[...] = jnp.zeros_like(st_sc)

    tril = (_iota((c, c), 1) <= _iota((c, c), 0)).astype(BF16)
    same_head = (_iota((LANES, LANES), 0) // HG_DK) == (_iota((LANES, LANES), 1) // HG_DK)
    seg = same_head.astype(BF16)
    sc = HG_SUB
    ti = _iota((sc, sc, LANES), 0)
    si = _iota((sc, sc, LANES), 1)
    first_head = _iota((1, LANES), 1) < HG_DK
    for bg in range(nb * pairs):
        r, g = divmod(bg, pairs)
        sl = slice(g * LANES, (g + 1) * LANES)
        q = q_ref[r, :, sl]
        k = k_ref[r, :, sl]
        v = v_ref[r, :, sl]
        vb = v.astype(BF16)
        b = _prefix_rows(tril, lf_ref[r, :, sl])
        st = st_sc[r, g]
        qe = (q * jnp.exp(b)).astype(BF16)
        o = lax.dot_general(qe, st.astype(BF16), _NT, preferred_element_type=F32)
        rows = []
        for i in range(c // sc):
            lo, hi = i * sc, (i + 1) * sc
            qi, ki, bi = q[lo:hi], k[lo:hi], b[lo:hi]
            decay = jnp.exp(jnp.where(si <= ti, bi[:, None, :] - bi[None, :, :], NEG_INF))
            p = (qi[:, None, :] * ki[None, :, :]) * decay
            a = _dot(p.reshape(sc * sc, LANES).astype(BF16), seg)
            oi = jnp.sum(a.reshape(sc, sc, LANES) * v[None, lo:hi, :], axis=1)
            if i > 0:
                br = b[lo - 1:lo, :]
                qt = qi * jnp.exp(bi - br)
                q2 = jnp.concatenate([jnp.where(first_head, qt, 0.0), jnp.where(first_head, 0.0, qt)], axis=0)
                ks = (k[:lo] * jnp.exp(br - b[:lo])).astype(BF16)
                a2 = lax.dot_general(q2.astype(BF16), ks, _NT, preferred_element_type=F32)
                o2 = _dot(a2.astype(BF16), vb[:lo])
                oi = oi + jnp.where(first_head, o2[:sc], o2[sc:])
            rows.append(oi)
        o_ref[r, :, sl] = o + jnp.concatenate(rows, axis=0)
        b_last = b[c - 1:c, :]
        kd = (k * jnp.exp(b_last - b)).astype(BF16)
        upd = lax.dot_general(vb, kd, _TN, preferred_element_type=F32)
        st_sc[r, g] = st * jnp.exp(b_last) + jnp.where(same_head, upd, 0.0)

    @pl.when(n == pl.num_programs(0) - 1)
    def _():
        st_ref[...] = st_sc[...]


def _hgrn_prompt(q, k, v, lf, batch):
    n, w = q.shape
    t = n // batch
    pairs = w // LANES
    rows3 = lambda a: a.reshape(batch, t, w)
    blk = pl.BlockSpec((batch, HG_CHUNK, w), lambda i: (0, i, 0))
    st_shape = (batch, pairs, LANES, LANES)
    o, st = pl.pallas_call(
        _hgrn_kernel, grid=(t // HG_CHUNK,),
        in_specs=[blk] * 4,
        out_specs=[blk, pl.BlockSpec(st_shape, lambda i: (0, 0, 0, 0))],
        out_shape=[jax.ShapeDtypeStruct((batch, t, w), F32), jax.ShapeDtypeStruct(st_shape, F32)],
        scratch_shapes=[pltpu.VMEM(st_shape, F32)],
        compiler_params=_params("arbitrary"), name="hgrn_prompt",
    )(rows3(q), rows3(k), rows3(v), rows3(lf))
    return o.reshape(n, w), st


def _fox_decode_kernel(pt_ref, q_ref, kn_ref, vn_ref, lfn_ref, *refs, scale, n_pages):
    k_refs = refs[:n_pages]
    v_refs = refs[n_pages:2 * n_pages]
    lf_refs = refs[2 * n_pages:3 * n_pages]
    o_ref = refs[3 * n_pages]
    w, ps = k_refs[0].shape
    nh = lf_refs[0].shape[0]
    hd = w // nh
    heads = lambda x: x.reshape(nh, hd, x.shape[1])
    q = q_ref[...] * scale
    qb = jnp.broadcast_to(q, (w, ps))
    triu = (_iota((ps, ps), 0) <= _iota((ps, ps), 1)).astype(BF16)

    carry = jnp.zeros((nh, 1), F32)
    scores = []
    for p in range(n_pages):
        s = jnp.sum(heads(k_refs[p][...] * qb), axis=1)
        within = _prefix_lanes(lf_refs[p][...], triu)
        scores.append(s - (within + carry))
        carry = carry + within[:, ps - 1:ps]
    s_new = jnp.sum(heads(kn_ref[...] * q), axis=1) - (carry + lfn_ref[...])

    m_tile = scores[0]
    for s in scores[1:]:
        m_tile = jnp.maximum(m_tile, s)
    m = jnp.maximum(s_new, jnp.max(m_tile, axis=1, keepdims=True))
    e_new = jnp.exp(s_new - m)
    exps = [jnp.exp(s - m) for s in scores]
    l_tile = exps[0]
    for e in exps[1:]:
        l_tile = l_tile + e
    inv = 1.0 / (e_new + jnp.sum(l_tile, axis=1, keepdims=True))
    acc = jnp.zeros((w, ps), F32)
    for p in range(n_pages):
        pb = jnp.broadcast_to(exps[p][:, None, :], (nh, hd, ps)).reshape(w, ps)
        acc = acc + v_refs[p][...] * pb
    rep = lambda x: jnp.broadcast_to(x[:, None, :], (nh, hd, 1)).reshape(w, 1)
    o_ref[...] = (jnp.sum(acc, axis=1, keepdims=True) + rep(e_new) * vn_ref[...]) * rep(inv)


def _fox_decode(page_table, q, k_new, v_new, lf_new, cache_kt, cache_vt, cache_lft, layer):
    r, w, _ = q.shape
    nh = lf_new.shape[1]
    n_pages = page_table.shape[1]
    ps = cache_kt.shape[3]
    pt = page_table.reshape(-1)
    tok = lambda c: pl.BlockSpec((None, c, 1), lambda i, pt: (i, 0, 0))

    def page(p, rows):
        return pl.BlockSpec((None, None, rows, ps), lambda i, pt: (layer, pt[i * n_pages + p], 0, 0))

    pages = lambda rows: [page(p, rows) for p in range(n_pages)]
    grid_spec = pltpu.PrefetchScalarGridSpec(
        num_scalar_prefetch=1, grid=(r,),
        in_specs=[tok(w), tok(w), tok(w), tok(nh)] + pages(w) + pages(w) + pages(nh),
        out_specs=tok(w),
    )
    return pl.pallas_call(
        functools.partial(_fox_decode_kernel, scale=HEAD_DIM ** -0.5, n_pages=n_pages),
        grid_spec=grid_spec, out_shape=jax.ShapeDtypeStruct((r, w, 1), F32),
        compiler_params=_params("parallel"), name="fox_decode",
    )(pt, q, k_new, v_new, lf_new, *([cache_kt] * n_pages), *([cache_vt] * n_pages), *([cache_lft] * n_pages))


def _hgrn_step_kernel(q_ref, k_ref, lf_ref, v_ref, s_ref, o_ref, so_ref):
    q = q_ref[...]
    k = k_ref[...]
    f = jnp.exp(lf_ref[...])
    v = v_ref[...]
    s = s_ref[...]
    so_ref[...] = f * s + k * v[None]
    o_ref[...] = jnp.sum((q * f) * s, axis=0) + jnp.sum(q * k, axis=0) * v


def _hgrn_step(q, k, lf, v, state_t, layer):
    h, dk, _, r = q.shape
    dv = v.shape[1]
    col = pl.BlockSpec((None, dk, 1, r), lambda i: (i, 0, 0, 0))
    rowv = pl.BlockSpec((None, dv, r), lambda i: (i, 0, 0))
    return pl.pallas_call(
        _hgrn_step_kernel, grid=(h,),
        in_specs=[col, col, col, rowv, pl.BlockSpec((None, None, dk, dv, r), lambda i: (layer, i, 0, 0, 0))],
        out_specs=[rowv, pl.BlockSpec((None, dk, dv, r), lambda i: (i, 0, 0, 0))],
        out_shape=[jax.ShapeDtypeStruct((h, dv, r), F32), jax.ShapeDtypeStruct((h, dk, dv, r), F32)],
        compiler_params=_params("parallel"), name="hgrn_step",
    )(q, k, lf, v, state_t)


def _head_mean_sq(y, width):
    w = y.shape[1]
    seg = ((_iota((w, w), 0) // width) == (_iota((w, w), 1) // width)).astype(BF16)
    sq = y * y
    hi = sq.astype(BF16)
    lo = (sq - hi.astype(F32)).astype(BF16)
    return (_dot(hi, seg) + _dot(lo, seg)) * (1.0 / width)


def _merge_kernel(fo_ref, ho_ref, gate_ref, x_ref, fgn_ref, hgn_ref, wo_ref, lnx_ref, wxq_ref, x1_ref, qx_ref):
    fo = fo_ref[...]
    ho = ho_ref[...]
    fw = fo.shape[1]
    fn = fo * lax.rsqrt(_head_mean_sq(fo, HEAD_DIM) + EPS) * fgn_ref[...]
    gate = gate_ref[...]
    hn = ho * lax.rsqrt(_head_mean_sq(ho, HEAD_DIM) + EPS) * hgn_ref[...] * (gate / (1.0 + jnp.exp(-gate)))
    y = _dot(fn.astype(BF16), wo_ref[:fw, :]) + _dot(hn.astype(BF16), wo_ref[fw:, :])
    x1 = x_ref[...] + y
    x1_ref[...] = x1
    qx_ref[...] = _dot(_rms(x1, lnx_ref[...]).astype(BF16), wxq_ref[...])


def _merge(fo, ho, gate, x, fgn, hgn, w_out, ln_x, w_xq, tm):
    n, d = x.shape
    xw = w_xq.shape[1]
    full = lambda a: pl.BlockSpec(a.shape, lambda i: (0,) * a.ndim)
    row = lambda c: pl.BlockSpec((tm, c), lambda i: (i, 0))
    return pl.pallas_call(
        _merge_kernel, grid=(n // tm,),
        in_specs=[row(fo.shape[1]), row(ho.shape[1]), row(gate.shape[1]), row(d),
                  full(fgn), full(hgn), full(w_out), full(ln_x), full(w_xq)],
        out_specs=[row(d), row(xw)],
        out_shape=[jax.ShapeDtypeStruct((n, d), F32), jax.ShapeDtypeStruct((n, xw), F32)],
        compiler_params=_params("parallel"), name="merge",
    )(fo, ho, gate, x, fgn, hgn, w_out, ln_x, w_xq)


def _memkv_kernel(m_ref, g_ref, wk_ref, wv_ref, k_ref, v_ref):
    m = _rms(m_ref[...], g_ref[...]).astype(BF16)
    k_ref[...] = _dot(m, wk_ref[...])
    v_ref[...] = _dot(m, wv_ref[...])


def _memkv(mem, g, wk, wv, tm):
    n, d = mem.shape
    xw = wk.shape[1]
    full = lambda a: pl.BlockSpec(a.shape, lambda i: (0,) * a.ndim)
    row = lambda c: pl.BlockSpec((tm, c), lambda i: (i, 0))
    return pl.pallas_call(
        _memkv_kernel, grid=(n // tm,),
        in_specs=[row(d), full(g), full(wk), full(wv)],
        out_specs=[row(xw), row(xw)],
        out_shape=[jax.ShapeDtypeStruct((n, xw), F32)] * 2,
        compiler_params=_params("parallel"), name="memkv",
    )(mem, g, wk, wv)


def _xattn_prompt_kernel(q_ref, mk_ref, mv_ref, o_ref, *, scale):
    heads = q_ref.shape[1] // X_HEAD_DIM
    for h in range(heads):
        sl = slice(h * X_HEAD_DIM, (h + 1) * X_HEAD_DIM)
        q = (q_ref[:, sl] * scale).astype(BF16)
        s = lax.dot_general(q, mk_ref[:, sl].astype(BF16), _NT, preferred_element_type=F32)
        e = jnp.exp(s - jnp.max(s, axis=1, keepdims=True))
        p = e / jnp.sum(e, axis=1, keepdims=True)
        o_ref[:, sl] = _dot(p.astype(BF16), mv_ref[:, sl].astype(BF16))


def _xattn_prompt(q, mk, mv, batch, tm):
    n, xw = q.shape
    t = n // batch
    nm = mk.shape[0] // batch
    nt = t // tm
    return pl.pallas_call(
        functools.partial(_xattn_prompt_kernel, scale=X_HEAD_DIM ** -0.5), grid=(batch, nt),
        in_specs=[pl.BlockSpec((tm, xw), lambda b, i: (b * nt + i, 0)),
                  pl.BlockSpec((nm, xw), lambda b, i: (b, 0)),
                  pl.BlockSpec((nm, xw), lambda b, i: (b, 0))],
        out_specs=pl.BlockSpec((tm, xw), lambda b, i: (b * nt + i, 0)),
        out_shape=jax.ShapeDtypeStruct((n, xw), F32),
        compiler_params=_params("parallel", "parallel"), name="xattn_prompt",
    )(q, mk, mv)


def _xattn_sample_kernel(qt_ref, mk_ref, mv_ref, o_ref, *, scale, heads):
    rb, hd, cols = qt_ref.shape
    nm = mk_ref.shape[1] // heads
    col = _iota((hd, cols), 1)
    erow = _iota((cols, hd), 0)
    valid = _iota((1, cols), 1) < heads
    for r in range(rb):
        qt = qt_ref[r] * scale
        s = jnp.zeros((nm, cols), F32)
        for h in range(heads):
            kh = mk_ref[r, pl.ds(h, nm, stride=heads), :].astype(BF16)
            s = s + _dot(kh, jnp.where(col == h, qt, 0.0).astype(BF16))
        e = jnp.exp(s - jnp.max(s, axis=0, keepdims=True))
        p = jnp.where(valid, e / jnp.sum(e, axis=0, keepdims=True), 0.0).astype(BF16)
        outs = []
        for h in range(heads):
            pe = _dot(p, (erow == h).astype(BF16))
            vh = mv_ref[r, pl.ds(h, nm, stride=heads), :]
            outs.append(jnp.sum((pe * vh).reshape(nm // 8, 8, hd), axis=0))
        o_ref[r] = jnp.sum(jnp.concatenate(outs, axis=1), axis=0, keepdims=True)


def _xattn_sample(q_t, mem_k, mem_v, layer, heads, rb):
    r, hd, cols = q_t.shape
    rows = mem_k.shape[2]
    mem = pl.BlockSpec((None, rb, rows, hd), lambda i: (layer, i, 0, 0))
    return pl.pallas_call(
        functools.partial(_xattn_sample_kernel, scale=X_HEAD_DIM ** -0.5, heads=heads), grid=(r // rb,),
        in_specs=[pl.BlockSpec((rb, hd, cols), lambda i: (i, 0, 0)), mem, mem],
        out_specs=pl.BlockSpec((rb, 1, heads * hd), lambda i: (i, 0, 0)),
        out_shape=jax.ShapeDtypeStruct((r, 1, heads * hd), F32),
        compiler_params=_params("parallel"), name="xattn_sample",
    )(q_t, mem_k, mem_v)


def _xo_peerq_kernel(x1_ref, ctx_ref, wxo_ref, lnf_ref, wq_ref, x2_ref, h_ref, pq_ref):
    x2 = x1_ref[...] + _dot(ctx_ref[...].astype(BF16), wxo_ref[...])
    x2_ref[...] = x2
    h = _rms(x2, lnf_ref[...]).astype(BF16)
    h_ref[...] = h
    pq_ref[...] = _dot(h, wq_ref[...]).astype(BF16)


def _xo_peerq(x1, ctx, w_xo, ln_ffn, peer_wq, tm):
    n, d = x1.shape
    qw = peer_wq.shape[1]
    full = lambda a: pl.BlockSpec(a.shape, lambda i: (0,) * a.ndim)
    row = lambda c: pl.BlockSpec((tm, c), lambda i: (i, 0))
    return pl.pallas_call(
        _xo_peerq_kernel, grid=(n // tm,),
        in_specs=[row(d), row(ctx.shape[1]), full(w_xo), full(ln_ffn), full(peer_wq)],
        out_specs=[row(d), row(d), row(qw)],
        out_shape=[jax.ShapeDtypeStruct((n, d), F32), jax.ShapeDtypeStruct((n, d), BF16),
                   jax.ShapeDtypeStruct((n, qw), BF16)],
        compiler_params=_params("parallel"), name="xo_peerq",
    )(x1, ctx, w_xo, ln_ffn, peer_wq)


def _top16(s):
    kk = s.shape[0]
    kio = _iota(s.shape, 0).astype(F32)
    vals, idxs = [], []
    for _ in range(PEER_TOPK):
        m = jnp.max(s, axis=0, keepdims=True)
        idx = jnp.min(jnp.where(s == m, kio, float(kk)), axis=0, keepdims=True)
        s = jnp.where(kio == idx, NEG_INF, s)
        vals.append(m)
        idxs.append(idx)
    return jnp.concatenate(vals, axis=0), jnp.concatenate(idxs, axis=0).astype(I32)


def _gather16(x, idx):
    lo, hi = x[:8], x[8:]
    out = []
    for part in (idx[:8], idx[8:]):
        low3 = part & 7
        out.append(jnp.where(part < 8, jnp.take_along_axis(lo, low3, axis=0), jnp.take_along_axis(hi, low3, axis=0)))
    return jnp.concatenate(out, axis=0)


def _route_kernel(pq_ref, sk_ref, i_ref, j_ref, g_ref, *, unrolled=False):
    half = sk_ref.shape[2]
    tt = min(2 * LANES, pq_ref.shape[0])
    k = PEER_TOPK

    def sub_tile(n, carry):
        r0 = pl.multiple_of(n * tt, tt)
        sv, si = [], []
        for p in range(2):
            q = pq_ref[pl.ds(r0, tt), p * half:(p + 1) * half]
            st = lax.dot_general(sk_ref[p], q, _NT, preferred_element_type=F32)
            v, i = _top16(st)
            sv.append(v)
            si.append(i)
        sub = 8
        rows8 = _iota((sub, tt), 0)
        cands = [sv[0][0:1, :] + sv[1]]
        for a in range(1, sub):
            cands.append(jnp.where(rows8 < k // (a + 1), sv[0][a:a + 1, :] + sv[1][:sub, :], NEG_INF))
        cands.append(sv[0][sub:, :] + sv[1][0:1, :])
        cand = jnp.concatenate(cands, axis=0)
        nrows = cand.shape[0]
        pio = _iota(cand.shape, 0).astype(F32)
        tv, tp = [], []
        for _ in range(k):
            m = jnp.max(cand, axis=0, keepdims=True)
            pos = jnp.min(jnp.where(cand == m, pio, float(nrows)), axis=0, keepdims=True)
            cand = jnp.where(pio == pos, NEG_INF, cand)
            tv.append(m)
            tp.append(pos)
        top = jnp.concatenate(tv, axis=0)
        e = jnp.exp(top - top[0:1, :])
        g_ref[:, pl.ds(r0, tt)] = e / jnp.sum(e, axis=0, keepdims=True)
        pos = jnp.concatenate(tp, axis=0).astype(I32)
        mid = pos - k
        first, last = pos < k, pos >= nrows - (k - sub)
        a_sel = jnp.where(first, 0, jnp.where(last, pos - (nrows - k), 1 + (mid >> 3)))
        b_sel = jnp.where(first, pos, jnp.where(last, 0, mid & (sub - 1)))
        i_ref[:, pl.ds(r0, tt)] = _gather16(si[0], a_sel)
        j_ref[:, pl.ds(r0, tt)] = _gather16(si[1], b_sel)
        return carry

    if unrolled:
        return [functools.partial(sub_tile, n, 0) for n in range(pq_ref.shape[0] // tt)]
    lax.fori_loop(0, pq_ref.shape[0] // tt, sub_tile, 0)


def _route(pq, subkeys, tt, first=0, tiles=None):
    hp, keys, half = subkeys.shape
    heads = hp // 2
    k = PEER_TOPK
    n = (pq.shape[0] // tt if tiles is None else tiles) * tt
    out = pl.BlockSpec((k, tt), lambda i, h: (h, i))
    return pl.pallas_call(
        _route_kernel, grid=(n // tt, heads),
        in_specs=[pl.BlockSpec((tt, 2 * half), lambda i, h: (first + i, h)),
                  pl.BlockSpec((2, keys, half), lambda i, h: (h, 0, 0))],
        out_specs=[out, out, out],
        out_shape=[jax.ShapeDtypeStruct((heads * k, n), I32), jax.ShapeDtypeStruct((heads * k, n), I32),
                   jax.ShapeDtypeStruct((heads * k, n), F32)],
        compiler_params=_params("parallel", "parallel"), name="peer_route",
    )(pq, subkeys)


def _peer_u_groups(h_ref, ut_ref, i_ref, j_ref, a_ref, ib, group=4):
    step = pl.program_id(1)

    @pl.when(step == 0)
    def _():
        a_ref[...] = jnp.zeros_like(a_ref)

    nj = j_ref.shape[1]

    def run(g0):
        isel = i_ref[...]
        jsel = j_ref[...]
        acc = a_ref[...]
        a_all = _dot(h_ref[...], ut_ref[:, g0 * nj:(g0 + group) * nj])
        for ii in range(group):
            a = a_all[:, ii * nj:(ii + 1) * nj]
            acc = jnp.where(isel == step * ib + g0 + ii, jnp.take_along_axis(a, jsel, axis=1), acc)
        a_ref[...] = acc

    return [functools.partial(run, g0) for g0 in range(0, ib, group)]


def _peer_u_kernel(h_ref, ut_ref, i_ref, j_ref, a_ref, *, ib):
    for run in _peer_u_groups(h_ref, ut_ref, i_ref, j_ref, a_ref, ib):
        run()


def _peer_u(h, ut, isel, jsel, tt, ib, first=0):
    d = h.shape[1]
    n, slots = isel.shape
    nblk = ut.shape[1] // (slots * ib)
    tok = lambda c: pl.BlockSpec((tt, c), lambda t, e: (t, 0))
    return pl.pallas_call(
        functools.partial(_peer_u_kernel, ib=ib), grid=(n // tt, nblk),
        in_specs=[pl.BlockSpec((tt, d), lambda t, e: (first + t, 0)),
                  pl.BlockSpec((d, slots * ib), lambda t, e: (0, e)), tok(slots), tok(slots)],
        out_specs=tok(slots), out_shape=jax.ShapeDtypeStruct((n, slots), F32),
        compiler_params=_params("parallel", "arbitrary"), name="peer_u",
    )(h, ut, isel, jsel)


def _route_peer_u_kernel(pq_ref, sk_ref, h_ref, ut_ref, isel_ref, jsel_ref, i_ref, j_ref, g_ref, a_ref, *, ib):
    matmuls = _peer_u_groups(h_ref, ut_ref, isel_ref, jsel_ref, a_ref, ib)
    tiles = _route_kernel(pq_ref, sk_ref, i_ref, j_ref, g_ref, unrolled=True)
    for n in range(max(len(matmuls), len(tiles))):
        if n < len(matmuls):
            matmuls[n]()
        if n < len(tiles):
            tiles[n]()


def _route_peer_u(pq, subkeys, h, ut, isel, jsel, tt, ib, route_first, act_first):
    d = h.shape[1]
    n, slots = isel.shape
    hp, keys, half = subkeys.shape
    heads = hp // 2
    assert ut.shape[1] == heads * slots * ib
    k = PEER_TOPK
    tok = lambda c: pl.BlockSpec((tt, c), lambda t, e: (t, 0))
    rout = pl.BlockSpec((k, tt), lambda t, e: (e, t))
    return pl.pallas_call(
        functools.partial(_route_peer_u_kernel, ib=ib), grid=(n // tt, heads),
        in_specs=[pl.BlockSpec((tt, 2 * half), lambda t, e: (route_first + t, e)),
                  pl.BlockSpec((2, keys, half), lambda t, e: (e, 0, 0)),
                  pl.BlockSpec((tt, d), lambda t, e: (act_first + t, 0)),
                  pl.BlockSpec((d, slots * ib), lambda t, e: (0, e)), tok(slots), tok(slots)],
        out_specs=[rout, rout, rout, tok(slots)],
        out_shape=[jax.ShapeDtypeStruct((heads * k, n), I32), jax.ShapeDtypeStruct((heads * k, n), I32),
                   jax.ShapeDtypeStruct((heads * k, n), F32), jax.ShapeDtypeStruct((n, slots), F32)],
        compiler_params=_params("parallel", "arbitrary"), name="route_peer_u",
    )(pq, subkeys, h, ut, isel, jsel)


def _peer_v_kernel(i_ref, j_ref, g_ref, a_ref, x_ref, v_ref, o_ref, z_sc, zs_sc, *, ib, stride):
    step = pl.program_id(1)
    tt, slots = i_ref.shape

    @pl.when(step == 0)
    def _():
        a = a_ref[...]
        z_sc[...] = g_ref[...] * (0.5 * a * (1.0 + lax.erf(a * (2.0 ** -0.5))))

        def scatter(t, carry):
            irow = i_ref[pl.ds(t, 1), :]
            jrow = j_ref[pl.ds(t, 1), :]
            zrow = z_sc[pl.ds(t, 1), :]
            io = _iota((slots, slots), 0)
            zit = jnp.where(irow == io, zrow, 0.0).astype(BF16)
            oht = jnp.where(jrow == io, 1.0, 0.0).astype(BF16)
            zs_sc[pl.ds(pl.multiple_of(t * stride, 8), slots), :] = lax.dot_general(
                zit, oht, _NT, preferred_element_type=F32)
            return carry

        lax.fori_loop(0, tt, scatter, 0, unroll=32)
        o_ref[...] = x_ref[...]

    base = step * ib
    zblk = jnp.concatenate(
        [zs_sc[pl.ds(base + ii, tt, stride=stride), :].astype(BF16) for ii in range(ib)], axis=1)
    o_ref[...] += _dot(zblk, v_ref[...])


def _peer_v(isel, jsel, gates, act, x, v, tt, ib):
    n, d = x.shape
    slots = isel.shape[1]
    nblk = v.shape[0] // (slots * ib)
    stride = slots + 8
    tok = lambda c: pl.BlockSpec((tt, c), lambda t, e: (t, 0))
    return pl.pallas_call(
        functools.partial(_peer_v_kernel, ib=ib, stride=stride), grid=(n // tt, nblk),
        in_specs=[tok(slots)] * 4 + [tok(d), pl.BlockSpec((slots * ib, d), lambda t, e: (e, 0))],
        out_specs=tok(d), out_shape=jax.ShapeDtypeStruct((n, d), F32),
        scratch_shapes=[pltpu.VMEM((tt, slots), F32), pltpu.VMEM((tt * stride, slots), F32)],
        compiler_params=_params("parallel", "arbitrary"), name="peer_v",
    )(isel, jsel, gates, act, x, v)


def _final_kernel(x_ref, g_ref, o_ref):
    o_ref[...] = _rms(x_ref[...], g_ref[...])


def _final_norm(x, g, tm):
    n, d = x.shape
    return pl.pallas_call(
        _final_kernel, grid=(n // tm,),
        in_specs=[pl.BlockSpec((tm, d), lambda i: (i, 0)), pl.BlockSpec((1, d), lambda i: (0, 0))],
        out_specs=pl.BlockSpec((tm, d), lambda i: (i, 0)), out_shape=jax.ShapeDtypeStruct((n, d), F32),
        compiler_params=_params("parallel"), name="final_norm",
    )(x, g)


def _tile(n, name):
    pref = TILE[name]
    return pref if n % pref == 0 else n


def _peer(x2, h3, pq, lw):
    n = x2.shape[0]
    tt, ib = _tile(n, "peer_u"), TILE["peer_blocks"]
    assert _tile(n, "route") == tt
    groups = TILE["peer_groups"] if n % (TILE["peer_groups"] * tt) == 0 else 1
    per = n // tt // groups
    routed = [_route(pq, lw["subkeys"], tt, 0, per)]
    acts = []
    for g in range(groups):
        isel_g, jsel_g = routed[g][0].T, routed[g][1].T
        if g + 1 < groups:
            it, jt, gt, act = _route_peer_u(pq, lw["subkeys"], h3, lw["ut"], isel_g, jsel_g, tt, ib,
                                            (g + 1) * per, g * per)
            routed.append((it, jt, gt))
        else:
            act = _peer_u(h3, lw["ut"], isel_g, jsel_g, tt, ib, g * per)
        acts.append(act)
    cat = lambda k: jnp.concatenate([r[k] for r in routed], axis=1).T
    return _peer_v(cat(0), cat(1), cat(2), jnp.concatenate(acts, axis=0), x2, lw["v"], _tile(n, "peer_v"), ib)


def kernel(x_prompt, x_sample, cache_k, cache_v, cache_logf, state_hg, cache_mem_k, cache_mem_v, page_table, mem_prompt, ln_mix, w_in, b_fox_f, hg_lb, fox_gn, hg_gn, w_out, ln_x, ln_mem, w_xq, w_xk, w_xv, w_xo, ln_ffn, peer_wq, peer_subkeys, peer_u, peer_v, ln_final):
    depth, d_model, _ = w_in.shape
    bp, seq, _ = x_prompt.shape
    bs = x_sample.shape[0]
    fox_heads = b_fox_f.shape[1]
    fox_w = fox_gn.shape[1]
    hg_w = hg_gn.shape[1]
    hg_heads = hg_w // HG_DK
    n_phys, page = cache_k.shape[1], cache_k.shape[2]
    n_mem = mem_prompt.shape[1]
    x_w = w_xq.shape[2]
    x_heads = x_w // X_HEAD_DIM
    p_heads = peer_subkeys.shape[1]
    p_keys, p_half = peer_subkeys.shape[3], peer_subkeys.shape[4]
    pairs = fox_w // LANES

    lb_all = jnp.cumsum(jax.nn.softmax(hg_lb.astype(F32), axis=0), axis=0)
    row = lambda a: a.reshape(1, -1).astype(F32)

    xp = x_prompt.reshape(bp * seq, d_model)
    xs = x_sample.reshape(bs, d_model)
    mem = mem_prompt.reshape(bp * n_mem, d_model)
    ck_t = cache_k.transpose(0, 1, 3, 4, 2).reshape(depth, n_phys, fox_w, page)
    cv_t = cache_v.transpose(0, 1, 3, 4, 2).reshape(depth, n_phys, fox_w, page)
    clf_t = cache_logf.transpose(0, 1, 3, 2)
    state_t = state_hg.transpose(0, 2, 3, 4, 1)
    memk = cache_mem_k.reshape(depth, bs, n_mem * x_heads, X_HEAD_DIM)
    memv = cache_mem_v.reshape(depth, bs, n_mem * x_heads, X_HEAD_DIM)

    outs = {k: [] for k in ("kp", "vp", "fp", "hp", "mkp", "mvp", "ks", "vs", "fs", "hs")}
    for l in range(depth):
        wl = w_in[l]
        c0 = 3 * fox_w
        lw = dict(
            w_main=jnp.concatenate([wl[:, :c0], wl[:, c0 + fox_heads:]], axis=1).astype(BF16),
            w_ff=jnp.pad(wl[:, c0:c0 + fox_heads], ((0, 0), (0, LANES - fox_heads))).astype(BF16),
            b_f=jnp.pad(row(b_fox_f[l]), ((0, 0), (0, LANES - fox_heads))),
            lb=row(lb_all[l] - lb_all[0]),
            subkeys=peer_subkeys[l].reshape(p_heads * 2, p_keys, p_half).astype(BF16),
            ut=peer_u[l].astype(BF16).T, v=peer_v[l].astype(BF16),
        )
        w_out_b, w_xq_b, w_xo_b = w_out[l].astype(BF16), w_xq[l].astype(BF16), w_xo[l].astype(BF16)
        peer_wq_b = peer_wq[l].astype(BF16)

        tm = _tile(bp * seq, "tokenwise")
        fq, fk, fv, flf, hq, hk, hv, hlf, hgate = _inproj(xp, row(ln_mix[l]), lw["w_main"], lw["w_ff"], lw["b_f"],
                                                           lw["lb"], tm, fox_heads)
        lf_t = flf.reshape(bp, seq, fox_heads).transpose(0, 2, 1).reshape(bp * fox_heads, seq)
        c_t = _cumsum_lanes(lf_t, _tile(seq, "cumsum"))
        ck = c_t.reshape(bp * pairs, 2, seq)
        cq = c_t.reshape(bp, pairs, 2, seq).transpose(1, 0, 3, 2).reshape(pairs, bp * seq, 2)
        fo = _fox_prompt(fq, fk, fv, cq, ck, bp, _tile(seq, "fox"))
        ho, st = _hgrn_prompt(hq, hk, hv, hlf, bp)
        x1, qx = _merge(fo, ho, hgate, xp, row(fox_gn[l]), row(hg_gn[l]), w_out_b, row(ln_x[l]), w_xq_b, tm)
        mk, mv = _memkv(mem, row(ln_mem[l]), w_xk[l].astype(BF16), w_xv[l].astype(BF16), n_mem)
        ctx = _xattn_prompt(qx, mk, mv, bp, tm)
        x2, h3, pq = _xo_peerq(x1, ctx, w_xo_b, row(ln_ffn[l]), peer_wq_b, tm)
        xp = _peer(x2, h3, pq, lw)
        outs["kp"].append(fk.reshape(bp, seq, fox_heads, HEAD_DIM))
        outs["vp"].append(fv.reshape(bp, seq, fox_heads, HEAD_DIM))
        outs["fp"].append(flf.reshape(bp, seq, fox_heads))
        st5 = st.reshape(bp, pairs, 2, HG_DK, 2, HG_DK)
        s_heads = jnp.stack([st5[:, :, 0, :, 0, :], st5[:, :, 1, :, 1, :]], axis=2)
        outs["hp"].append(s_heads.reshape(bp, hg_heads, HG_DK, HG_DK).transpose(0, 1, 3, 2))
        outs["mkp"].append(mk.reshape(bp, n_mem, x_heads, X_HEAD_DIM))
        outs["mvp"].append(mv.reshape(bp, n_mem, x_heads, X_HEAD_DIM))

        sq, sk, sv, slf, tq, tk, tv, tlf, tgate = _inproj(xs, row(ln_mix[l]), lw["w_main"], lw["w_ff"], lw["b_f"],
                                                           lw["lb"], bs, fox_heads)
        tok3 = lambda a: a.reshape(bs, 1, a.shape[1])
        colm = lambda a: a.reshape(bs, a.shape[1], 1)
        fo_s = _fox_decode(page_table, colm(sq), colm(sk), colm(sv), colm(slf), ck_t, cv_t, clf_t,
                           l).reshape(bs, fox_w)
        lanes_r = lambda a: a.T.reshape(hg_heads, HG_DK, 1, bs)
        ho_t, s_new = _hgrn_step(lanes_r(tq), lanes_r(tk), lanes_r(tlf), tv.T.reshape(hg_heads, HG_DK, bs),
                                 state_t, l)
        x1s, qxs = _merge(fo_s, ho_t.reshape(hg_w, bs).T, tgate, xs, row(fox_gn[l]), row(hg_gn[l]), w_out_b,
                          row(ln_x[l]), w_xq_b, bs)
        qxs_t = jnp.pad(qxs.reshape(bs, x_heads, X_HEAD_DIM).transpose(0, 2, 1), ((0, 0), (0, 0), (0, 8 - x_heads)))
        ctx_s = _xattn_sample(qxs_t, memk, memv, l, x_heads, _tile(bs, "xattn_sample")).reshape(bs, x_w)
        x2s, h3s, pqs = _xo_peerq(x1s, ctx_s, w_xo_b, row(ln_ffn[l]), peer_wq_b, bs)
        xs = _peer(x2s, h3s, pqs, lw)
        outs["ks"].append(sk.reshape(bs, 1, fox_heads, HEAD_DIM))
        outs["vs"].append(sv.reshape(bs, 1, fox_heads, HEAD_DIM))
        outs["fs"].append(slf.reshape(bs, 1, fox_heads))
        outs["hs"].append(s_new)

    y_prompt = _final_norm(xp, row(ln_final), _tile(bp * seq, "tokenwise")).reshape(bp, seq, d_model)
    y_sample = _final_norm(xs, row(ln_final), bs).reshape(bs, 1, d_model)
    st_ = lambda k: jnp.stack(outs[k])
    return (y_prompt, y_sample, st_("kp"), st_("vp"), st_("fp"), st_("hp"), st_("mkp"), st_("mvp"),
            st_("ks"), st_("vs"), st_("fs"), st_("hs").transpose(0, 4, 1, 2, 3))
```

```python
import functools
import math

import jax
import jax.numpy as jnp
from jax import lax
from jax.experimental import pallas as pl
from jax.experimental.pallas import tpu as pltpu

F32 = jnp.float32
BF16 = jnp.bfloat16
I32 = jnp.int32
EPS = 1e-6
NEG_INF = float("-inf")

HEAD_DIM = 64
HG_DK = 64
HG_CHUNK = 64
HG_SUB = 16
X_HEAD_DIM = 128
PEER_TOPK = 16
LANES = 128
V7X_VMEM_BYTES = 64 * 1024 * 1024
VMEM_LIMIT = V7X_VMEM_BYTES - 8 * 1024 * 1024

TILE = dict(
    tokenwise=512,
    cumsum=512,
    fox=1024,
    route=1024,
    peer_groups=16,
    peer_u=1024,
    peer_v=512,
    peer_blocks=16,
    xattn_sample=8,
)

_NT = (((1,), (1,)), ((), ()))
_TN = (((0,), (0,)), ((), ()))


def _params(*sem):
    return pltpu.CompilerParams(dimension_semantics=sem, vmem_limit_bytes=VMEM_LIMIT)


def _rms(x, g):
    ms = jnp.mean(x * x, axis=-1, keepdims=True)
    return x * lax.rsqrt(ms + EPS) * g


def _log_sigmoid(x):
    return jnp.minimum(x, 0.0) - jnp.log1p(jnp.exp(-jnp.abs(x)))


def _dot(a, b):
    return jnp.dot(a, b, preferred_element_type=F32)


def _split3(x):
    hi = x.astype(BF16)
    r = x - hi.astype(F32)
    mid = r.astype(BF16)
    lo = (r - mid.astype(F32)).astype(BF16)
    return hi, mid, lo


def _prefix_rows(tril, x):
    return _dot(jnp.concatenate([tril] * 3, axis=1), jnp.concatenate(_split3(x), axis=0))


def _prefix_lanes(x, triu):
    return _dot(jnp.concatenate(_split3(x), axis=1), jnp.concatenate([triu] * 3, axis=0))


def _iota(shape, axis):
    return lax.broadcasted_iota(I32, shape, axis)


def _inproj_kernel(x_ref, g_ref, w_ref, wff_ref, bf_ref, lb_ref,
                   fq_ref, fk_ref, fv_ref, lf_ref, hq_ref, hk_ref, hv_ref, hlf_ref, gate_ref, *, fox_heads):
    h = _rms(x_ref[...], g_ref[...]).astype(BF16)
    w = w_ref.shape[1] // 7

    def mm(j):
        return _dot(h, w_ref[:, j * w:(j + 1) * w])

    fq_ref[...] = mm(0)
    fk_ref[...] = mm(1)
    fv_ref[...] = mm(2)
    ff = _dot(h, wff_ref[...])
    lf_ref[...] = _log_sigmoid(ff + bf_ref[...])[:, :fox_heads]
    hq_ref[...] = mm(3)
    z = mm(4)
    lb = lb_ref[...]
    a = jnp.log(lb)
    b = jnp.log1p(-lb) + _log_sigmoid(z)
    hlf_ref[...] = jnp.maximum(a, b) + jnp.log1p(jnp.exp(-jnp.abs(a - b)))
    hk_ref[...] = (1.0 - lb) * (1.0 / (1.0 + jnp.exp(z)))
    hv_ref[...] = mm(5)
    gate_ref[...] = mm(6)


def _inproj(x, g, w_main, w_ff, b_f, lb, tm, fox_heads):
    n, d = x.shape
    w = w_main.shape[1] // 7
    full = lambda a: pl.BlockSpec(a.shape, lambda i: (0,) * a.ndim)
    row = lambda c: pl.BlockSpec((tm, c), lambda i: (i, 0))
    outs = [jax.ShapeDtypeStruct((n, w), F32)] * 3 + [jax.ShapeDtypeStruct((n, fox_heads), F32)] + \
           [jax.ShapeDtypeStruct((n, w), F32)] * 5
    out_specs = [row(w)] * 3 + [row(fox_heads)] + [row(w)] * 5
    return pl.pallas_call(
        functools.partial(_inproj_kernel, fox_heads=fox_heads),
        grid=(n // tm,),
        in_specs=[row(d), full(g), full(w_main), full(w_ff), full(b_f), full(lb)],
        out_specs=out_specs, out_shape=outs,
        compiler_params=_params("parallel"), name="inproj",
    )(x, g, w_main, w_ff, b_f, lb)


def _cumsum_kernel(x_ref, o_ref, carry_ref):
    @pl.when(pl.program_id(0) == 0)
    def _():
        carry_ref[...] = jnp.zeros_like(carry_ref)

    x = x_ref[...]
    n = x.shape[1]
    tri = (_iota((n, n), 0) <= _iota((n, n), 1)).astype(BF16)
    c = _prefix_lanes(x, tri) + carry_ref[:, :1]
    o_ref[...] = c
    carry_ref[...] = jnp.broadcast_to(c[:, n - 1:n], carry_ref.shape)


def _cumsum_lanes(x, tc):
    r, t = x.shape
    return pl.pallas_call(
        _cumsum_kernel, grid=(t // tc,),
        in_specs=[pl.BlockSpec((r, tc), lambda i: (0, i))],
        out_specs=pl.BlockSpec((r, tc), lambda i: (0, i)),
        out_shape=jax.ShapeDtypeStruct((r, t), F32),
        scratch_shapes=[pltpu.VMEM((r, LANES), F32)],
        compiler_params=_params("arbitrary"), name="logf_cumsum",
    )(x)


def _fox_kernel(qt_ref, kt_ref, q_ref, k_ref, v_ref, cq_ref, ck_ref, o_ref, m_ref, cqr_ref, acc_ref, qh_ref, *,
                scale):
    qi = qt_ref[pl.program_id(2)]
    ki = kt_ref[pl.program_id(2)]
    tq = q_ref.shape[0]
    tk = k_ref.shape[0]
    lane = _iota((1, LANES), 1)
    log2e = math.log2(math.e)

    @pl.when(ki == 0)
    def _():
        m_ref[...] = jnp.full_like(m_ref, NEG_INF)
        acc_ref[...] = jnp.zeros_like(acc_ref)
        q = q_ref[...] * (scale * log2e)
        for hh in range(2):
            cqr_ref[hh] = jnp.broadcast_to(cq_ref[:, hh:hh + 1] * log2e, (tq, LANES))
            qh_ref[hh] = jnp.where((lane // HEAD_DIM) == hh, q, 0.0).astype(BF16)

    def step(masked, blocks):
        k = k_ref[...].astype(BF16)
        v = v_ref[...]
        ck = ck_ref[...] * log2e
        for hh in range(2):
            vh = jnp.where((lane // HEAD_DIM) == hh, v, 1.0).astype(BF16)
            for r0, r1, nk in blocks:
                s = lax.dot_general(qh_ref[hh, r0:r1], k[:nk], _NT, preferred_element_type=F32) - ck[hh:hh + 1, :nk]
                if masked:
                    causal = (ki * tk + _iota((1, nk), 1)) <= (qi * tq + r0 + _iota((r1 - r0, 1), 0))
                    s = jnp.where(causal, s, NEG_INF)
                m_prev = m_ref[hh, r0:r1]
                cq = cqr_ref[hh, r0:r1]
                m_new = jnp.maximum(m_prev, jnp.max(s, axis=1, keepdims=True) + cq)
                p = jnp.exp2(s - jnp.tile(m_new - cq, (1, nk // LANES)))
                acc_ref[hh, r0:r1] = jnp.exp2(m_prev - m_new) * acc_ref[hh, r0:r1] + _dot(p.astype(BF16), vh[:nk])
                m_ref[hh, r0:r1] = m_new

    @pl.when(ki < qi)
    def _():
        step(False, [(0, tq, tk)])

    @pl.when(ki == qi)
    def _():
        half = tq // 2
        step(True, [(0, half, tk // 2), (half, tq, tk)] if half % LANES == 0 else [(0, tq, tk)])
        a0 = acc_ref[0]
        a1 = acc_ref[1]
        o0 = a0 / pltpu.roll(a0, HEAD_DIM, axis=1)
        o1 = a1 / pltpu.roll(a1, HEAD_DIM, axis=1)
        o_ref[...] = jnp.where(lane < HEAD_DIM, o0, o1)


def _fox_prompt(q, k, v, cq, ck, batch, tq):
    n, w = q.shape
    t = n // batch
    pairs = w // LANES
    nq = t // tq
    kern = functools.partial(_fox_kernel, scale=HEAD_DIM ** -0.5)
    steps = [(i, j) for i in range(nq) for j in range(i + 1)]
    qi_tab = jnp.asarray([s[0] for s in steps], I32)
    ki_tab = jnp.asarray([s[1] for s in steps], I32)
    grid_spec = pltpu.PrefetchScalarGridSpec(
        num_scalar_prefetch=2, grid=(batch, pairs, len(steps)),
        in_specs=[
            pl.BlockSpec((tq, LANES), lambda b, g, s, qt, kt: (b * nq + qt[s], g)),
            pl.BlockSpec((tq, LANES), lambda b, g, s, qt, kt: (b * nq + kt[s], g)),
            pl.BlockSpec((tq, LANES), lambda b, g, s, qt, kt: (b * nq + kt[s], g)),
            pl.BlockSpec((None, tq, 2), lambda b, g, s, qt, kt: (g, b * nq + qt[s], 0)),
            pl.BlockSpec((None, 2, tq), lambda b, g, s, qt, kt: (b * pairs + g, 0, kt[s])),
        ],
        out_specs=pl.BlockSpec((tq, LANES), lambda b, g, s, qt, kt: (b * nq + qt[s], g)),
        scratch_shapes=[pltpu.VMEM((2, tq, LANES), F32)] * 3 + [pltpu.VMEM((2, tq, LANES), BF16)],
    )
    return pl.pallas_call(
        kern, grid_spec=grid_spec, out_shape=jax.ShapeDtypeStruct((n, w), F32),
        compiler_params=_params("parallel", "parallel", "arbitrary"), name="fox_prompt",
    )(qi_tab, ki_tab, q, k, v, cq, ck)


def _hgrn_kernel(q_ref, k_ref, v_ref, lf_ref, o_ref, st_ref, st_sc, *, also=()):
    n = pl.program_id(0)
    nb, c, w = q_ref.shape
    pairs = w // LANES

    @pl.when(n == 0)
    def _():
        st_sc[...] = jnp.zeros_like(st_sc)

    tril = (_iota((c, c), 1) <= _iota((c, c), 0)).astype(BF16)
    same_head = (_iota((LANES, LANES), 0) // HG_DK) == (_iota((LANES, LANES), 1) // HG_DK)
    seg = same_head.astype(BF16)
    sc = HG_SUB
    ti = _iota((sc, sc, LANES), 0)
    si = _iota((sc, sc, LANES), 1)
    first_head = _iota((1, LANES), 1) < HG_DK
    for bg in range(nb * pairs):
        r, g = divmod(bg, pairs)
        sl = slice(g * LANES, (g + 1) * LANES)
        q = q_ref[r, :, sl]
        k = k_ref[r, :, sl]
        v = v_ref[r, :, sl]
        vb = v.astype(BF16)
        b = _prefix_rows(tril, lf_ref[r, :, sl])
        st = st_sc[r, g]
        qe = (q * jnp.exp(b)).astype(BF16)
        o = lax.dot_general(qe, st.astype(BF16), _NT, preferred_element_type=F32)
        rows = []
        for i in range(c // sc):
            lo, hi = i * sc, (i + 1) * sc
            qi, ki, bi = q[lo:hi], k[lo:hi], b[lo:hi]
            decay = jnp.exp(jnp.where(si <= ti, bi[:, None, :] - bi[None, :, :], NEG_INF))
            p = (qi[:, None, :] * ki[None, :, :]) * decay
            a = _dot(p.reshape(sc * sc, LANES).astype(BF16), seg)
            oi = jnp.sum(a.reshape(sc, sc, LANES) * v[None, lo:hi, :], axis=1)
            if i > 0:
                br = b[lo - 1:lo, :]
                qt = qi * jnp.exp(bi - br)
                q2 = jnp.concatenate([jnp.where(first_head, qt, 0.0), jnp.where(first_head, 0.0, qt)], axis=0)
                ks = (k[:lo] * jnp.exp(br - b[:lo])).astype(BF16)
                a2 = lax.dot_general(q2.astype(BF16), ks, _NT, preferred_element_type=F32)
                o2 = _dot(a2.astype(BF16), vb[:lo])
                oi = oi + jnp.where(first_head, o2[:sc], o2[sc:])
            rows.append(oi)
        o_ref[r, :, sl] = o + jnp.concatenate(rows, axis=0)
        b_last = b[c - 1:c, :]
        kd = (k * jnp.exp(b_last - b)).astype(BF16)
        upd = lax.dot_general(vb, kd, _TN, preferred_element_type=F32)
        st_sc[r, g] = st * jnp.exp(b_last) + jnp.where(same_head, upd, 0.0)
        share = -(-len(also) // (nb * pairs))
        for thunk in also[bg * share:(bg + 1) * share]:
            thunk()

    @pl.when(n == pl.num_programs(0) - 1)
    def _():
        st_ref[...] = st_sc[...]


def _hgrn_prompt(q, k, v, lf, batch):
    n, w = q.shape
    t = n // batch
    pairs = w // LANES
    rows3 = lambda a: a.reshape(batch, t, w)
    blk = pl.BlockSpec((batch, HG_CHUNK, w), lambda i: (0, i, 0))
    st_shape = (batch, pairs, LANES, LANES)
    o, st = pl.pallas_call(
        _hgrn_kernel, grid=(t // HG_CHUNK,),
        in_specs=[blk] * 4,
        out_specs=[blk, pl.BlockSpec(st_shape, lambda i: (0, 0, 0, 0))],
        out_shape=[jax.ShapeDtypeStruct((batch, t, w), F32), jax.ShapeDtypeStruct(st_shape, F32)],
        scratch_shapes=[pltpu.VMEM(st_shape, F32)],
        compiler_params=_params("arbitrary"), name="hgrn_prompt",
    )(rows3(q), rows3(k), rows3(v), rows3(lf))
    return o.reshape(n, w), st


def _fox_decode_stages(pt_ref, q_ref, kn_ref, vn_ref, lfn_ref, *refs, scale, n_pages):
    k_refs = refs[:n_pages]
    v_refs = refs[n_pages:2 * n_pages]
    lf_refs = refs[2 * n_pages:3 * n_pages]
    o_ref = refs[3 * n_pages]
    w, ps = k_refs[0].shape
    nh = lf_refs[0].shape[0]
    hd = w // nh
    heads = lambda x: x.reshape(nh, hd, x.shape[1])
    st = dict(scores=[], carry=jnp.zeros((nh, 1), F32), acc=jnp.zeros((w, ps), F32))

    def setup():
        st["q"] = q_ref[...] * scale
        st["qb"] = jnp.broadcast_to(st["q"], (w, ps))
        st["triu"] = (_iota((ps, ps), 0) <= _iota((ps, ps), 1)).astype(BF16)

    def score(p):
        s = jnp.sum(heads(k_refs[p][...] * st["qb"]), axis=1)
        within = _prefix_lanes(lf_refs[p][...], st["triu"])
        st["scores"].append(s - (within + st["carry"]))
        st["carry"] = st["carry"] + within[:, ps - 1:ps]

    def softmax():
        scores = st["scores"]
        s_new = jnp.sum(heads(kn_ref[...] * st["q"]), axis=1) - (st["carry"] + lfn_ref[...])
        m_tile = scores[0]
        for s in scores[1:]:
            m_tile = jnp.maximum(m_tile, s)
        m = jnp.maximum(s_new, jnp.max(m_tile, axis=1, keepdims=True))
        st["e_new"] = jnp.exp(s_new - m)
        st["exps"] = [jnp.exp(s - m) for s in scores]
        l_tile = st["exps"][0]
        for e in st["exps"][1:]:
            l_tile = l_tile + e
        st["inv"] = 1.0 / (st["e_new"] + jnp.sum(l_tile, axis=1, keepdims=True))

    def value(p):
        pb = jnp.broadcast_to(st["exps"][p][:, None, :], (nh, hd, ps)).reshape(w, ps)
        st["acc"] = st["acc"] + v_refs[p][...] * pb

    def finish():
        rep = lambda x: jnp.broadcast_to(x[:, None, :], (nh, hd, 1)).reshape(w, 1)
        o_ref[...] = (jnp.sum(st["acc"], axis=1, keepdims=True) + rep(st["e_new"]) * vn_ref[...]) * rep(st["inv"])

    part = functools.partial
    return ([setup] + [part(score, p) for p in range(n_pages)] + [softmax]
            + [part(value, p) for p in range(n_pages)] + [finish])


def _fox_decode_kernel(*refs, scale, n_pages):
    for stage in _fox_decode_stages(*refs, scale=scale, n_pages=n_pages):
        stage()


def _fox_decode(page_table, q, k_new, v_new, lf_new, cache_kt, cache_vt, cache_lft, layer):
    r, w, _ = q.shape
    nh = lf_new.shape[1]
    n_pages = page_table.shape[1]
    ps = cache_kt.shape[3]
    pt = page_table.reshape(-1)
    tok = lambda c: pl.BlockSpec((None, c, 1), lambda i, pt: (i, 0, 0))

    def page(p, rows):
        return pl.BlockSpec((None, None, rows, ps), lambda i, pt: (layer, pt[i * n_pages + p], 0, 0))

    pages = lambda rows: [page(p, rows) for p in range(n_pages)]
    grid_spec = pltpu.PrefetchScalarGridSpec(
        num_scalar_prefetch=1, grid=(r,),
        in_specs=[tok(w), tok(w), tok(w), tok(nh)] + pages(w) + pages(w) + pages(nh),
        out_specs=tok(w),
    )
    return pl.pallas_call(
        functools.partial(_fox_decode_kernel, scale=HEAD_DIM ** -0.5, n_pages=n_pages),
        grid_spec=grid_spec, out_shape=jax.ShapeDtypeStruct((r, w, 1), F32),
        compiler_params=_params("parallel"), name="fox_decode",
    )(pt, q, k_new, v_new, lf_new, *([cache_kt] * n_pages), *([cache_vt] * n_pages), *([cache_lft] * n_pages))


def _hgrn_decode_kernel(pt_ref, hq_ref, hk_ref, hv_ref, hlf_ref, q_ref, kn_ref, vn_ref, lfn_ref, *refs, scale,
                        n_pages):
    pages = refs[:3 * n_pages]
    ho_ref, st_ref, od_ref, st_sc = refs[3 * n_pages:]
    stages = _fox_decode_stages(pt_ref, q_ref, kn_ref, vn_ref, lfn_ref, *pages, od_ref, scale=scale, n_pages=n_pages)
    _hgrn_kernel(hq_ref, hk_ref, hv_ref, hlf_ref, ho_ref, st_ref, st_sc, also=stages)


def _hgrn_prompt_fox_decode(hq, hk, hv, hlf, batch, page_table, q, k_new, v_new, lf_new, cache_kt, cache_vt,
                            cache_lft, layer):
    n, w = hq.shape
    t = n // batch
    pairs = w // LANES
    r, wd, _ = q.shape
    nh = lf_new.shape[1]
    n_pages = page_table.shape[1]
    ps = cache_kt.shape[3]
    assert t // HG_CHUNK == r
    rows3 = lambda a: a.reshape(batch, t, w)
    blk = pl.BlockSpec((batch, HG_CHUNK, w), lambda i, pt: (0, i, 0))
    st_shape = (batch, pairs, LANES, LANES)
    tok = lambda c: pl.BlockSpec((None, c, 1), lambda i, pt: (i, 0, 0))

    def page(p, rows):
        return pl.BlockSpec((None, None, rows, ps), lambda i, pt: (layer, pt[i * n_pages + p], 0, 0))

    pages = lambda rows: [page(p, rows) for p in range(n_pages)]
    grid_spec = pltpu.PrefetchScalarGridSpec(
        num_scalar_prefetch=1, grid=(r,),
        in_specs=[blk] * 4 + [tok(wd), tok(wd), tok(wd), tok(nh)] + pages(wd) + pages(wd) + pages(nh),
        out_specs=[blk, pl.BlockSpec(st_shape, lambda i, pt: (0, 0, 0, 0)), tok(wd)],
        scratch_shapes=[pltpu.VMEM(st_shape, F32)],
    )
    o, st, od = pl.pallas_call(
        functools.partial(_hgrn_decode_kernel, scale=HEAD_DIM ** -0.5, n_pages=n_pages),
        grid_spec=grid_spec,
        out_shape=[jax.ShapeDtypeStruct((batch, t, w), F32), jax.ShapeDtypeStruct(st_shape, F32),
                   jax.ShapeDtypeStruct((r, wd, 1), F32)],
        compiler_params=_params("arbitrary"), name="hgrn_prompt_fox_decode",
    )(page_table.reshape(-1), rows3(hq), rows3(hk), rows3(hv), rows3(hlf), q, k_new, v_new, lf_new,
      *([cache_kt] * n_pages), *([cache_vt] * n_pages), *([cache_lft] * n_pages))
    return o.reshape(n, w), st, od


def _hgrn_step_kernel(q_ref, k_ref, lf_ref, v_ref, s_ref, o_ref, so_ref):
    q = q_ref[...]
    k = k_ref[...]
    f = jnp.exp(lf_ref[...])
    v = v_ref[...]
    s = s_ref[...]
    so_ref[...] = f * s + k * v[None]
    o_ref[...] = jnp.sum((q * f) * s, axis=0) + jnp.sum(q * k, axis=0) * v


def _hgrn_step(q, k, lf, v, state_t, layer):
    h, dk, _, r = q.shape
    dv = v.shape[1]
    col = pl.BlockSpec((None, dk, 1, r), lambda i: (i, 0, 0, 0))
    rowv = pl.BlockSpec((None, dv, r), lambda i: (i, 0, 0))
    return pl.pallas_call(
        _hgrn_step_kernel, grid=(h,),
        in_specs=[col, col, col, rowv, pl.BlockSpec((None, None, dk, dv, r), lambda i: (layer, i, 0, 0, 0))],
        out_specs=[rowv, pl.BlockSpec((None, dk, dv, r), lambda i: (i, 0, 0, 0))],
        out_shape=[jax.ShapeDtypeStruct((h, dv, r), F32), jax.ShapeDtypeStruct((h, dk, dv, r), F32)],
        compiler_params=_params("parallel"), name="hgrn_step",
    )(q, k, lf, v, state_t)


def _head_mean_sq(y, width):
    w = y.shape[1]
    seg = ((_iota((w, w), 0) // width) == (_iota((w, w), 1) // width)).astype(BF16)
    sq = y * y
    hi = sq.astype(BF16)
    lo = (sq - hi.astype(F32)).astype(BF16)
    return (_dot(hi, seg) + _dot(lo, seg)) * (1.0 / width)


def _merge_kernel(fo_ref, ho_ref, gate_ref, x_ref, fgn_ref, hgn_ref, wo_ref, lnx_ref, wxq_ref, x1_ref, qx_ref):
    fo = fo_ref[...]
    ho = ho_ref[...]
    fw = fo.shape[1]
    fn = fo * lax.rsqrt(_head_mean_sq(fo, HEAD_DIM) + EPS) * fgn_ref[...]
    gate = gate_ref[...]
    hn = ho * lax.rsqrt(_head_mean_sq(ho, HEAD_DIM) + EPS) * hgn_ref[...] * (gate / (1.0 + jnp.exp(-gate)))
    y = _dot(fn.astype(BF16), wo_ref[:fw, :]) + _dot(hn.astype(BF16), wo_ref[fw:, :])
    x1 = x_ref[...] + y
    x1_ref[...] = x1
    qx_ref[...] = _dot(_rms(x1, lnx_ref[...]).astype(BF16), wxq_ref[...])


def _merge(fo, ho, gate, x, fgn, hgn, w_out, ln_x, w_xq, tm):
    n, d = x.shape
    xw = w_xq.shape[1]
    full = lambda a: pl.BlockSpec(a.shape, lambda i: (0,) * a.ndim)
    row = lambda c: pl.BlockSpec((tm, c), lambda i: (i, 0))
    return pl.pallas_call(
        _merge_kernel, grid=(n // tm,),
        in_specs=[row(fo.shape[1]), row(ho.shape[1]), row(gate.shape[1]), row(d),
                  full(fgn), full(hgn), full(w_out), full(ln_x), full(w_xq)],
        out_specs=[row(d), row(xw)],
        out_shape=[jax.ShapeDtypeStruct((n, d), F32), jax.ShapeDtypeStruct((n, xw), F32)],
        compiler_params=_params("parallel"), name="merge",
    )(fo, ho, gate, x, fgn, hgn, w_out, ln_x, w_xq)


def _memkv_kernel(m_ref, g_ref, wk_ref, wv_ref, k_ref, v_ref):
    m = _rms(m_ref[...], g_ref[...]).astype(BF16)
    k_ref[...] = _dot(m, wk_ref[...])
    v_ref[...] = _dot(m, wv_ref[...])


def _memkv(mem, g, wk, wv, tm):
    n, d = mem.shape
    xw = wk.shape[1]
    full = lambda a: pl.BlockSpec(a.shape, lambda i: (0,) * a.ndim)
    row = lambda c: pl.BlockSpec((tm, c), lambda i: (i, 0))
    return pl.pallas_call(
        _memkv_kernel, grid=(n // tm,),
        in_specs=[row(d), full(g), full(wk), full(wv)],
        out_specs=[row(xw), row(xw)],
        out_shape=[jax.ShapeDtypeStruct((n, xw), F32)] * 2,
        compiler_params=_params("parallel"), name="memkv",
    )(mem, g, wk, wv)


def _xattn_prompt_kernel(q_ref, mk_ref, mv_ref, o_ref, *, scale):
    heads = q_ref.shape[1] // X_HEAD_DIM
    for h in range(heads):
        sl = slice(h * X_HEAD_DIM, (h + 1) * X_HEAD_DIM)
        q = (q_ref[:, sl] * scale).astype(BF16)
        s = lax.dot_general(q, mk_ref[:, sl].astype(BF16), _NT, preferred_element_type=F32)
        e = jnp.exp(s - jnp.max(s, axis=1, keepdims=True))
        p = e / jnp.sum(e, axis=1, keepdims=True)
        o_ref[:, sl] = _dot(p.astype(BF16), mv_ref[:, sl].astype(BF16))


def _xattn_prompt(q, mk, mv, batch, tm):
    n, xw = q.shape
    t = n // batch
    nm = mk.shape[0] // batch
    nt = t // tm
    return pl.pallas_call(
        functools.partial(_xattn_prompt_kernel, scale=X_HEAD_DIM ** -0.5), grid=(batch, nt),
        in_specs=[pl.BlockSpec((tm, xw), lambda b, i: (b * nt + i, 0)),
                  pl.BlockSpec((nm, xw), lambda b, i: (b, 0)),
                  pl.BlockSpec((nm, xw), lambda b, i: (b, 0))],
        out_specs=pl.BlockSpec((tm, xw), lambda b, i: (b * nt + i, 0)),
        out_shape=jax.ShapeDtypeStruct((n, xw), F32),
        compiler_params=_params("parallel", "parallel"), name="xattn_prompt",
    )(q, mk, mv)


def _xattn_sample_kernel(qt_ref, mk_ref, mv_ref, o_ref, *, scale, heads):
    rb, hd, cols = qt_ref.shape
    nm = mk_ref.shape[1] // heads
    col = _iota((hd, cols), 1)
    erow = _iota((cols, hd), 0)
    valid = _iota((1, cols), 1) < heads
    for r in range(rb):
        qt = qt_ref[r] * scale
        s = jnp.zeros((nm, cols), F32)
        for h in range(heads):
            kh = mk_ref[r, pl.ds(h, nm, stride=heads), :].astype(BF16)
            s = s + _dot(kh, jnp.where(col == h, qt, 0.0).astype(BF16))
        e = jnp.exp(s - jnp.max(s, axis=0, keepdims=True))
        p = jnp.where(valid, e / jnp.sum(e, axis=0, keepdims=True), 0.0).astype(BF16)
        outs = []
        for h in range(heads):
            pe = _dot(p, (erow == h).astype(BF16))
            vh = mv_ref[r, pl.ds(h, nm, stride=heads), :]
            outs.append(jnp.sum((pe * vh).reshape(nm // 8, 8, hd), axis=0))
        o_ref[r] = jnp.sum(jnp.concatenate(outs, axis=1), axis=0, keepdims=True)


def _xattn_sample(q_t, mem_k, mem_v, layer, heads, rb):
    r, hd, cols = q_t.shape
    rows = mem_k.shape[2]
    mem = pl.BlockSpec((None, rb, rows, hd), lambda i: (layer, i, 0, 0))
    return pl.pallas_call(
        functools.partial(_xattn_sample_kernel, scale=X_HEAD_DIM ** -0.5, heads=heads), grid=(r // rb,),
        in_specs=[pl.BlockSpec((rb, hd, cols), lambda i: (i, 0, 0)), mem, mem],
        out_specs=pl.BlockSpec((rb, 1, heads * hd), lambda i: (i, 0, 0)),
        out_shape=jax.ShapeDtypeStruct((r, 1, heads * hd), F32),
        compiler_params=_params("parallel"), name="xattn_sample",
    )(q_t, mem_k, mem_v)


def _xo_peerq_kernel(x1_ref, ctx_ref, wxo_ref, lnf_ref, wq_ref, x2_ref, h_ref, pq_ref):
    x2 = x1_ref[...] + _dot(ctx_ref[...].astype(BF16), wxo_ref[...])
    x2_ref[...] = x2
    h = _rms(x2, lnf_ref[...]).astype(BF16)
    h_ref[...] = h
    pq_ref[...] = _dot(h, wq_ref[...]).astype(BF16)


def _xo_peerq(x1, ctx, w_xo, ln_ffn, peer_wq, tm):
    n, d = x1.shape
    qw = peer_wq.shape[1]
    full = lambda a: pl.BlockSpec(a.shape, lambda i: (0,) * a.ndim)
    row = lambda c: pl.BlockSpec((tm, c), lambda i: (i, 0))
    return pl.pallas_call(
        _xo_peerq_kernel, grid=(n // tm,),
        in_specs=[row(d), row(ctx.shape[1]), full(w_xo), full(ln_ffn), full(peer_wq)],
        out_specs=[row(d), row(d), row(qw)],
        out_shape=[jax.ShapeDtypeStruct((n, d), F32), jax.ShapeDtypeStruct((n, d), BF16),
                   jax.ShapeDtypeStruct((n, qw), BF16)],
        compiler_params=_params("parallel"), name="xo_peerq",
    )(x1, ctx, w_xo, ln_ffn, peer_wq)


def _top16(s):
    kk = s.shape[0]
    kio = _iota(s.shape, 0).astype(F32)
    vals, idxs = [], []
    for _ in range(PEER_TOPK):
        m = jnp.max(s, axis=0, keepdims=True)
        idx = jnp.min(jnp.where(s == m, kio, float(kk)), axis=0, keepdims=True)
        s = jnp.where(kio == idx, NEG_INF, s)
        vals.append(m)
        idxs.append(idx)
    return jnp.concatenate(vals, axis=0), jnp.concatenate(idxs, axis=0).astype(I32)


def _gather16(x, idx):
    lo, hi = x[:8], x[8:]
    out = []
    for part in (idx[:8], idx[8:]):
        low3 = part & 7
        out.append(jnp.where(part < 8, jnp.take_along_axis(lo, low3, axis=0), jnp.take_along_axis(hi, low3, axis=0)))
    return jnp.concatenate(out, axis=0)


def _route_kernel(pq_ref, sk_ref, i_ref, j_ref, g_ref, *, unrolled=False):
    half = sk_ref.shape[2]
    tt = min(2 * LANES, pq_ref.shape[0])
    k = PEER_TOPK

    def sub_tile(n, carry):
        r0 = pl.multiple_of(n * tt, tt)
        sv, si = [], []
        for p in range(2):
            q = pq_ref[pl.ds(r0, tt), p * half:(p + 1) * half]
            st = lax.dot_general(sk_ref[p], q, _NT, preferred_element_type=F32)
            v, i = _top16(st)
            sv.append(v)
            si.append(i)
        sub = 8
        rows8 = _iota((sub, tt), 0)
        cands = [sv[0][0:1, :] + sv[1]]
        for a in range(1, sub):
            cands.append(jnp.where(rows8 < k // (a + 1), sv[0][a:a + 1, :] + sv[1][:sub, :], NEG_INF))
        cands.append(sv[0][sub:, :] + sv[1][0:1, :])
        cand = jnp.concatenate(cands, axis=0)
        nrows = cand.shape[0]
        pio = _iota(cand.shape, 0).astype(F32)
        tv, tp = [], []
        for _ in range(k):
            m = jnp.max(cand, axis=0, keepdims=True)
            pos = jnp.min(jnp.where(cand == m, pio, float(nrows)), axis=0, keepdims=True)
            cand = jnp.where(pio == pos, NEG_INF, cand)
            tv.append(m)
            tp.append(pos)
        top = jnp.concatenate(tv, axis=0)
        e = jnp.exp(top - top[0:1, :])
        g_ref[:, pl.ds(r0, tt)] = e / jnp.sum(e, axis=0, keepdims=True)
        pos = jnp.concatenate(tp, axis=0).astype(I32)
        mid = pos - k
        first, last = pos < k, pos >= nrows - (k - sub)
        a_sel = jnp.where(first, 0, jnp.where(last, pos - (nrows - k), 1 + (mid >> 3)))
        b_sel = jnp.where(first, pos, jnp.where(last, 0, mid & (sub - 1)))
        i_ref[:, pl.ds(r0, tt)] = _gather16(si[0], a_sel)
        j_ref[:, pl.ds(r0, tt)] = _gather16(si[1], b_sel)
        return carry

    if unrolled:
        return [functools.partial(sub_tile, n, 0) for n in range(pq_ref.shape[0] // tt)]
    lax.fori_loop(0, pq_ref.shape[0] // tt, sub_tile, 0)


def _route(pq, subkeys, tt, first=0, tiles=None):
    hp, keys, half = subkeys.shape
    heads = hp // 2
    k = PEER_TOPK
    n = (pq.shape[0] // tt if tiles is None else tiles) * tt
    out = pl.BlockSpec((k, tt), lambda i, h: (h, i))
    return pl.pallas_call(
        _route_kernel, grid=(n // tt, heads),
        in_specs=[pl.BlockSpec((tt, 2 * half), lambda i, h: (first + i, h)),
                  pl.BlockSpec((2, keys, half), lambda i, h: (h, 0, 0))],
        out_specs=[out, out, out],
        out_shape=[jax.ShapeDtypeStruct((heads * k, n), I32), jax.ShapeDtypeStruct((heads * k, n), I32),
                   jax.ShapeDtypeStruct((heads * k, n), F32)],
        compiler_params=_params("parallel", "parallel"), name="peer_route",
    )(pq, subkeys)


def _peer_u_groups(h_ref, ut_ref, i_ref, j_ref, a_ref, ib, group=4):
    step = pl.program_id(1)

    @pl.when(step == 0)
    def _():
        a_ref[...] = jnp.zeros_like(a_ref)

    nj = j_ref.shape[1]

    def run(g0):
        isel = i_ref[...]
        jsel = j_ref[...]
        acc = a_ref[...]
        a_all = _dot(h_ref[...], ut_ref[:, g0 * nj:(g0 + group) * nj])
        for ii in range(group):
            a = a_all[:, ii * nj:(ii + 1) * nj]
            acc = jnp.where(isel == step * ib + g0 + ii, jnp.take_along_axis(a, jsel, axis=1), acc)
        a_ref[...] = acc

    return [functools.partial(run, g0) for g0 in range(0, ib, group)]


def _peer_u_kernel(h_ref, ut_ref, i_ref, j_ref, a_ref, *, ib):
    for run in _peer_u_groups(h_ref, ut_ref, i_ref, j_ref, a_ref, ib):
        run()


def _peer_u(h, ut, isel, jsel, tt, ib, first=0):
    d = h.shape[1]
    n, slots = isel.shape
    nblk = ut.shape[1] // (slots * ib)
    tok = lambda c: pl.BlockSpec((tt, c), lambda t, e: (t, 0))
    return pl.pallas_call(
        functools.partial(_peer_u_kernel, ib=ib), grid=(n // tt, nblk),
        in_specs=[pl.BlockSpec((tt, d), lambda t, e: (first + t, 0)),
                  pl.BlockSpec((d, slots * ib), lambda t, e: (0, e)), tok(slots), tok(slots)],
        out_specs=tok(slots), out_shape=jax.ShapeDtypeStruct((n, slots), F32),
        compiler_params=_params("parallel", "arbitrary"), name="peer_u",
    )(h, ut, isel, jsel)


def _route_peer_u_kernel(pq_ref, sk_ref, h_ref, ut_ref, isel_ref, jsel_ref, i_ref, j_ref, g_ref, a_ref, *, ib):
    matmuls = _peer_u_groups(h_ref, ut_ref, isel_ref, jsel_ref, a_ref, ib)
    tiles = _route_kernel(pq_ref, sk_ref, i_ref, j_ref, g_ref, unrolled=True)
    for n in range(max(len(matmuls), len(tiles))):
        if n < len(matmuls):
            matmuls[n]()
        if n < len(tiles):
            tiles[n]()


def _route_peer_u(pq, subkeys, h, ut, isel, jsel, tt, ib, route_first, act_first):
    d = h.shape[1]
    n, slots = isel.shape
    hp, keys, half = subkeys.shape
    heads = hp // 2
    assert ut.shape[1] == heads * slots * ib
    k = PEER_TOPK
    tok = lambda c: pl.BlockSpec((tt, c), lambda t, e: (t, 0))
    rout = pl.BlockSpec((k, tt), lambda t, e: (e, t))
    return pl.pallas_call(
        functools.partial(_route_peer_u_kernel, ib=ib), grid=(n // tt, heads),
        in_specs=[pl.BlockSpec((tt, 2 * half), lambda t, e: (route_first + t, e)),
                  pl.BlockSpec((2, keys, half), lambda t, e: (e, 0, 0)),
                  pl.BlockSpec((tt, d), lambda t, e: (act_first + t, 0)),
                  pl.BlockSpec((d, slots * ib), lambda t, e: (0, e)), tok(slots), tok(slots)],
        out_specs=[rout, rout, rout, tok(slots)],
        out_shape=[jax.ShapeDtypeStruct((heads * k, n), I32), jax.ShapeDtypeStruct((heads * k, n), I32),
                   jax.ShapeDtypeStruct((heads * k, n), F32), jax.ShapeDtypeStruct((n, slots), F32)],
        compiler_params=_params("parallel", "arbitrary"), name="route_peer_u",
    )(pq, subkeys, h, ut, isel, jsel)


def _peer_v_kernel(i_ref, j_ref, g_ref, a_ref, x_ref, v_ref, o_ref, z_sc, zs_sc, *, ib, stride):
    step = pl.program_id(1)
    tt, slots = i_ref.shape

    @pl.when(step == 0)
    def _():
        a = a_ref[...]
        z_sc[...] = g_ref[...] * (0.5 * a * (1.0 + lax.erf(a * (2.0 ** -0.5))))

        def scatter(t, carry):
            irow = i_ref[pl.ds(t, 1), :]
            jrow = j_ref[pl.ds(t, 1), :]
            zrow = z_sc[pl.ds(t, 1), :]
            io = _iota((slots, slots), 0)
            zit = jnp.where(irow == io, zrow, 0.0).astype(BF16)
            oht = jnp.where(jrow == io, 1.0, 0.0).astype(BF16)
            zs_sc[pl.ds(pl.multiple_of(t * stride, 8), slots), :] = lax.dot_general(
                zit, oht, _NT, preferred_element_type=F32)
            return carry

        lax.fori_loop(0, tt, scatter, 0, unroll=32)
        o_ref[...] = x_ref[...]

    base = step * ib
    zblk = jnp.concatenate(
        [zs_sc[pl.ds(base + ii, tt, stride=stride), :].astype(BF16) for ii in range(ib)], axis=1)
    o_ref[...] += _dot(zblk, v_ref[...])


def _peer_v(isel, jsel, gates, act, x, v, tt, ib):
    n, d = x.shape
    slots = isel.shape[1]
    nblk = v.shape[0] // (slots * ib)
    stride = slots + 8
    tok = lambda c: pl.BlockSpec((tt, c), lambda t, e: (t, 0))
    return pl.pallas_call(
        functools.partial(_peer_v_kernel, ib=ib, stride=stride), grid=(n // tt, nblk),
        in_specs=[tok(slots)] * 4 + [tok(d), pl.BlockSpec((slots * ib, d), lambda t, e: (e, 0))],
        out_specs=tok(d), out_shape=jax.ShapeDtypeStruct((n, d), F32),
        scratch_shapes=[pltpu.VMEM((tt, slots), F32), pltpu.VMEM((tt * stride, slots), F32)],
        compiler_params=_params("parallel", "arbitrary"), name="peer_v",
    )(isel, jsel, gates, act, x, v)


def _final_kernel(x_ref, g_ref, o_ref):
    o_ref[...] = _rms(x_ref[...], g_ref[...])


def _final_norm(x, g, tm):
    n, d = x.shape
    return pl.pallas_call(
        _final_kernel, grid=(n // tm,),
        in_specs=[pl.BlockSpec((tm, d), lambda i: (i, 0)), pl.BlockSpec((1, d), lambda i: (0, 0))],
        out_specs=pl.BlockSpec((tm, d), lambda i: (i, 0)), out_shape=jax.ShapeDtypeStruct((n, d), F32),
        compiler_params=_params("parallel"), name="final_norm",
    )(x, g)


def _tile(n, name):
    pref = TILE[name]
    return pref if n % pref == 0 else n


def _peer(x2, h3, pq, lw):
    n = x2.shape[0]
    tt, ib = _tile(n, "peer_u"), TILE["peer_blocks"]
    assert _tile(n, "route") == tt
    groups = TILE["peer_groups"] if n % (TILE["peer_groups"] * tt) == 0 else 1
    per = n // tt // groups
    routed = [_route(pq, lw["subkeys"], tt, 0, per)]
    acts = []
    for g in range(groups):
        isel_g, jsel_g = routed[g][0].T, routed[g][1].T
        if g + 1 < groups:
            it, jt, gt, act = _route_peer_u(pq, lw["subkeys"], h3, lw["ut"], isel_g, jsel_g, tt, ib,
                                            (g + 1) * per, g * per)
            routed.append((it, jt, gt))
        else:
            act = _peer_u(h3, lw["ut"], isel_g, jsel_g, tt, ib, g * per)
        acts.append(act)
    cat = lambda k: jnp.concatenate([r[k] for r in routed], axis=1).T
    return _peer_v(cat(0), cat(1), cat(2), jnp.concatenate(acts, axis=0), x2, lw["v"], _tile(n, "peer_v"), ib)


def kernel(x_prompt, x_sample, cache_k, cache_v, cache_logf, state_hg, cache_mem_k, cache_mem_v, page_table, mem_prompt, ln_mix, w_in, b_fox_f, hg_lb, fox_gn, hg_gn, w_out, ln_x, ln_mem, w_xq, w_xk, w_xv, w_xo, ln_ffn, peer_wq, peer_subkeys, peer_u, peer_v, ln_final):
    depth, d_model, _ = w_in.shape
    bp, seq, _ = x_prompt.shape
    bs = x_sample.shape[0]
    fox_heads = b_fox_f.shape[1]
    fox_w = fox_gn.shape[1]
    hg_w = hg_gn.shape[1]
    hg_heads = hg_w // HG_DK
    n_phys, page = cache_k.shape[1], cache_k.shape[2]
    n_mem = mem_prompt.shape[1]
    x_w = w_xq.shape[2]
    x_heads = x_w // X_HEAD_DIM
    p_heads = peer_subkeys.shape[1]
    p_keys, p_half = peer_subkeys.shape[3], peer_subkeys.shape[4]
    pairs = fox_w // LANES

    lb_all = jnp.cumsum(jax.nn.softmax(hg_lb.astype(F32), axis=0), axis=0)
    row = lambda a: a.reshape(1, -1).astype(F32)

    xp = x_prompt.reshape(bp * seq, d_model)
    xs = x_sample.reshape(bs, d_model)
    mem = mem_prompt.reshape(bp * n_mem, d_model)
    ck_t = cache_k.transpose(0, 1, 3, 4, 2).reshape(depth, n_phys, fox_w, page)
    cv_t = cache_v.transpose(0, 1, 3, 4, 2).reshape(depth, n_phys, fox_w, page)
    clf_t = cache_logf.transpose(0, 1, 3, 2)
    state_t = state_hg.transpose(0, 2, 3, 4, 1)
    memk = cache_mem_k.reshape(depth, bs, n_mem * x_heads, X_HEAD_DIM)
    memv = cache_mem_v.reshape(depth, bs, n_mem * x_heads, X_HEAD_DIM)

    outs = {k: [] for k in ("kp", "vp", "fp", "hp", "mkp", "mvp", "ks", "vs", "fs", "hs")}
    for l in range(depth):
        wl = w_in[l]
        c0 = 3 * fox_w
        lw = dict(
            w_main=jnp.concatenate([wl[:, :c0], wl[:, c0 + fox_heads:]], axis=1).astype(BF16),
            w_ff=jnp.pad(wl[:, c0:c0 + fox_heads], ((0, 0), (0, LANES - fox_heads))).astype(BF16),
            b_f=jnp.pad(row(b_fox_f[l]), ((0, 0), (0, LANES - fox_heads))),
            lb=row(lb_all[l] - lb_all[0]),
            subkeys=peer_subkeys[l].reshape(p_heads * 2, p_keys, p_half).astype(BF16),
            ut=peer_u[l].astype(BF16).T, v=peer_v[l].astype(BF16),
        )
        w_out_b, w_xq_b, w_xo_b = w_out[l].astype(BF16), w_xq[l].astype(BF16), w_xo[l].astype(BF16)
        peer_wq_b = peer_wq[l].astype(BF16)

        tm = _tile(bp * seq, "tokenwise")
        fq, fk, fv, flf, hq, hk, hv, hlf, hgate = _inproj(xp, row(ln_mix[l]), lw["w_main"], lw["w_ff"], lw["b_f"],
                                                           lw["lb"], tm, fox_heads)
        lf_t = flf.reshape(bp, seq, fox_heads).transpose(0, 2, 1).reshape(bp * fox_heads, seq)
        c_t = _cumsum_lanes(lf_t, _tile(seq, "cumsum"))
        ck = c_t.reshape(bp * pairs, 2, seq)
        cq = c_t.reshape(bp, pairs, 2, seq).transpose(1, 0, 3, 2).reshape(pairs, bp * seq, 2)
        fo = _fox_prompt(fq, fk, fv, cq, ck, bp, _tile(seq, "fox"))
        sq, sk, sv, slf, tq, tk, tv, tlf, tgate = _inproj(xs, row(ln_mix[l]), lw["w_main"], lw["w_ff"], lw["b_f"],
                                                           lw["lb"], bs, fox_heads)
        colm = lambda a: a.reshape(bs, a.shape[1], 1)
        decode_args = (page_table, colm(sq), colm(sk), colm(sv), colm(slf), ck_t, cv_t, clf_t, l)
        if seq // HG_CHUNK == bs:
            ho, st, fo_s = _hgrn_prompt_fox_decode(hq, hk, hv, hlf, bp, *decode_args)
        else:
            ho, st = _hgrn_prompt(hq, hk, hv, hlf, bp)
            fo_s = _fox_decode(*decode_args)
        fo_s = fo_s.reshape(bs, fox_w)
        x1, qx = _merge(fo, ho, hgate, xp, row(fox_gn[l]), row(hg_gn[l]), w_out_b, row(ln_x[l]), w_xq_b, tm)
        mk, mv = _memkv(mem, row(ln_mem[l]), w_xk[l].astype(BF16), w_xv[l].astype(BF16), n_mem)
        ctx = _xattn_prompt(qx, mk, mv, bp, tm)
        x2, h3, pq = _xo_peerq(x1, ctx, w_xo_b, row(ln_ffn[l]), peer_wq_b, tm)
        xp = _peer(x2, h3, pq, lw)
        outs["kp"].append(fk.reshape(bp, seq, fox_heads, HEAD_DIM))
        outs["vp"].append(fv.reshape(bp, seq, fox_heads, HEAD_DIM))
        outs["fp"].append(flf.reshape(bp, seq, fox_heads))
        st5 = st.reshape(bp, pairs, 2, HG_DK, 2, HG_DK)
        s_heads = jnp.stack([st5[:, :, 0, :, 0, :], st5[:, :, 1, :, 1, :]], axis=2)
        outs["hp"].append(s_heads.reshape(bp, hg_heads, HG_DK, HG_DK).transpose(0, 1, 3, 2))
        outs["mkp"].append(mk.reshape(bp, n_mem, x_heads, X_HEAD_DIM))
        outs["mvp"].append(mv.reshape(bp, n_mem, x_heads, X_HEAD_DIM))

        lanes_r = lambda a: a.T.reshape(hg_heads, HG_DK, 1, bs)
        ho_t, s_new = _hgrn_step(lanes_r(tq), lanes_r(tk), lanes_r(tlf), tv.T.reshape(hg_heads, HG_DK, bs),
                                 state_t, l)
        x1s, qxs = _merge(fo_s, ho_t.reshape(hg_w, bs).T, tgate, xs, row(fox_gn[l]), row(hg_gn[l]), w_out_b,
                          row(ln_x[l]), w_xq_b, bs)
        qxs_t = jnp.pad(qxs.reshape(bs, x_heads, X_HEAD_DIM).transpose(0, 2, 1), ((0, 0), (0, 0), (0, 8 - x_heads)))
        ctx_s = _xattn_sample(qxs_t, memk, memv, l, x_heads, _tile(bs, "xattn_sample")).reshape(bs, x_w)
        x2s, h3s, pqs = _xo_peerq(x1s, ctx_s, w_xo_b, row(ln_ffn[l]), peer_wq_b, bs)
        xs = _peer(x2s, h3s, pqs, lw)
        outs["ks"].append(sk.reshape(bs, 1, fox_heads, HEAD_DIM))
        outs["vs"].append(sv.reshape(bs, 1, fox_heads, HEAD_DIM))
        outs["fs"].append(slf.reshape(bs, 1, fox_heads))
        outs["hs"].append(s_new)

    y_prompt = _final_norm(xp, row(ln_final), _tile(bp * seq, "tokenwise")).reshape(bp, seq, d_model)
    y_sample = _final_norm(xs, row(ln_final), bs).reshape(bs, 1, d_model)
    st_ = lambda k: jnp.stack(outs[k])
    return (y_prompt, y_sample, st_("kp"), st_("vp"), st_("fp"), st_("hp"), st_("mkp"), st_("mvp"),
            st_("ks"), st_("vs"), st_("fs"), st_("hs").transpose(0, 4, 1, 2, 3))
```

```python
import functools
import math

import jax
import jax.numpy as jnp
from jax import lax
from jax.experimental import pallas as pl
from jax.experimental.pallas import tpu as pltpu

F32 = jnp.float32
BF16 = jnp.bfloat16
I32 = jnp.int32
EPS = 1e-6
NEG_INF = float("-inf")

HEAD_DIM = 64
HG_DK = 64
HG_CHUNK = 64
HG_SUB = 16
X_HEAD_DIM = 128
PEER_TOPK = 16
LANES = 128
V7X_VMEM_BYTES = 64 * 1024 * 1024
VMEM_LIMIT = V7X_VMEM_BYTES - 8 * 1024 * 1024

TILE = dict(
    tokenwise=512,
    tokenwise_wide=1024,
    cumsum=512,
    fox=1024,
    route=1024,
    peer_groups=8,
    peer_u=1024,
    peer_v=512,
    peer_blocks=16,
    xattn_sample=8,
)

_NT = (((1,), (1,)), ((), ()))
_TN = (((0,), (0,)), ((), ()))


def _params(*sem):
    return pltpu.CompilerParams(dimension_semantics=sem, vmem_limit_bytes=VMEM_LIMIT)


def _rms(x, g):
    ms = jnp.mean(x * x, axis=-1, keepdims=True)
    return x * lax.rsqrt(ms + EPS) * g


def _log_sigmoid(x):
    return jnp.minimum(x, 0.0) - jnp.log1p(jnp.exp(-jnp.abs(x)))


def _dot(a, b):
    return jnp.dot(a, b, preferred_element_type=F32)


def _split3(x):
    hi = x.astype(BF16)
    r = x - hi.astype(F32)
    mid = r.astype(BF16)
    lo = (r - mid.astype(F32)).astype(BF16)
    return hi, mid, lo


def _prefix_rows(tril, x):
    return _dot(jnp.concatenate([tril] * 3, axis=1), jnp.concatenate(_split3(x), axis=0))


def _prefix_lanes(x, triu):
    return _dot(jnp.concatenate(_split3(x), axis=1), jnp.concatenate([triu] * 3, axis=0))


def _iota(shape, axis):
    return lax.broadcasted_iota(I32, shape, axis)


def _inproj_kernel(x_ref, g_ref, w_ref, wff_ref, bf_ref, lb_ref,
                   fq_ref, fk_ref, fv_ref, lf_ref, hq_ref, hk_ref, hv_ref, hlf_ref, gate_ref, *, fox_heads):
    h = _rms(x_ref[...], g_ref[...]).astype(BF16)
    w = w_ref.shape[1] // 7

    def mm(j):
        return _dot(h, w_ref[:, j * w:(j + 1) * w])

    fq_ref[...] = mm(0)
    fk_ref[...] = mm(1)
    fv_ref[...] = mm(2)
    ff = _dot(h, wff_ref[...])
    lf_ref[...] = _log_sigmoid(ff + bf_ref[...])[:, :fox_heads]
    hq_ref[...] = mm(3)
    z = mm(4)
    lb = lb_ref[...]
    a = jnp.log(lb)
    b = jnp.log1p(-lb) + _log_sigmoid(z)
    hlf_ref[...] = jnp.maximum(a, b) + jnp.log1p(jnp.exp(-jnp.abs(a - b)))
    hk_ref[...] = (1.0 - lb) * (1.0 / (1.0 + jnp.exp(z)))
    hv_ref[...] = mm(5)
    gate_ref[...] = mm(6)


def _inproj(x, g, w_main, w_ff, b_f, lb, tm, fox_heads):
    n, d = x.shape
    w = w_main.shape[1] // 7
    full = lambda a: pl.BlockSpec(a.shape, lambda i: (0,) * a.ndim)
    row = lambda c: pl.BlockSpec((tm, c), lambda i: (i, 0))
    outs = [jax.ShapeDtypeStruct((n, w), F32)] * 3 + [jax.ShapeDtypeStruct((n, fox_heads), F32)] + \
           [jax.ShapeDtypeStruct((n, w), F32)] * 5
    out_specs = [row(w)] * 3 + [row(fox_heads)] + [row(w)] * 5
    return pl.pallas_call(
        functools.partial(_inproj_kernel, fox_heads=fox_heads),
        grid=(n // tm,),
        in_specs=[row(d), full(g), full(w_main), full(w_ff), full(b_f), full(lb)],
        out_specs=out_specs, out_shape=outs,
        compiler_params=_params("parallel"), name="inproj",
    )(x, g, w_main, w_ff, b_f, lb)


def _cumsum_kernel(x_ref, o_ref, carry_ref):
    @pl.when(pl.program_id(0) == 0)
    def _():
        carry_ref[...] = jnp.zeros_like(carry_ref)

    x = x_ref[...]
    n = x.shape[1]
    tri = (_iota((n, n), 0) <= _iota((n, n), 1)).astype(BF16)
    c = _prefix_lanes(x, tri) + carry_ref[:, :1]
    o_ref[...] = c
    carry_ref[...] = jnp.broadcast_to(c[:, n - 1:n], carry_ref.shape)


def _cumsum_lanes(x, tc):
    r, t = x.shape
    return pl.pallas_call(
        _cumsum_kernel, grid=(t // tc,),
        in_specs=[pl.BlockSpec((r, tc), lambda i: (0, i))],
        out_specs=pl.BlockSpec((r, tc), lambda i: (0, i)),
        out_shape=jax.ShapeDtypeStruct((r, t), F32),
        scratch_shapes=[pltpu.VMEM((r, LANES), F32)],
        compiler_params=_params("arbitrary"), name="logf_cumsum",
    )(x)


def _fox_kernel(qt_ref, kt_ref, q_ref, k_ref, v_ref, cq_ref, ck_ref, o_ref, m_ref, cqr_ref, acc_ref, qh_ref, *,
                scale):
    qi = qt_ref[pl.program_id(2)]
    ki = kt_ref[pl.program_id(2)]
    tq = q_ref.shape[0]
    tk = k_ref.shape[0]
    lane = _iota((1, LANES), 1)
    log2e = math.log2(math.e)

    @pl.when(ki == 0)
    def _():
        m_ref[...] = jnp.full_like(m_ref, NEG_INF)
        acc_ref[...] = jnp.zeros_like(acc_ref)
        q = q_ref[...] * (scale * log2e)
        for hh in range(2):
            cqr_ref[hh] = jnp.broadcast_to(cq_ref[:, hh:hh + 1] * log2e, (tq, LANES))
            qh_ref[hh] = jnp.where((lane // HEAD_DIM) == hh, q, 0.0).astype(BF16)

    def step(masked, blocks):
        k = k_ref[...].astype(BF16)
        v = v_ref[...]
        ck = ck_ref[...] * log2e
        for hh in range(2):
            vh = jnp.where((lane // HEAD_DIM) == hh, v, 1.0).astype(BF16)
            for r0, r1, nk in blocks:
                s = lax.dot_general(qh_ref[hh, r0:r1], k[:nk], _NT, preferred_element_type=F32) - ck[hh:hh + 1, :nk]
                if masked:
                    causal = (ki * tk + _iota((1, nk), 1)) <= (qi * tq + r0 + _iota((r1 - r0, 1), 0))
                    s = jnp.where(causal, s, NEG_INF)
                m_prev = m_ref[hh, r0:r1]
                cq = cqr_ref[hh, r0:r1]
                m_new = jnp.maximum(m_prev, jnp.max(s, axis=1, keepdims=True) + cq)
                p = jnp.exp2(s - jnp.tile(m_new - cq, (1, nk // LANES)))
                acc_ref[hh, r0:r1] = jnp.exp2(m_prev - m_new) * acc_ref[hh, r0:r1] + _dot(p.astype(BF16), vh[:nk])
                m_ref[hh, r0:r1] = m_new

    @pl.when(ki < qi)
    def _():
        step(False, [(0, tq, tk)])

    @pl.when(ki == qi)
    def _():
        half = tq // 2
        step(True, [(0, half, tk // 2), (half, tq, tk)] if half % LANES == 0 else [(0, tq, tk)])
        a0 = acc_ref[0]
        a1 = acc_ref[1]
        o0 = a0 / pltpu.roll(a0, HEAD_DIM, axis=1)
        o1 = a1 / pltpu.roll(a1, HEAD_DIM, axis=1)
        o_ref[...] = jnp.where(lane < HEAD_DIM, o0, o1)


def _fox_prompt(q, k, v, cq, ck, batch, tq):
    n, w = q.shape
    t = n // batch
    pairs = w // LANES
    nq = t // tq
    kern = functools.partial(_fox_kernel, scale=HEAD_DIM ** -0.5)
    steps = [(i, j) for i in range(nq) for j in range(i + 1)]
    qi_tab = jnp.asarray([s[0] for s in steps], I32)
    ki_tab = jnp.asarray([s[1] for s in steps], I32)
    grid_spec = pltpu.PrefetchScalarGridSpec(
        num_scalar_prefetch=2, grid=(batch, pairs, len(steps)),
        in_specs=[
            pl.BlockSpec((tq, LANES), lambda b, g, s, qt, kt: (b * nq + qt[s], g)),
            pl.BlockSpec((tq, LANES), lambda b, g, s, qt, kt: (b * nq + kt[s], g)),
            pl.BlockSpec((tq, LANES), lambda b, g, s, qt, kt: (b * nq + kt[s], g)),
            pl.BlockSpec((None, tq, 2), lambda b, g, s, qt, kt: (g, b * nq + qt[s], 0)),
            pl.BlockSpec((None, 2, tq), lambda b, g, s, qt, kt: (b * pairs + g, 0, kt[s])),
        ],
        out_specs=pl.BlockSpec((tq, LANES), lambda b, g, s, qt, kt: (b * nq + qt[s], g)),
        scratch_shapes=[pltpu.VMEM((2, tq, LANES), F32)] * 3 + [pltpu.VMEM((2, tq, LANES), BF16)],
    )
    return pl.pallas_call(
        kern, grid_spec=grid_spec, out_shape=jax.ShapeDtypeStruct((n, w), F32),
        compiler_params=_params("parallel", "parallel", "arbitrary"), name="fox_prompt",
    )(qi_tab, ki_tab, q, k, v, cq, ck)


def _hgrn_kernel(q_ref, k_ref, v_ref, lf_ref, o_ref, st_ref, st_sc, *, also=()):
    n = pl.program_id(0)
    nb, c, w = q_ref.shape
    pairs = w // LANES

    @pl.when(n == 0)
    def _():
        st_sc[...] = jnp.zeros_like(st_sc)

    tril = (_iota((c, c), 1) <= _iota((c, c), 0)).astype(BF16)
    same_head = (_iota((LANES, LANES), 0) // HG_DK) == (_iota((LANES, LANES), 1) // HG_DK)
    seg = same_head.astype(BF16)
    sc = HG_SUB
    ti = _iota((sc, sc, LANES), 0)
    si = _iota((sc, sc, LANES), 1)
    first_head = _iota((1, LANES), 1) < HG_DK
    for bg in range(nb * pairs):
        r, g = divmod(bg, pairs)
        sl = slice(g * LANES, (g + 1) * LANES)
        q = q_ref[r, :, sl]
        k = k_ref[r, :, sl]
        v = v_ref[r, :, sl]
        vb = v.astype(BF16)
        b = _prefix_rows(tril, lf_ref[r, :, sl])
        st = st_sc[r, g]
        qe = (q * jnp.exp(b)).astype(BF16)
        o = lax.dot_general(qe, st.astype(BF16), _NT, preferred_element_type=F32)
        rows = []
        for i in range(c // sc):
            lo, hi = i * sc, (i + 1) * sc
            qi, ki, bi = q[lo:hi], k[lo:hi], b[lo:hi]
            decay = jnp.exp(jnp.where(si <= ti, bi[:, None, :] - bi[None, :, :], NEG_INF))
            p = (qi[:, None, :] * ki[None, :, :]) * decay
            a = _dot(p.reshape(sc * sc, LANES).astype(BF16), seg)
            oi = jnp.sum(a.reshape(sc, sc, LANES) * v[None, lo:hi, :], axis=1)
            if i > 0:
                br = b[lo - 1:lo, :]
                qt = qi * jnp.exp(bi - br)
                q2 = jnp.concatenate([jnp.where(first_head, qt, 0.0), jnp.where(first_head, 0.0, qt)], axis=0)
                ks = (k[:lo] * jnp.exp(br - b[:lo])).astype(BF16)
                a2 = lax.dot_general(q2.astype(BF16), ks, _NT, preferred_element_type=F32)
                o2 = _dot(a2.astype(BF16), vb[:lo])
                oi = oi + jnp.where(first_head, o2[:sc], o2[sc:])
            rows.append(oi)
        o_ref[r, :, sl] = o + jnp.concatenate(rows, axis=0)
        b_last = b[c - 1:c, :]
        kd = (k * jnp.exp(b_last - b)).astype(BF16)
        upd = lax.dot_general(vb, kd, _TN, preferred_element_type=F32)
        st_sc[r, g] = st * jnp.exp(b_last) + jnp.where(same_head, upd, 0.0)
        share = -(-len(also) // (nb * pairs))
        for thunk in also[bg * share:(bg + 1) * share]:
            thunk()

    @pl.when(n == pl.num_programs(0) - 1)
    def _():
        st_ref[...] = st_sc[...]


def _hgrn_prompt(q, k, v, lf, batch):
    n, w = q.shape
    t = n // batch
    pairs = w // LANES
    rows3 = lambda a: a.reshape(batch, t, w)
    blk = pl.BlockSpec((batch, HG_CHUNK, w), lambda i: (0, i, 0))
    st_shape = (batch, pairs, LANES, LANES)
    o, st = pl.pallas_call(
        _hgrn_kernel, grid=(t // HG_CHUNK,),
        in_specs=[blk] * 4,
        out_specs=[blk, pl.BlockSpec(st_shape, lambda i: (0, 0, 0, 0))],
        out_shape=[jax.ShapeDtypeStruct((batch, t, w), F32), jax.ShapeDtypeStruct(st_shape, F32)],
        scratch_shapes=[pltpu.VMEM(st_shape, F32)],
        compiler_params=_params("arbitrary"), name="hgrn_prompt",
    )(rows3(q), rows3(k), rows3(v), rows3(lf))
    return o.reshape(n, w), st


def _fox_decode_stages(pt_ref, q_ref, kn_ref, vn_ref, lfn_ref, *refs, scale, n_pages):
    k_refs = refs[:n_pages]
    v_refs = refs[n_pages:2 * n_pages]
    lf_refs = refs[2 * n_pages:3 * n_pages]
    o_ref = refs[3 * n_pages]
    w, ps = k_refs[0].shape
    nh = lf_refs[0].shape[0]
    hd = w // nh
    heads = lambda x: x.reshape(nh, hd, x.shape[1])
    st = dict(scores=[], carry=jnp.zeros((nh, 1), F32), acc=jnp.zeros((w, ps), F32))

    def setup():
        st["q"] = q_ref[...] * scale
        st["qb"] = jnp.broadcast_to(st["q"], (w, ps))
        st["triu"] = (_iota((ps, ps), 0) <= _iota((ps, ps), 1)).astype(BF16)

    def score(p):
        s = jnp.sum(heads(k_refs[p][...] * st["qb"]), axis=1)
        within = _prefix_lanes(lf_refs[p][...], st["triu"])
        st["scores"].append(s - (within + st["carry"]))
        st["carry"] = st["carry"] + within[:, ps - 1:ps]

    def softmax():
        scores = st["scores"]
        s_new = jnp.sum(heads(kn_ref[...] * st["q"]), axis=1) - (st["carry"] + lfn_ref[...])
        m_tile = scores[0]
        for s in scores[1:]:
            m_tile = jnp.maximum(m_tile, s)
        m = jnp.maximum(s_new, jnp.max(m_tile, axis=1, keepdims=True))
        st["e_new"] = jnp.exp(s_new - m)
        st["exps"] = [jnp.exp(s - m) for s in scores]
        l_tile = st["exps"][0]
        for e in st["exps"][1:]:
            l_tile = l_tile + e
        st["inv"] = 1.0 / (st["e_new"] + jnp.sum(l_tile, axis=1, keepdims=True))

    def value(p):
        pb = jnp.broadcast_to(st["exps"][p][:, None, :], (nh, hd, ps)).reshape(w, ps)
        st["acc"] = st["acc"] + v_refs[p][...] * pb

    def finish():
        rep = lambda x: jnp.broadcast_to(x[:, None, :], (nh, hd, 1)).reshape(w, 1)
        o_ref[...] = (jnp.sum(st["acc"], axis=1, keepdims=True) + rep(st["e_new"]) * vn_ref[...]) * rep(st["inv"])

    part = functools.partial
    return ([setup] + [part(score, p) for p in range(n_pages)] + [softmax]
            + [part(value, p) for p in range(n_pages)] + [finish])


def _fox_decode_kernel(*refs, scale, n_pages):
    for stage in _fox_decode_stages(*refs, scale=scale, n_pages=n_pages):
        stage()


def _fox_decode(page_table, q, k_new, v_new, lf_new, cache_kt, cache_vt, cache_lft, layer):
    r, w, _ = q.shape
    nh = lf_new.shape[1]
    n_pages = page_table.shape[1]
    ps = cache_kt.shape[3]
    pt = page_table.reshape(-1)
    tok = lambda c: pl.BlockSpec((None, c, 1), lambda i, pt: (i, 0, 0))

    def page(p, rows):
        return pl.BlockSpec((None, None, rows, ps), lambda i, pt: (layer, pt[i * n_pages + p], 0, 0))

    pages = lambda rows: [page(p, rows) for p in range(n_pages)]
    grid_spec = pltpu.PrefetchScalarGridSpec(
        num_scalar_prefetch=1, grid=(r,),
        in_specs=[tok(w), tok(w), tok(w), tok(nh)] + pages(w) + pages(w) + pages(nh),
        out_specs=tok(w),
    )
    return pl.pallas_call(
        functools.partial(_fox_decode_kernel, scale=HEAD_DIM ** -0.5, n_pages=n_pages),
        grid_spec=grid_spec, out_shape=jax.ShapeDtypeStruct((r, w, 1), F32),
        compiler_params=_params("parallel"), name="fox_decode",
    )(pt, q, k_new, v_new, lf_new, *([cache_kt] * n_pages), *([cache_vt] * n_pages), *([cache_lft] * n_pages))


def _hgrn_decode_kernel(pt_ref, hq_ref, hk_ref, hv_ref, hlf_ref, q_ref, kn_ref, vn_ref, lfn_ref, *refs, scale,
                        n_pages):
    pages = refs[:3 * n_pages]
    ho_ref, st_ref, od_ref, st_sc = refs[3 * n_pages:]
    stages = _fox_decode_stages(pt_ref, q_ref, kn_ref, vn_ref, lfn_ref, *pages, od_ref, scale=scale, n_pages=n_pages)
    _hgrn_kernel(hq_ref, hk_ref, hv_ref, hlf_ref, ho_ref, st_ref, st_sc, also=stages)


def _hgrn_prompt_fox_decode(hq, hk, hv, hlf, batch, page_table, q, k_new, v_new, lf_new, cache_kt, cache_vt,
                            cache_lft, layer):
    n, w = hq.shape
    t = n // batch
    pairs = w // LANES
    r, wd, _ = q.shape
    nh = lf_new.shape[1]
    n_pages = page_table.shape[1]
    ps = cache_kt.shape[3]
    assert t // HG_CHUNK == r
    rows3 = lambda a: a.reshape(batch, t, w)
    blk = pl.BlockSpec((batch, HG_CHUNK, w), lambda i, pt: (0, i, 0))
    st_shape = (batch, pairs, LANES, LANES)
    tok = lambda c: pl.BlockSpec((None, c, 1), lambda i, pt: (i, 0, 0))

    def page(p, rows):
        return pl.BlockSpec((None, None, rows, ps), lambda i, pt: (layer, pt[i * n_pages + p], 0, 0))

    pages = lambda rows: [page(p, rows) for p in range(n_pages)]
    grid_spec = pltpu.PrefetchScalarGridSpec(
        num_scalar_prefetch=1, grid=(r,),
        in_specs=[blk] * 4 + [tok(wd), tok(wd), tok(wd), tok(nh)] + pages(wd) + pages(wd) + pages(nh),
        out_specs=[blk, pl.BlockSpec(st_shape, lambda i, pt: (0, 0, 0, 0)), tok(wd)],
        scratch_shapes=[pltpu.VMEM(st_shape, F32)],
    )
    o, st, od = pl.pallas_call(
        functools.partial(_hgrn_decode_kernel, scale=HEAD_DIM ** -0.5, n_pages=n_pages),
        grid_spec=grid_spec,
        out_shape=[jax.ShapeDtypeStruct((batch, t, w), F32), jax.ShapeDtypeStruct(st_shape, F32),
                   jax.ShapeDtypeStruct((r, wd, 1), F32)],
        compiler_params=_params("arbitrary"), name="hgrn_prompt_fox_decode",
    )(page_table.reshape(-1), rows3(hq), rows3(hk), rows3(hv), rows3(hlf), q, k_new, v_new, lf_new,
      *([cache_kt] * n_pages), *([cache_vt] * n_pages), *([cache_lft] * n_pages))
    return o.reshape(n, w), st, od


def _hgrn_step_kernel(q_ref, k_ref, lf_ref, v_ref, s_ref, o_ref, so_ref):
    q = q_ref[...]
    k = k_ref[...]
    f = jnp.exp(lf_ref[...])
    v = v_ref[...]
    s = s_ref[...]
    so_ref[...] = f * s + k * v[None]
    o_ref[...] = jnp.sum((q * f) * s, axis=0) + jnp.sum(q * k, axis=0) * v


def _hgrn_step(q, k, lf, v, state_t, layer):
    h, dk, _, r = q.shape
    dv = v.shape[1]
    col = pl.BlockSpec((None, dk, 1, r), lambda i: (i, 0, 0, 0))
    rowv = pl.BlockSpec((None, dv, r), lambda i: (i, 0, 0))
    return pl.pallas_call(
        _hgrn_step_kernel, grid=(h,),
        in_specs=[col, col, col, rowv, pl.BlockSpec((None, None, dk, dv, r), lambda i: (layer, i, 0, 0, 0))],
        out_specs=[rowv, pl.BlockSpec((None, dk, dv, r), lambda i: (i, 0, 0, 0))],
        out_shape=[jax.ShapeDtypeStruct((h, dv, r), F32), jax.ShapeDtypeStruct((h, dk, dv, r), F32)],
        compiler_params=_params("parallel"), name="hgrn_step",
    )(q, k, lf, v, state_t)


def _head_mean_sq(y, width):
    w = y.shape[1]
    seg = ((_iota((w, w), 0) // width) == (_iota((w, w), 1) // width)).astype(BF16)
    sq = y * y
    hi = sq.astype(BF16)
    lo = (sq - hi.astype(F32)).astype(BF16)
    return (_dot(hi, seg) + _dot(lo, seg)) * (1.0 / width)


def _merge_kernel(fo_ref, ho_ref, gate_ref, x_ref, fgn_ref, hgn_ref, wo_ref, lnx_ref, wxq_ref, x1_ref, qx_ref):
    fo = fo_ref[...]
    ho = ho_ref[...]
    fw = fo.shape[1]
    fn = fo * lax.rsqrt(_head_mean_sq(fo, HEAD_DIM) + EPS) * fgn_ref[...]
    gate = gate_ref[...]
    hn = ho * lax.rsqrt(_head_mean_sq(ho, HEAD_DIM) + EPS) * hgn_ref[...] * (gate / (1.0 + jnp.exp(-gate)))
    y = _dot(fn.astype(BF16), wo_ref[:fw, :]) + _dot(hn.astype(BF16), wo_ref[fw:, :])
    x1 = x_ref[...] + y
    x1_ref[...] = x1
    qx_ref[...] = _dot(_rms(x1, lnx_ref[...]).astype(BF16), wxq_ref[...])


def _merge(fo, ho, gate, x, fgn, hgn, w_out, ln_x, w_xq, tm):
    n, d = x.shape
    xw = w_xq.shape[1]
    full = lambda a: pl.BlockSpec(a.shape, lambda i: (0,) * a.ndim)
    row = lambda c: pl.BlockSpec((tm, c), lambda i: (i, 0))
    return pl.pallas_call(
        _merge_kernel, grid=(n // tm,),
        in_specs=[row(fo.shape[1]), row(ho.shape[1]), row(gate.shape[1]), row(d),
                  full(fgn), full(hgn), full(w_out), full(ln_x), full(w_xq)],
        out_specs=[row(d), row(xw)],
        out_shape=[jax.ShapeDtypeStruct((n, d), F32), jax.ShapeDtypeStruct((n, xw), F32)],
        compiler_params=_params("parallel"), name="merge",
    )(fo, ho, gate, x, fgn, hgn, w_out, ln_x, w_xq)


def _memkv_kernel(m_ref, g_ref, wk_ref, wv_ref, k_ref, v_ref):
    m = _rms(m_ref[...], g_ref[...]).astype(BF16)
    k_ref[...] = _dot(m, wk_ref[...])
    v_ref[...] = _dot(m, wv_ref[...])


def _memkv(mem, g, wk, wv, tm):
    n, d = mem.shape
    xw = wk.shape[1]
    full = lambda a: pl.BlockSpec(a.shape, lambda i: (0,) * a.ndim)
    row = lambda c: pl.BlockSpec((tm, c), lambda i: (i, 0))
    return pl.pallas_call(
        _memkv_kernel, grid=(n // tm,),
        in_specs=[row(d), full(g), full(wk), full(wv)],
        out_specs=[row(xw), row(xw)],
        out_shape=[jax.ShapeDtypeStruct((n, xw), F32)] * 2,
        compiler_params=_params("parallel"), name="memkv",
    )(mem, g, wk, wv)


def _xattn_prompt_kernel(q_ref, mk_ref, mv_ref, o_ref, *, scale):
    heads = q_ref.shape[1] // X_HEAD_DIM
    for h in range(heads):
        sl = slice(h * X_HEAD_DIM, (h + 1) * X_HEAD_DIM)
        q = (q_ref[:, sl] * scale).astype(BF16)
        s = lax.dot_general(q, mk_ref[:, sl].astype(BF16), _NT, preferred_element_type=F32)
        e = jnp.exp(s - jnp.max(s, axis=1, keepdims=True))
        p = e / jnp.sum(e, axis=1, keepdims=True)
        o_ref[:, sl] = _dot(p.astype(BF16), mv_ref[:, sl].astype(BF16))


def _xattn_prompt(q, mk, mv, batch, tm):
    n, xw = q.shape
    t = n // batch
    nm = mk.shape[0] // batch
    nt = t // tm
    return pl.pallas_call(
        functools.partial(_xattn_prompt_kernel, scale=X_HEAD_DIM ** -0.5), grid=(batch, nt),
        in_specs=[pl.BlockSpec((tm, xw), lambda b, i: (b * nt + i, 0)),
                  pl.BlockSpec((nm, xw), lambda b, i: (b, 0)),
                  pl.BlockSpec((nm, xw), lambda b, i: (b, 0))],
        out_specs=pl.BlockSpec((tm, xw), lambda b, i: (b * nt + i, 0)),
        out_shape=jax.ShapeDtypeStruct((n, xw), F32),
        compiler_params=_params("parallel", "parallel"), name="xattn_prompt",
    )(q, mk, mv)


def _xattn_sample_kernel(qt_ref, mk_ref, mv_ref, o_ref, *, scale, heads):
    rb, hd, cols = qt_ref.shape
    nm = mk_ref.shape[1] // heads
    col = _iota((hd, cols), 1)
    erow = _iota((cols, hd), 0)
    valid = _iota((1, cols), 1) < heads
    for r in range(rb):
        qt = qt_ref[r] * scale
        s = jnp.zeros((nm, cols), F32)
        for h in range(heads):
            kh = mk_ref[r, pl.ds(h, nm, stride=heads), :].astype(BF16)
            s = s + _dot(kh, jnp.where(col == h, qt, 0.0).astype(BF16))
        e = jnp.exp(s - jnp.max(s, axis=0, keepdims=True))
        p = jnp.where(valid, e / jnp.sum(e, axis=0, keepdims=True), 0.0).astype(BF16)
        outs = []
        for h in range(heads):
            pe = _dot(p, (erow == h).astype(BF16))
            vh = mv_ref[r, pl.ds(h, nm, stride=heads), :]
            outs.append(jnp.sum((pe * vh).reshape(nm // 8, 8, hd), axis=0))
        o_ref[r] = jnp.sum(jnp.concatenate(outs, axis=1), axis=0, keepdims=True)


def _xattn_sample(q_t, mem_k, mem_v, layer, heads, rb):
    r, hd, cols = q_t.shape
    rows = mem_k.shape[2]
    mem = pl.BlockSpec((None, rb, rows, hd), lambda i: (layer, i, 0, 0))
    return pl.pallas_call(
        functools.partial(_xattn_sample_kernel, scale=X_HEAD_DIM ** -0.5, heads=heads), grid=(r // rb,),
        in_specs=[pl.BlockSpec((rb, hd, cols), lambda i: (i, 0, 0)), mem, mem],
        out_specs=pl.BlockSpec((rb, 1, heads * hd), lambda i: (i, 0, 0)),
        out_shape=jax.ShapeDtypeStruct((r, 1, heads * hd), F32),
        compiler_params=_params("parallel"), name="xattn_sample",
    )(q_t, mem_k, mem_v)


def _xo_peerq_kernel(x1_ref, ctx_ref, wxo_ref, lnf_ref, wq_ref, x2_ref, h_ref, pq_ref):
    x2 = x1_ref[...] + _dot(ctx_ref[...].astype(BF16), wxo_ref[...])
    x2_ref[...] = x2
    h = _rms(x2, lnf_ref[...]).astype(BF16)
    h_ref[...] = h
    pq_ref[...] = _dot(h, wq_ref[...]).astype(BF16)


def _xo_peerq(x1, ctx, w_xo, ln_ffn, peer_wq, tm):
    n, d = x1.shape
    qw = peer_wq.shape[1]
    full = lambda a: pl.BlockSpec(a.shape, lambda i: (0,) * a.ndim)
    row = lambda c: pl.BlockSpec((tm, c), lambda i: (i, 0))
    return pl.pallas_call(
        _xo_peerq_kernel, grid=(n // tm,),
        in_specs=[row(d), row(ctx.shape[1]), full(w_xo), full(ln_ffn), full(peer_wq)],
        out_specs=[row(d), row(d), row(qw)],
        out_shape=[jax.ShapeDtypeStruct((n, d), F32), jax.ShapeDtypeStruct((n, d), BF16),
                   jax.ShapeDtypeStruct((n, qw), BF16)],
        compiler_params=_params("parallel"), name="xo_peerq",
    )(x1, ctx, w_xo, ln_ffn, peer_wq)


def _top16(s):
    kk = s.shape[0]
    kio = _iota(s.shape, 0).astype(F32)
    vals, idxs = [], []
    for _ in range(PEER_TOPK):
        m = jnp.max(s, axis=0, keepdims=True)
        idx = jnp.min(jnp.where(s == m, kio, float(kk)), axis=0, keepdims=True)
        s = jnp.where(kio == idx, NEG_INF, s)
        vals.append(m)
        idxs.append(idx)
    return jnp.concatenate(vals, axis=0), jnp.concatenate(idxs, axis=0).astype(I32)


def _gather16(x, idx):
    lo, hi = x[:8], x[8:]
    out = []
    for part in (idx[:8], idx[8:]):
        low3 = part & 7
        out.append(jnp.where(part < 8, jnp.take_along_axis(lo, low3, axis=0), jnp.take_along_axis(hi, low3, axis=0)))
    return jnp.concatenate(out, axis=0)


def _route_kernel(pq_ref, sk_ref, i_ref, j_ref, g_ref, *, unrolled=False):
    half = sk_ref.shape[2]
    tt = min(2 * LANES, pq_ref.shape[0])
    k = PEER_TOPK

    def sub_tile(n, carry):
        r0 = pl.multiple_of(n * tt, tt)
        sv, si = [], []
        for p in range(2):
            q = pq_ref[pl.ds(r0, tt), p * half:(p + 1) * half]
            st = lax.dot_general(sk_ref[p], q, _NT, preferred_element_type=F32)
            v, i = _top16(st)
            sv.append(v)
            si.append(i)
        sub = 8
        rows8 = _iota((sub, tt), 0)
        cands = [sv[0][0:1, :] + sv[1]]
        for a in range(1, sub):
            cands.append(jnp.where(rows8 < k // (a + 1), sv[0][a:a + 1, :] + sv[1][:sub, :], NEG_INF))
        cands.append(sv[0][sub:, :] + sv[1][0:1, :])
        cand = jnp.concatenate(cands, axis=0)
        nrows = cand.shape[0]
        pio = _iota(cand.shape, 0).astype(F32)
        tv, tp = [], []
        for _ in range(k):
            m = jnp.max(cand, axis=0, keepdims=True)
            pos = jnp.min(jnp.where(cand == m, pio, float(nrows)), axis=0, keepdims=True)
            cand = jnp.where(pio == pos, NEG_INF, cand)
            tv.append(m)
            tp.append(pos)
        top = jnp.concatenate(tv, axis=0)
        e = jnp.exp(top - top[0:1, :])
        g_ref[:, pl.ds(r0, tt)] = e / jnp.sum(e, axis=0, keepdims=True)
        pos = jnp.concatenate(tp, axis=0).astype(I32)
        mid = pos - k
        first, last = pos < k, pos >= nrows - (k - sub)
        a_sel = jnp.where(first, 0, jnp.where(last, pos - (nrows - k), 1 + (mid >> 3)))
        b_sel = jnp.where(first, pos, jnp.where(last, 0, mid & (sub - 1)))
        i_ref[:, pl.ds(r0, tt)] = _gather16(si[0], a_sel)
        j_ref[:, pl.ds(r0, tt)] = _gather16(si[1], b_sel)
        return carry

    if unrolled:
        return [functools.partial(sub_tile, n, 0) for n in range(pq_ref.shape[0] // tt)]
    lax.fori_loop(0, pq_ref.shape[0] // tt, sub_tile, 0)


def _route(pq, subkeys, tt, first=0, tiles=None):
    hp, keys, half = subkeys.shape
    heads = hp // 2
    k = PEER_TOPK
    n = (pq.shape[0] // tt if tiles is None else tiles) * tt
    out = pl.BlockSpec((k, tt), lambda i, h: (h, i))
    return pl.pallas_call(
        _route_kernel, grid=(n // tt, heads),
        in_specs=[pl.BlockSpec((tt, 2 * half), lambda i, h: (first + i, h)),
                  pl.BlockSpec((2, keys, half), lambda i, h: (h, 0, 0))],
        out_specs=[out, out, out],
        out_shape=[jax.ShapeDtypeStruct((heads * k, n), I32), jax.ShapeDtypeStruct((heads * k, n), I32),
                   jax.ShapeDtypeStruct((heads * k, n), F32)],
        compiler_params=_params("parallel", "parallel"), name="peer_route",
    )(pq, subkeys)


def _peer_u_groups(h_ref, ut_ref, i_ref, j_ref, a_ref, ib, group=4):
    step = pl.program_id(1)

    @pl.when(step == 0)
    def _():
        a_ref[...] = jnp.zeros_like(a_ref)

    nj = j_ref.shape[1]

    def run(g0):
        isel = i_ref[...]
        jsel = j_ref[...]
        acc = a_ref[...]
        a_all = _dot(h_ref[...], ut_ref[:, g0 * nj:(g0 + group) * nj])
        for ii in range(group):
            a = a_all[:, ii * nj:(ii + 1) * nj]
            acc = jnp.where(isel == step * ib + g0 + ii, jnp.take_along_axis(a, jsel, axis=1), acc)
        a_ref[...] = acc

    return [functools.partial(run, g0) for g0 in range(0, ib, group)]


def _peer_u_kernel(h_ref, ut_ref, i_ref, j_ref, a_ref, *, ib):
    for run in _peer_u_groups(h_ref, ut_ref, i_ref, j_ref, a_ref, ib):
        run()


def _peer_u(h, ut, isel, jsel, tt, ib, first=0):
    d = h.shape[1]
    n, slots = isel.shape
    nblk = ut.shape[1] // (slots * ib)
    tok = lambda c: pl.BlockSpec((tt, c), lambda t, e: (t, 0))
    return pl.pallas_call(
        functools.partial(_peer_u_kernel, ib=ib), grid=(n // tt, nblk),
        in_specs=[pl.BlockSpec((tt, d), lambda t, e: (first + t, 0)),
                  pl.BlockSpec((d, slots * ib), lambda t, e: (0, e)), tok(slots), tok(slots)],
        out_specs=tok(slots), out_shape=jax.ShapeDtypeStruct((n, slots), F32),
        compiler_params=_params("parallel", "arbitrary"), name="peer_u",
    )(h, ut, isel, jsel)


def _route_peer_u_kernel(pq_ref, sk_ref, h_ref, ut_ref, isel_ref, jsel_ref, i_ref, j_ref, g_ref, a_ref, *, ib):
    matmuls = _peer_u_groups(h_ref, ut_ref, isel_ref, jsel_ref, a_ref, ib)
    tiles = _route_kernel(pq_ref, sk_ref, i_ref, j_ref, g_ref, unrolled=True)
    for n in range(max(len(matmuls), len(tiles))):
        if n < len(matmuls):
            matmuls[n]()
        if n < len(tiles):
            tiles[n]()


def _route_peer_u(pq, subkeys, h, ut, isel, jsel, tt, ib, route_first, act_first):
    d = h.shape[1]
    n, slots = isel.shape
    hp, keys, half = subkeys.shape
    heads = hp // 2
    assert ut.shape[1] == heads * slots * ib
    k = PEER_TOPK
    tok = lambda c: pl.BlockSpec((tt, c), lambda t, e: (t, 0))
    rout = pl.BlockSpec((k, tt), lambda t, e: (e, t))
    return pl.pallas_call(
        functools.partial(_route_peer_u_kernel, ib=ib), grid=(n // tt, heads),
        in_specs=[pl.BlockSpec((tt, 2 * half), lambda t, e: (route_first + t, e)),
                  pl.BlockSpec((2, keys, half), lambda t, e: (e, 0, 0)),
                  pl.BlockSpec((tt, d), lambda t, e: (act_first + t, 0)),
                  pl.BlockSpec((d, slots * ib), lambda t, e: (0, e)), tok(slots), tok(slots)],
        out_specs=[rout, rout, rout, tok(slots)],
        out_shape=[jax.ShapeDtypeStruct((heads * k, n), I32), jax.ShapeDtypeStruct((heads * k, n), I32),
                   jax.ShapeDtypeStruct((heads * k, n), F32), jax.ShapeDtypeStruct((n, slots), F32)],
        compiler_params=_params("parallel", "arbitrary"), name="route_peer_u",
    )(pq, subkeys, h, ut, isel, jsel)


def _peer_v_kernel(i_ref, j_ref, g_ref, a_ref, x_ref, v_ref, o_ref, z_sc, zs_sc, *, ib, stride):
    step = pl.program_id(1)
    tt, slots = i_ref.shape

    @pl.when(step == 0)
    def _():
        a = a_ref[...]
        z_sc[...] = g_ref[...] * (0.5 * a * (1.0 + lax.erf(a * (2.0 ** -0.5))))

        def scatter(t, carry):
            irow = i_ref[pl.ds(t, 1), :]
            jrow = j_ref[pl.ds(t, 1), :]
            zrow = z_sc[pl.ds(t, 1), :]
            io = _iota((slots, slots), 0)
            zit = jnp.where(irow == io, zrow, 0.0).astype(BF16)
            oht = jnp.where(jrow == io, 1.0, 0.0).astype(BF16)
            zs_sc[pl.ds(pl.multiple_of(t * stride, 8), slots), :] = lax.dot_general(
                zit, oht, _NT, preferred_element_type=F32)
            return carry

        lax.fori_loop(0, tt, scatter, 0, unroll=32)
        o_ref[...] = x_ref[...]

    base = step * ib
    zblk = jnp.concatenate(
        [zs_sc[pl.ds(base + ii, tt, stride=stride), :].astype(BF16) for ii in range(ib)], axis=1)
    o_ref[...] += _dot(zblk, v_ref[...])


def _peer_v(isel, jsel, gates, act, x, v, tt, ib):
    n, d = x.shape
    slots = isel.shape[1]
    nblk = v.shape[0] // (slots * ib)
    stride = slots + 8
    tok = lambda c: pl.BlockSpec((tt, c), lambda t, e: (t, 0))
    return pl.pallas_call(
        functools.partial(_peer_v_kernel, ib=ib, stride=stride), grid=(n // tt, nblk),
        in_specs=[tok(slots)] * 4 + [tok(d), pl.BlockSpec((slots * ib, d), lambda t, e: (e, 0))],
        out_specs=tok(d), out_shape=jax.ShapeDtypeStruct((n, d), F32),
        scratch_shapes=[pltpu.VMEM((tt, slots), F32), pltpu.VMEM((tt * stride, slots), F32)],
        compiler_params=_params("parallel", "arbitrary"), name="peer_v",
    )(isel, jsel, gates, act, x, v)


def _final_kernel(x_ref, g_ref, o_ref):
    o_ref[...] = _rms(x_ref[...], g_ref[...])


def _final_norm(x, g, tm):
    n, d = x.shape
    return pl.pallas_call(
        _final_kernel, grid=(n // tm,),
        in_specs=[pl.BlockSpec((tm, d), lambda i: (i, 0)), pl.BlockSpec((1, d), lambda i: (0, 0))],
        out_specs=pl.BlockSpec((tm, d), lambda i: (i, 0)), out_shape=jax.ShapeDtypeStruct((n, d), F32),
        compiler_params=_params("parallel"), name="final_norm",
    )(x, g)


def _tile(n, name):
    pref = TILE[name]
    return pref if n % pref == 0 else n


def _peer(x2, h3, pq, lw):
    n = x2.shape[0]
    tt, ib = _tile(n, "peer_u"), TILE["peer_blocks"]
    assert _tile(n, "route") == tt
    groups = TILE["peer_groups"] if n % (TILE["peer_groups"] * tt) == 0 else 1
    per = n // tt // groups
    routed = [_route(pq, lw["subkeys"], tt, 0, per)]
    acts = []
    for g in range(groups):
        isel_g, jsel_g = routed[g][0].T, routed[g][1].T
        if g + 1 < groups:
            it, jt, gt, act = _route_peer_u(pq, lw["subkeys"], h3, lw["ut"], isel_g, jsel_g, tt, ib,
                                            (g + 1) * per, g * per)
            routed.append((it, jt, gt))
        else:
            act = _peer_u(h3, lw["ut"], isel_g, jsel_g, tt, ib, g * per)
        acts.append(act)
    cat = lambda k: jnp.concatenate([r[k] for r in routed], axis=1).T
    return _peer_v(cat(0), cat(1), cat(2), jnp.concatenate(acts, axis=0), x2, lw["v"], _tile(n, "peer_v"), ib)


def kernel(x_prompt, x_sample, cache_k, cache_v, cache_logf, state_hg, cache_mem_k, cache_mem_v, page_table, mem_prompt, ln_mix, w_in, b_fox_f, hg_lb, fox_gn, hg_gn, w_out, ln_x, ln_mem, w_xq, w_xk, w_xv, w_xo, ln_ffn, peer_wq, peer_subkeys, peer_u, peer_v, ln_final):
    depth, d_model, _ = w_in.shape
    bp, seq, _ = x_prompt.shape
    bs = x_sample.shape[0]
    fox_heads = b_fox_f.shape[1]
    fox_w = fox_gn.shape[1]
    hg_w = hg_gn.shape[1]
    hg_heads = hg_w // HG_DK
    n_phys, page = cache_k.shape[1], cache_k.shape[2]
    n_mem = mem_prompt.shape[1]
    x_w = w_xq.shape[2]
    x_heads = x_w // X_HEAD_DIM
    p_heads = peer_subkeys.shape[1]
    p_keys, p_half = peer_subkeys.shape[3], peer_subkeys.shape[4]
    pairs = fox_w // LANES

    lb_all = jnp.cumsum(jax.nn.softmax(hg_lb.astype(F32), axis=0), axis=0)
    row = lambda a: a.reshape(1, -1).astype(F32)

    xp = x_prompt.reshape(bp * seq, d_model)
    xs = x_sample.reshape(bs, d_model)
    mem = mem_prompt.reshape(bp * n_mem, d_model)
    ck_t = cache_k.transpose(0, 1, 3, 4, 2).reshape(depth, n_phys, fox_w, page)
    cv_t = cache_v.transpose(0, 1, 3, 4, 2).reshape(depth, n_phys, fox_w, page)
    clf_t = cache_logf.transpose(0, 1, 3, 2)
    state_t = state_hg.transpose(0, 2, 3, 4, 1)
    memk = cache_mem_k.reshape(depth, bs, n_mem * x_heads, X_HEAD_DIM)
    memv = cache_mem_v.reshape(depth, bs, n_mem * x_heads, X_HEAD_DIM)

    outs = {k: [] for k in ("kp", "vp", "fp", "hp", "mkp", "mvp", "ks", "vs", "fs", "hs")}
    for l in range(depth):
        wl = w_in[l]
        c0 = 3 * fox_w
        lw = dict(
            w_main=jnp.concatenate([wl[:, :c0], wl[:, c0 + fox_heads:]], axis=1).astype(BF16),
            w_ff=jnp.pad(wl[:, c0:c0 + fox_heads], ((0, 0), (0, LANES - fox_heads))).astype(BF16),
            b_f=jnp.pad(row(b_fox_f[l]), ((0, 0), (0, LANES - fox_heads))),
            lb=row(lb_all[l] - lb_all[0]),
            subkeys=peer_subkeys[l].reshape(p_heads * 2, p_keys, p_half).astype(BF16),
            ut=peer_u[l].astype(BF16).T, v=peer_v[l].astype(BF16),
        )
        w_out_b, w_xq_b, w_xo_b = w_out[l].astype(BF16), w_xq[l].astype(BF16), w_xo[l].astype(BF16)
        peer_wq_b = peer_wq[l].astype(BF16)

        tm = _tile(bp * seq, "tokenwise")
        fq, fk, fv, flf, hq, hk, hv, hlf, hgate = _inproj(xp, row(ln_mix[l]), lw["w_main"], lw["w_ff"], lw["b_f"],
                                                           lw["lb"], tm, fox_heads)
        lf_t = flf.reshape(bp, seq, fox_heads).transpose(0, 2, 1).reshape(bp * fox_heads, seq)
        c_t = _cumsum_lanes(lf_t, _tile(seq, "cumsum"))
        ck = c_t.reshape(bp * pairs, 2, seq)
        cq = c_t.reshape(bp, pairs, 2, seq).transpose(1, 0, 3, 2).reshape(pairs, bp * seq, 2)
        fo = _fox_prompt(fq, fk, fv, cq, ck, bp, _tile(seq, "fox"))
        sq, sk, sv, slf, tq, tk, tv, tlf, tgate = _inproj(xs, row(ln_mix[l]), lw["w_main"], lw["w_ff"], lw["b_f"],
                                                           lw["lb"], bs, fox_heads)
        colm = lambda a: a.reshape(bs, a.shape[1], 1)
        decode_args = (page_table, colm(sq), colm(sk), colm(sv), colm(slf), ck_t, cv_t, clf_t, l)
        if seq // HG_CHUNK == bs:
            ho, st, fo_s = _hgrn_prompt_fox_decode(hq, hk, hv, hlf, bp, *decode_args)
        else:
            ho, st = _hgrn_prompt(hq, hk, hv, hlf, bp)
            fo_s = _fox_decode(*decode_args)
        fo_s = fo_s.reshape(bs, fox_w)
        tw = _tile(bp * seq, "tokenwise_wide")
        x1, qx = _merge(fo, ho, hgate, xp, row(fox_gn[l]), row(hg_gn[l]), w_out_b, row(ln_x[l]), w_xq_b, tw)
        mk, mv = _memkv(mem, row(ln_mem[l]), w_xk[l].astype(BF16), w_xv[l].astype(BF16), n_mem)
        ctx = _xattn_prompt(qx, mk, mv, bp, tm)
        x2, h3, pq = _xo_peerq(x1, ctx, w_xo_b, row(ln_ffn[l]), peer_wq_b, tw)
        xp = _peer(x2, h3, pq, lw)
        outs["kp"].append(fk.reshape(bp, seq, fox_heads, HEAD_DIM))
        outs["vp"].append(fv.reshape(bp, seq, fox_heads, HEAD_DIM))
        outs["fp"].append(flf.reshape(bp, seq, fox_heads))
        st5 = st.reshape(bp, pairs, 2, HG_DK, 2, HG_DK)
        s_heads = jnp.stack([st5[:, :, 0, :, 0, :], st5[:, :, 1, :, 1, :]], axis=2)
        outs["hp"].append(s_heads.reshape(bp, hg_heads, HG_DK, HG_DK).transpose(0, 1, 3, 2))
        outs["mkp"].append(mk.reshape(bp, n_mem, x_heads, X_HEAD_DIM))
        outs["mvp"].append(mv.reshape(bp, n_mem, x_heads, X_HEAD_DIM))

        lanes_r = lambda a: a.T.reshape(hg_heads, HG_DK, 1, bs)
        ho_t, s_new = _hgrn_step(lanes_r(tq), lanes_r(tk), lanes_r(tlf), tv.T.reshape(hg_heads, HG_DK, bs),
                                 state_t, l)
        x1s, qxs = _merge(fo_s, ho_t.reshape(hg_w, bs).T, tgate, xs, row(fox_gn[l]), row(hg_gn[l]), w_out_b,
                          row(ln_x[l]), w_xq_b, bs)
        qxs_t = jnp.pad(qxs.reshape(bs, x_heads, X_HEAD_DIM).transpose(0, 2, 1), ((0, 0), (0, 0), (0, 8 - x_heads)))
        ctx_s = _xattn_sample(qxs_t, memk, memv, l, x_heads, _tile(bs, "xattn_sample")).reshape(bs, x_w)
        x2s, h3s, pqs = _xo_peerq(x1s, ctx_s, w_xo_b, row(ln_ffn[l]), peer_wq_b, bs)
        xs = _peer(x2s, h3s, pqs, lw)
        outs["ks"].append(sk.reshape(bs, 1, fox_heads, HEAD_DIM))
        outs["vs"].append(sv.reshape(bs, 1, fox_heads, HEAD_DIM))
        outs["fs"].append(slf.reshape(bs, 1, fox_heads))
        outs["hs"].append(s_new)

    y_prompt = _final_norm(xp, row(ln_final), _tile(bp * seq, "tokenwise")).reshape(bp, seq, d_model)
    y_sample = _final_norm(xs, row(ln_final), bs).reshape(bs, 1, d_model)
    st_ = lambda k: jnp.stack(outs[k])
    return (y_prompt, y_sample, st_("kp"), st_("vp"), st_("fp"), st_("hp"), st_("mkp"), st_("mvp"),
            st_("ks"), st_("vs"), st_("fs"), st_("hs").transpose(0, 4, 1, 2, 3))
```

```python
import functools
import math

import jax
import jax.numpy as jnp
from jax import lax
from jax.experimental import pallas as pl
from jax.experimental.pallas import tpu as pltpu

F32 = jnp.float32
BF16 = jnp.bfloat16
I32 = jnp.int32
EPS = 1e-6
NEG_INF = float("-inf")

HEAD_DIM = 64
HG_DK = 64
HG_CHUNK = 64
HG_SUB = 16
X_HEAD_DIM = 128
PEER_TOPK = 16
LANES = 128
V7X_VMEM_BYTES = 64 * 1024 * 1024
VMEM_LIMIT = V7X_VMEM_BYTES - 8 * 1024 * 1024

TILE = dict(
    tokenwise=512,
    tokenwise_wide=1024,
    cumsum=512,
    fox=1024,
    route=1024,
    peer_groups=8,
    peer_u=1024,
    peer_v=512,
    peer_blocks=16,
    xattn_sample=8,
)

_NT = (((1,), (1,)), ((), ()))
_TN = (((0,), (0,)), ((), ()))


def _params(*sem):
    return pltpu.CompilerParams(dimension_semantics=sem, vmem_limit_bytes=VMEM_LIMIT)


def _rms(x, g):
    ms = jnp.mean(x * x, axis=-1, keepdims=True)
    return x * lax.rsqrt(ms + EPS) * g


def _log_sigmoid(x):
    return jnp.minimum(x, 0.0) - jnp.log1p(jnp.exp(-jnp.abs(x)))


def _dot(a, b):
    return jnp.dot(a, b, preferred_element_type=F32)


def _split3(x):
    hi = x.astype(BF16)
    r = x - hi.astype(F32)
    mid = r.astype(BF16)
    lo = (r - mid.astype(F32)).astype(BF16)
    return hi, mid, lo


def _prefix_rows(tril, x):
    return _dot(jnp.concatenate([tril] * 3, axis=1), jnp.concatenate(_split3(x), axis=0))


def _prefix_lanes(x, triu):
    return _dot(jnp.concatenate(_split3(x), axis=1), jnp.concatenate([triu] * 3, axis=0))


def _iota(shape, axis):
    return lax.broadcasted_iota(I32, shape, axis)


def _inproj_kernel(x_ref, g_ref, w_ref, wff_ref, bf_ref, lb_ref,
                   fq_ref, fk_ref, fv_ref, lf_ref, hq_ref, hk_ref, hv_ref, hlf_ref, gate_ref, *, fox_heads):
    h = _rms(x_ref[...], g_ref[...]).astype(BF16)
    w = w_ref.shape[1] // 7

    def mm(j):
        return _dot(h, w_ref[:, j * w:(j + 1) * w])

    fq_ref[...] = mm(0)
    fk_ref[...] = mm(1)
    fv_ref[...] = mm(2)
    ff = _dot(h, wff_ref[...])
    lf_ref[...] = _log_sigmoid(ff + bf_ref[...])[:, :fox_heads]
    hq_ref[...] = mm(3)
    z = mm(4)
    lb = lb_ref[...]
    a = jnp.log(lb)
    b = jnp.log1p(-lb) + _log_sigmoid(z)
    hlf_ref[...] = jnp.maximum(a, b) + jnp.log1p(jnp.exp(-jnp.abs(a - b)))
    hk_ref[...] = (1.0 - lb) * (1.0 / (1.0 + jnp.exp(z)))
    hv_ref[...] = mm(5)
    gate_ref[...] = mm(6)


def _inproj(x, g, w_main, w_ff, b_f, lb, tm, fox_heads):
    n, d = x.shape
    w = w_main.shape[1] // 7
    full = lambda a: pl.BlockSpec(a.shape, lambda i: (0,) * a.ndim)
    row = lambda c: pl.BlockSpec((tm, c), lambda i: (i, 0))
    outs = [jax.ShapeDtypeStruct((n, w), F32)] * 3 + [jax.ShapeDtypeStruct((n, fox_heads), F32)] + \
           [jax.ShapeDtypeStruct((n, w), F32)] * 5
    out_specs = [row(w)] * 3 + [row(fox_heads)] + [row(w)] * 5
    return pl.pallas_call(
        functools.partial(_inproj_kernel, fox_heads=fox_heads),
        grid=(n // tm,),
        in_specs=[row(d), full(g), full(w_main), full(w_ff), full(b_f), full(lb)],
        out_specs=out_specs, out_shape=outs,
        compiler_params=_params("parallel"), name="inproj",
    )(x, g, w_main, w_ff, b_f, lb)


def _cumsum_kernel(x_ref, o_ref, carry_ref):
    @pl.when(pl.program_id(0) == 0)
    def _():
        carry_ref[...] = jnp.zeros_like(carry_ref)

    x = x_ref[...]
    n = x.shape[1]
    tri = (_iota((n, n), 0) <= _iota((n, n), 1)).astype(BF16)
    c = _prefix_lanes(x, tri) + carry_ref[:, :1]
    o_ref[...] = c
    carry_ref[...] = jnp.broadcast_to(c[:, n - 1:n], carry_ref.shape)


def _cumsum_lanes(x, tc):
    r, t = x.shape
    return pl.pallas_call(
        _cumsum_kernel, grid=(t // tc,),
        in_specs=[pl.BlockSpec((r, tc), lambda i: (0, i))],
        out_specs=pl.BlockSpec((r, tc), lambda i: (0, i)),
        out_shape=jax.ShapeDtypeStruct((r, t), F32),
        scratch_shapes=[pltpu.VMEM((r, LANES), F32)],
        compiler_params=_params("arbitrary"), name="logf_cumsum",
    )(x)


def _fox_kernel(qt_ref, kt_ref, q_ref, k_ref, v_ref, cq_ref, ck_ref, o_ref, m_ref, cqr_ref, acc_ref, qh_ref, *,
                scale):
    qi = qt_ref[pl.program_id(2)]
    ki = kt_ref[pl.program_id(2)]
    tq = q_ref.shape[0]
    tk = k_ref.shape[0]
    lane = _iota((1, LANES), 1)
    log2e = math.log2(math.e)

    @pl.when(ki == 0)
    def _():
        m_ref[...] = jnp.full_like(m_ref, NEG_INF)
        acc_ref[...] = jnp.zeros_like(acc_ref)
        q = q_ref[...] * (scale * log2e)
        for hh in range(2):
            cqr_ref[hh] = jnp.broadcast_to(cq_ref[:, hh:hh + 1] * log2e, (tq, LANES))
            qh_ref[hh] = jnp.where((lane // HEAD_DIM) == hh, q, 0.0).astype(BF16)

    def step(masked, blocks):
        k = k_ref[...].astype(BF16)
        v = v_ref[...]
        ck = ck_ref[...] * log2e
        for hh in range(2):
            vh = jnp.where((lane // HEAD_DIM) == hh, v, 1.0).astype(BF16)
            for r0, r1, nk in blocks:
                s = lax.dot_general(qh_ref[hh, r0:r1], k[:nk], _NT, preferred_element_type=F32) - ck[hh:hh + 1, :nk]
                if masked:
                    causal = (ki * tk + _iota((1, nk), 1)) <= (qi * tq + r0 + _iota((r1 - r0, 1), 0))
                    s = jnp.where(causal, s, NEG_INF)
                m_prev = m_ref[hh, r0:r1]
                cq = cqr_ref[hh, r0:r1]
                m_new = jnp.maximum(m_prev, jnp.max(s, axis=1, keepdims=True) + cq)
                p = jnp.exp2(s - jnp.tile(m_new - cq, (1, nk // LANES)))
                acc_ref[hh, r0:r1] = jnp.exp2(m_prev - m_new) * acc_ref[hh, r0:r1] + _dot(p.astype(BF16), vh[:nk])
                m_ref[hh, r0:r1] = m_new

    @pl.when(ki < qi)
    def _():
        step(False, [(0, tq, tk)])

    @pl.when(ki == qi)
    def _():
        half = tq // 2
        step(True, [(0, half, tk // 2), (half, tq, tk)] if half % LANES == 0 else [(0, tq, tk)])
        a0 = acc_ref[0]
        a1 = acc_ref[1]
        o0 = a0 / pltpu.roll(a0, HEAD_DIM, axis=1)
        o1 = a1 / pltpu.roll(a1, HEAD_DIM, axis=1)
        o_ref[...] = jnp.where(lane < HEAD_DIM, o0, o1)


def _fox_prompt(q, k, v, cq, ck, batch, tq):
    n, w = q.shape
    t = n // batch
    pairs = w // LANES
    nq = t // tq
    kern = functools.partial(_fox_kernel, scale=HEAD_DIM ** -0.5)
    steps = [(i, j) for i in range(nq) for j in range(i + 1)]
    qi_tab = jnp.asarray([s[0] for s in steps], I32)
    ki_tab = jnp.asarray([s[1] for s in steps], I32)
    grid_spec = pltpu.PrefetchScalarGridSpec(
        num_scalar_prefetch=2, grid=(batch, pairs, len(steps)),
        in_specs=[
            pl.BlockSpec((tq, LANES), lambda b, g, s, qt, kt: (b * nq + qt[s], g)),
            pl.BlockSpec((tq, LANES), lambda b, g, s, qt, kt: (b * nq + kt[s], g)),
            pl.BlockSpec((tq, LANES), lambda b, g, s, qt, kt: (b * nq + kt[s], g)),
            pl.BlockSpec((None, tq, 2), lambda b, g, s, qt, kt: (g, b * nq + qt[s], 0)),
            pl.BlockSpec((None, 2, tq), lambda b, g, s, qt, kt: (b * pairs + g, 0, kt[s])),
        ],
        out_specs=pl.BlockSpec((tq, LANES), lambda b, g, s, qt, kt: (b * nq + qt[s], g)),
        scratch_shapes=[pltpu.VMEM((2, tq, LANES), F32)] * 3 + [pltpu.VMEM((2, tq, LANES), BF16)],
    )
    return pl.pallas_call(
        kern, grid_spec=grid_spec, out_shape=jax.ShapeDtypeStruct((n, w), F32),
        compiler_params=_params("parallel", "parallel", "arbitrary"), name="fox_prompt",
    )(qi_tab, ki_tab, q, k, v, cq, ck)


def _hgrn_kernel(q_ref, k_ref, v_ref, lf_ref, o_ref, st_ref, st_sc, *, also=()):
    n = pl.program_id(0)
    nb, c, w = q_ref.shape
    pairs = w // LANES

    @pl.when(n == 0)
    def _():
        st_sc[...] = jnp.zeros_like(st_sc)

    tril = (_iota((c, c), 1) <= _iota((c, c), 0)).astype(BF16)
    same_head = (_iota((LANES, LANES), 0) // HG_DK) == (_iota((LANES, LANES), 1) // HG_DK)
    seg = same_head.astype(BF16)
    sc = HG_SUB
    ti = _iota((sc, sc, LANES), 0)
    si = _iota((sc, sc, LANES), 1)
    first_head = _iota((1, LANES), 1) < HG_DK
    for bg in range(nb * pairs):
        r, g = divmod(bg, pairs)
        sl = slice(g * LANES, (g + 1) * LANES)
        q = q_ref[r, :, sl]
        k = k_ref[r, :, sl]
        v = v_ref[r, :, sl]
        vb = v.astype(BF16)
        b = _prefix_rows(tril, lf_ref[r, :, sl])
        st = st_sc[r, g]
        qe = (q * jnp.exp(b)).astype(BF16)
        o = lax.dot_general(qe, st.astype(BF16), _NT, preferred_element_type=F32)
        rows = []
        for i in range(c // sc):
            lo, hi = i * sc, (i + 1) * sc
            qi, ki, bi = q[lo:hi], k[lo:hi], b[lo:hi]
            decay = jnp.exp(jnp.where(si <= ti, bi[:, None, :] - bi[None, :, :], NEG_INF))
            p = (qi[:, None, :] * ki[None, :, :]) * decay
            a = _dot(p.reshape(sc * sc, LANES).astype(BF16), seg)
            oi = jnp.sum(a.reshape(sc, sc, LANES) * v[None, lo:hi, :], axis=1)
            if i > 0:
                br = b[lo - 1:lo, :]
                qt = qi * jnp.exp(bi - br)
                q2 = jnp.concatenate([jnp.where(first_head, qt, 0.0), jnp.where(first_head, 0.0, qt)], axis=0)
                ks = (k[:lo] * jnp.exp(br - b[:lo])).astype(BF16)
                a2 = lax.dot_general(q2.astype(BF16), ks, _NT, preferred_element_type=F32)
                o2 = _dot(a2.astype(BF16), vb[:lo])
                oi = oi + jnp.where(first_head, o2[:sc], o2[sc:])
            rows.append(oi)
        o_ref[r, :, sl] = o + jnp.concatenate(rows, axis=0)
        b_last = b[c - 1:c, :]
        kd = (k * jnp.exp(b_last - b)).astype(BF16)
        upd = lax.dot_general(vb, kd, _TN, preferred_element_type=F32)
        st_sc[r, g] = st * jnp.exp(b_last) + jnp.where(same_head, upd, 0.0)
        share = -(-len(also) // (nb * pairs))
        for thunk in also[bg * share:(bg + 1) * share]:
            thunk()

    @pl.when(n == pl.num_programs(0) - 1)
    def _():
        st_ref[...] = st_sc[...]


def _hgrn_prompt(q, k, v, lf, batch):
    n, w = q.shape
    t = n // batch
    pairs = w // LANES
    rows3 = lambda a: a.reshape(batch, t, w)
    blk = pl.BlockSpec((batch, HG_CHUNK, w), lambda i: (0, i, 0))
    st_shape = (batch, pairs, LANES, LANES)
    o, st = pl.pallas_call(
        _hgrn_kernel, grid=(t // HG_CHUNK,),
        in_specs=[blk] * 4,
        out_specs=[blk, pl.BlockSpec(st_shape, lambda i: (0, 0, 0, 0))],
        out_shape=[jax.ShapeDtypeStruct((batch, t, w), F32), jax.ShapeDtypeStruct(st_shape, F32)],
        scratch_shapes=[pltpu.VMEM(st_shape, F32)],
        compiler_params=_params("arbitrary"), name="hgrn_prompt",
    )(rows3(q), rows3(k), rows3(v), rows3(lf))
    return o.reshape(n, w), st


def _fox_decode_stages(pt_ref, q_ref, kn_ref, vn_ref, lfn_ref, *refs, scale, n_pages):
    k_refs = refs[:n_pages]
    v_refs = refs[n_pages:2 * n_pages]
    lf_refs = refs[2 * n_pages:3 * n_pages]
    o_ref = refs[3 * n_pages]
    w, ps = k_refs[0].shape
    nh = lf_refs[0].shape[0]
    hd = w // nh
    heads = lambda x: x.reshape(nh, hd, x.shape[1])
    st = dict(scores=[], carry=jnp.zeros((nh, 1), F32), acc=jnp.zeros((w, ps), F32))

    def setup():
        st["q"] = q_ref[...] * scale
        st["qb"] = jnp.broadcast_to(st["q"], (w, ps))
        st["triu"] = (_iota((ps, ps), 0) <= _iota((ps, ps), 1)).astype(BF16)

    def score(p):
        s = jnp.sum(heads(k_refs[p][...] * st["qb"]), axis=1)
        within = _prefix_lanes(lf_refs[p][...], st["triu"])
        st["scores"].append(s - (within + st["carry"]))
        st["carry"] = st["carry"] + within[:, ps - 1:ps]

    def softmax():
        scores = st["scores"]
        s_new = jnp.sum(heads(kn_ref[...] * st["q"]), axis=1) - (st["carry"] + lfn_ref[...])
        m_tile = scores[0]
        for s in scores[1:]:
            m_tile = jnp.maximum(m_tile, s)
        m = jnp.maximum(s_new, jnp.max(m_tile, axis=1, keepdims=True))
        st["e_new"] = jnp.exp(s_new - m)
        st["exps"] = [jnp.exp(s - m) for s in scores]
        l_tile = st["exps"][0]
        for e in st["exps"][1:]:
            l_tile = l_tile + e
        st["inv"] = 1.0 / (st["e_new"] + jnp.sum(l_tile, axis=1, keepdims=True))

    def value(p):
        pb = jnp.broadcast_to(st["exps"][p][:, None, :], (nh, hd, ps)).reshape(w, ps)
        st["acc"] = st["acc"] + v_refs[p][...] * pb

    def finish():
        rep = lambda x: jnp.broadcast_to(x[:, None, :], (nh, hd, 1)).reshape(w, 1)
        o_ref[...] = (jnp.sum(st["acc"], axis=1, keepdims=True) + rep(st["e_new"]) * vn_ref[...]) * rep(st["inv"])

    part = functools.partial
    return ([setup] + [part(score, p) for p in range(n_pages)] + [softmax]
            + [part(value, p) for p in range(n_pages)] + [finish])


def _fox_decode_kernel(*refs, scale, n_pages):
    for stage in _fox_decode_stages(*refs, scale=scale, n_pages=n_pages):
        stage()


def _fox_decode(page_table, q, k_new, v_new, lf_new, cache_kt, cache_vt, cache_lft, layer):
    r, w, _ = q.shape
    nh = lf_new.shape[1]
    n_pages = page_table.shape[1]
    ps = cache_kt.shape[3]
    pt = page_table.reshape(-1)
    tok = lambda c: pl.BlockSpec((None, c, 1), lambda i, pt: (i, 0, 0))

    def page(p, rows):
        return pl.BlockSpec((None, None, rows, ps), lambda i, pt: (layer, pt[i * n_pages + p], 0, 0))

    pages = lambda rows: [page(p, rows) for p in range(n_pages)]
    grid_spec = pltpu.PrefetchScalarGridSpec(
        num_scalar_prefetch=1, grid=(r,),
        in_specs=[tok(w), tok(w), tok(w), tok(nh)] + pages(w) + pages(w) + pages(nh),
        out_specs=tok(w),
    )
    return pl.pallas_call(
        functools.partial(_fox_decode_kernel, scale=HEAD_DIM ** -0.5, n_pages=n_pages),
        grid_spec=grid_spec, out_shape=jax.ShapeDtypeStruct((r, w, 1), F32),
        compiler_params=_params("parallel"), name="fox_decode",
    )(pt, q, k_new, v_new, lf_new, *([cache_kt] * n_pages), *([cache_vt] * n_pages), *([cache_lft] * n_pages))


def _hgrn_decode_kernel(pt_ref, hq_ref, hk_ref, hv_ref, hlf_ref, q_ref, kn_ref, vn_ref, lfn_ref, *refs, scale,
                        n_pages):
    pages = refs[:3 * n_pages]
    ho_ref, st_ref, od_ref, st_sc = refs[3 * n_pages:]
    stages = _fox_decode_stages(pt_ref, q_ref, kn_ref, vn_ref, lfn_ref, *pages, od_ref, scale=scale, n_pages=n_pages)
    _hgrn_kernel(hq_ref, hk_ref, hv_ref, hlf_ref, ho_ref, st_ref, st_sc, also=stages)


def _hgrn_prompt_fox_decode(hq, hk, hv, hlf, batch, page_table, q, k_new, v_new, lf_new, cache_kt, cache_vt,
                            cache_lft, layer):
    n, w = hq.shape
    t = n // batch
    pairs = w // LANES
    r, wd, _ = q.shape
    nh = lf_new.shape[1]
    n_pages = page_table.shape[1]
    ps = cache_kt.shape[3]
    assert t // HG_CHUNK == r
    rows3 = lambda a: a.reshape(batch, t, w)
    blk = pl.BlockSpec((batch, HG_CHUNK, w), lambda i, pt: (0, i, 0))
    st_shape = (batch, pairs, LANES, LANES)
    tok = lambda c: pl.BlockSpec((None, c, 1), lambda i, pt: (i, 0, 0))

    def page(p, rows):
        return pl.BlockSpec((None, None, rows, ps), lambda i, pt: (layer, pt[i * n_pages + p], 0, 0))

    pages = lambda rows: [page(p, rows) for p in range(n_pages)]
    grid_spec = pltpu.PrefetchScalarGridSpec(
        num_scalar_prefetch=1, grid=(r,),
        in_specs=[blk] * 4 + [tok(wd), tok(wd), tok(wd), tok(nh)] + pages(wd) + pages(wd) + pages(nh),
        out_specs=[blk, pl.BlockSpec(st_shape, lambda i, pt: (0, 0, 0, 0)), tok(wd)],
        scratch_shapes=[pltpu.VMEM(st_shape, F32)],
    )
    o, st, od = pl.pallas_call(
        functools.partial(_hgrn_decode_kernel, scale=HEAD_DIM ** -0.5, n_pages=n_pages),
        grid_spec=grid_spec,
        out_shape=[jax.ShapeDtypeStruct((batch, t, w), F32), jax.ShapeDtypeStruct(st_shape, F32),
                   jax.ShapeDtypeStruct((r, wd, 1), F32)],
        compiler_params=_params("arbitrary"), name="hgrn_prompt_fox_decode",
    )(page_table.reshape(-1), rows3(hq), rows3(hk), rows3(hv), rows3(hlf), q, k_new, v_new, lf_new,
      *([cache_kt] * n_pages), *([cache_vt] * n_pages), *([cache_lft] * n_pages))
    return o.reshape(n, w), st, od


def _hgrn_step_kernel(q_ref, k_ref, lf_ref, v_ref, s_ref, o_ref, so_ref):
    q = q_ref[...]
    k = k_ref[...]
    f = jnp.exp(lf_ref[...])
    v = v_ref[...]
    s = s_ref[...]
    so_ref[...] = f * s + k * v[None]
    o_ref[...] = jnp.sum((q * f) * s, axis=0) + jnp.sum(q * k, axis=0) * v


def _hgrn_step(q, k, lf, v, state_t, layer):
    h, dk, _, r = q.shape
    dv = v.shape[1]
    col = pl.BlockSpec((None, dk, 1, r), lambda i: (i, 0, 0, 0))
    rowv = pl.BlockSpec((None, dv, r), lambda i: (i, 0, 0))
    return pl.pallas_call(
        _hgrn_step_kernel, grid=(h,),
        in_specs=[col, col, col, rowv, pl.BlockSpec((None, None, dk, dv, r), lambda i: (layer, i, 0, 0, 0))],
        out_specs=[rowv, pl.BlockSpec((None, dk, dv, r), lambda i: (i, 0, 0, 0))],
        out_shape=[jax.ShapeDtypeStruct((h, dv, r), F32), jax.ShapeDtypeStruct((h, dk, dv, r), F32)],
        compiler_params=_params("parallel"), name="hgrn_step",
    )(q, k, lf, v, state_t)


def _head_mean_sq(y, width):
    w = y.shape[1]
    seg = ((_iota((w, w), 0) // width) == (_iota((w, w), 1) // width)).astype(BF16)
    sq = y * y
    hi = sq.astype(BF16)
    lo = (sq - hi.astype(F32)).astype(BF16)
    return (_dot(hi, seg) + _dot(lo, seg)) * (1.0 / width)


def _merge_kernel(fo_ref, ho_ref, gate_ref, x_ref, fgn_ref, hgn_ref, wo_ref, lnx_ref, wxq_ref, x1_ref, qx_ref):
    fo = fo_ref[...]
    ho = ho_ref[...]
    fw = fo.shape[1]
    fn = fo * lax.rsqrt(_head_mean_sq(fo, HEAD_DIM) + EPS) * fgn_ref[...]
    gate = gate_ref[...]
    hn = ho * lax.rsqrt(_head_mean_sq(ho, HEAD_DIM) + EPS) * hgn_ref[...] * (gate / (1.0 + jnp.exp(-gate)))
    y = _dot(fn.astype(BF16), wo_ref[:fw, :]) + _dot(hn.astype(BF16), wo_ref[fw:, :])
    x1 = x_ref[...] + y
    x1_ref[...] = x1
    qx_ref[...] = _dot(_rms(x1, lnx_ref[...]).astype(BF16), wxq_ref[...])


def _merge(fo, ho, gate, x, fgn, hgn, w_out, ln_x, w_xq, tm):
    n, d = x.shape
    xw = w_xq.shape[1]
    full = lambda a: pl.BlockSpec(a.shape, lambda i: (0,) * a.ndim)
    row = lambda c: pl.BlockSpec((tm, c), lambda i: (i, 0))
    return pl.pallas_call(
        _merge_kernel, grid=(n // tm,),
        in_specs=[row(fo.shape[1]), row(ho.shape[1]), row(gate.shape[1]), row(d),
                  full(fgn), full(hgn), full(w_out), full(ln_x), full(w_xq)],
        out_specs=[row(d), row(xw)],
        out_shape=[jax.ShapeDtypeStruct((n, d), F32), jax.ShapeDtypeStruct((n, xw), F32)],
        compiler_params=_params("parallel"), name="merge",
    )(fo, ho, gate, x, fgn, hgn, w_out, ln_x, w_xq)


def _memkv_kernel(m_ref, g_ref, wk_ref, wv_ref, k_ref, v_ref):
    m = _rms(m_ref[...], g_ref[...]).astype(BF16)
    k_ref[...] = _dot(m, wk_ref[...])
    v_ref[...] = _dot(m, wv_ref[...])


def _memkv(mem, g, wk, wv, tm):
    n, d = mem.shape
    xw = wk.shape[1]
    full = lambda a: pl.BlockSpec(a.shape, lambda i: (0,) * a.ndim)
    row = lambda c: pl.BlockSpec((tm, c), lambda i: (i, 0))
    return pl.pallas_call(
        _memkv_kernel, grid=(n // tm,),
        in_specs=[row(d), full(g), full(wk), full(wv)],
        out_specs=[row(xw), row(xw)],
        out_shape=[jax.ShapeDtypeStruct((n, xw), F32)] * 2,
        compiler_params=_params("parallel"), name="memkv",
    )(mem, g, wk, wv)


def _xattn_prompt_kernel(q_ref, mk_ref, mv_ref, o_ref, *, scale):
    heads = q_ref.shape[1] // X_HEAD_DIM
    for h in range(heads):
        sl = slice(h * X_HEAD_DIM, (h + 1) * X_HEAD_DIM)
        q = (q_ref[:, sl] * scale).astype(BF16)
        s = lax.dot_general(q, mk_ref[:, sl].astype(BF16), _NT, preferred_element_type=F32)
        e = jnp.exp(s - jnp.max(s, axis=1, keepdims=True))
        p = e / jnp.sum(e, axis=1, keepdims=True)
        o_ref[:, sl] = _dot(p.astype(BF16), mv_ref[:, sl].astype(BF16))


def _xattn_prompt(q, mk, mv, batch, tm):
    n, xw = q.shape
    t = n // batch
    nm = mk.shape[0] // batch
    nt = t // tm
    return pl.pallas_call(
        functools.partial(_xattn_prompt_kernel, scale=X_HEAD_DIM ** -0.5), grid=(batch, nt),
        in_specs=[pl.BlockSpec((tm, xw), lambda b, i: (b * nt + i, 0)),
                  pl.BlockSpec((nm, xw), lambda b, i: (b, 0)),
                  pl.BlockSpec((nm, xw), lambda b, i: (b, 0))],
        out_specs=pl.BlockSpec((tm, xw), lambda b, i: (b * nt + i, 0)),
        out_shape=jax.ShapeDtypeStruct((n, xw), F32),
        compiler_params=_params("parallel", "parallel"), name="xattn_prompt",
    )(q, mk, mv)


def _xattn_sample_kernel(qt_ref, mk_ref, mv_ref, o_ref, *, scale, heads):
    rb, hd, cols = qt_ref.shape
    nm = mk_ref.shape[1] // heads
    col = _iota((hd, cols), 1)
    erow = _iota((cols, hd), 0)
    valid = _iota((1, cols), 1) < heads
    for r in range(rb):
        qt = qt_ref[r] * scale
        s = jnp.zeros((nm, cols), F32)
        for h in range(heads):
            kh = mk_ref[r, pl.ds(h, nm, stride=heads), :].astype(BF16)
            s = s + _dot(kh, jnp.where(col == h, qt, 0.0).astype(BF16))
        e = jnp.exp(s - jnp.max(s, axis=0, keepdims=True))
        p = jnp.where(valid, e / jnp.sum(e, axis=0, keepdims=True), 0.0).astype(BF16)
        outs = []
        for h in range(heads):
            pe = _dot(p, (erow == h).astype(BF16))
            vh = mv_ref[r, pl.ds(h, nm, stride=heads), :]
            outs.append(jnp.sum((pe * vh).reshape(nm // 8, 8, hd), axis=0))
        o_ref[r] = jnp.sum(jnp.concatenate(outs, axis=1), axis=0, keepdims=True)


def _xattn_sample(q_t, mem_k, mem_v, layer, heads, rb):
    r, hd, cols = q_t.shape
    rows = mem_k.shape[2]
    mem = pl.BlockSpec((None, rb, rows, hd), lambda i: (layer, i, 0, 0))
    return pl.pallas_call(
        functools.partial(_xattn_sample_kernel, scale=X_HEAD_DIM ** -0.5, heads=heads), grid=(r // rb,),
        in_specs=[pl.BlockSpec((rb, hd, cols), lambda i: (i, 0, 0)), mem, mem],
        out_specs=pl.BlockSpec((rb, 1, heads * hd), lambda i: (i, 0, 0)),
        out_shape=jax.ShapeDtypeStruct((r, 1, heads * hd), F32),
        compiler_params=_params("parallel"), name="xattn_sample",
    )(q_t, mem_k, mem_v)


def _xo_peerq_kernel(x1_ref, ctx_ref, wxo_ref, lnf_ref, wq_ref, x2_ref, h_ref, pq_ref):
    x2 = x1_ref[...] + _dot(ctx_ref[...].astype(BF16), wxo_ref[...])
    x2_ref[...] = x2
    h = _rms(x2, lnf_ref[...]).astype(BF16)
    h_ref[...] = h
    pq_ref[...] = _dot(h, wq_ref[...]).astype(BF16)


def _xo_peerq(x1, ctx, w_xo, ln_ffn, peer_wq, tm):
    n, d = x1.shape
    qw = peer_wq.shape[1]
    full = lambda a: pl.BlockSpec(a.shape, lambda i: (0,) * a.ndim)
    row = lambda c: pl.BlockSpec((tm, c), lambda i: (i, 0))
    return pl.pallas_call(
        _xo_peerq_kernel, grid=(n // tm,),
        in_specs=[row(d), row(ctx.shape[1]), full(w_xo), full(ln_ffn), full(peer_wq)],
        out_specs=[row(d), row(d), row(qw)],
        out_shape=[jax.ShapeDtypeStruct((n, d), F32), jax.ShapeDtypeStruct((n, d), BF16),
                   jax.ShapeDtypeStruct((n, qw), BF16)],
        compiler_params=_params("parallel"), name="xo_peerq",
    )(x1, ctx, w_xo, ln_ffn, peer_wq)


def _top16(s):
    kk = s.shape[0]
    kio = _iota(s.shape, 0).astype(F32)
    vals, idxs = [], []
    for _ in range(PEER_TOPK):
        m = jnp.max(s, axis=0, keepdims=True)
        idx = jnp.min(jnp.where(s == m, kio, float(kk)), axis=0, keepdims=True)
        s = jnp.where(kio == idx, NEG_INF, s)
        vals.append(m)
        idxs.append(idx)
    return jnp.concatenate(vals, axis=0), jnp.concatenate(idxs, axis=0).astype(I32)


def _gather16(x, idx):
    lo, hi = x[:8], x[8:]
    out = []
    for part in (idx[:8], idx[8:]):
        low3 = part & 7
        out.append(jnp.where(part < 8, jnp.take_along_axis(lo, low3, axis=0), jnp.take_along_axis(hi, low3, axis=0)))
    return jnp.concatenate(out, axis=0)


def _route_kernel(pq_ref, sk_ref, i_ref, j_ref, g_ref, *, unrolled=False):
    half = sk_ref.shape[2]
    tt = min(2 * LANES, pq_ref.shape[0])
    k = PEER_TOPK

    def sub_tile(n, carry):
        r0 = pl.multiple_of(n * tt, tt)
        sv, si = [], []
        for p in range(2):
            q = pq_ref[pl.ds(r0, tt), p * half:(p + 1) * half]
            st = lax.dot_general(sk_ref[p], q, _NT, preferred_element_type=F32)
            v, i = _top16(st)
            sv.append(v)
            si.append(i)
        sub = 8
        rows8 = _iota((sub, tt), 0)
        cands = [sv[0][0:1, :] + sv[1]]
        for a in range(1, sub):
            cands.append(jnp.where(rows8 < k // (a + 1), sv[0][a:a + 1, :] + sv[1][:sub, :], NEG_INF))
        cands.append(sv[0][sub:, :] + sv[1][0:1, :])
        cand = jnp.concatenate(cands, axis=0)
        nrows = cand.shape[0]
        pio = _iota(cand.shape, 0).astype(F32)
        tv, tp = [], []
        for _ in range(k):
            m = jnp.max(cand, axis=0, keepdims=True)
            pos = jnp.min(jnp.where(cand == m, pio, float(nrows)), axis=0, keepdims=True)
            cand = jnp.where(pio == pos, NEG_INF, cand)
            tv.append(m)
            tp.append(pos)
        top = jnp.concatenate(tv, axis=0)
        e = jnp.exp(top - top[0:1, :])
        g_ref[:, pl.ds(r0, tt)] = e / jnp.sum(e, axis=0, keepdims=True)
        pos = jnp.concatenate(tp, axis=0).astype(I32)
        mid = pos - k
        first, last = pos < k, pos >= nrows - (k - sub)
        a_sel = jnp.where(first, 0, jnp.where(last, pos - (nrows - k), 1 + (mid >> 3)))
        b_sel = jnp.where(first, pos, jnp.where(last, 0, mid & (sub - 1)))
        i_ref[:, pl.ds(r0, tt)] = _gather16(si[0], a_sel)
        j_ref[:, pl.ds(r0, tt)] = _gather16(si[1], b_sel)
        return carry

    if unrolled:
        return [functools.partial(sub_tile, n, 0) for n in range(pq_ref.shape[0] // tt)]
    lax.fori_loop(0, pq_ref.shape[0] // tt, sub_tile, 0)


def _route(pq, subkeys, tt, first=0, tiles=None):
    hp, keys, half = subkeys.shape
    heads = hp // 2
    k = PEER_TOPK
    n = (pq.shape[0] // tt if tiles is None else tiles) * tt
    out = pl.BlockSpec((k, tt), lambda i, h: (h, i))
    return pl.pallas_call(
        _route_kernel, grid=(n // tt, heads),
        in_specs=[pl.BlockSpec((tt, 2 * half), lambda i, h: (first + i, h)),
                  pl.BlockSpec((2, keys, half), lambda i, h: (h, 0, 0))],
        out_specs=[out, out, out],
        out_shape=[jax.ShapeDtypeStruct((heads * k, n), I32), jax.ShapeDtypeStruct((heads * k, n), I32),
                   jax.ShapeDtypeStruct((heads * k, n), F32)],
        compiler_params=_params("parallel", "parallel"), name="peer_route",
    )(pq, subkeys)


def _peer_u_groups(h_ref, ut_ref, i_ref, j_ref, a_ref, ib, group=4):
    step = pl.program_id(1)

    @pl.when(step == 0)
    def _():
        a_ref[...] = jnp.zeros_like(a_ref)

    nj = j_ref.shape[1]

    def run(g0):
        isel = i_ref[...]
        jsel = j_ref[...]
        acc = a_ref[...]
        a_all = _dot(h_ref[...], ut_ref[:, g0 * nj:(g0 + group) * nj])
        for ii in range(group):
            a = a_all[:, ii * nj:(ii + 1) * nj]
            acc = jnp.where(isel == step * ib + g0 + ii, jnp.take_along_axis(a, jsel, axis=1), acc)
        a_ref[...] = acc

    return [functools.partial(run, g0) for g0 in range(0, ib, group)]


def _peer_u_kernel(h_ref, ut_ref, i_ref, j_ref, a_ref, *, ib):
    for run in _peer_u_groups(h_ref, ut_ref, i_ref, j_ref, a_ref, ib):
        run()


def _peer_u(h, ut, isel, jsel, tt, ib, first=0):
    d = h.shape[1]
    n, slots = isel.shape
    nblk = ut.shape[1] // (slots * ib)
    tok = lambda c: pl.BlockSpec((tt, c), lambda t, e: (t, 0))
    return pl.pallas_call(
        functools.partial(_peer_u_kernel, ib=ib), grid=(n // tt, nblk),
        in_specs=[pl.BlockSpec((tt, d), lambda t, e: (first + t, 0)),
                  pl.BlockSpec((d, slots * ib), lambda t, e: (0, e)), tok(slots), tok(slots)],
        out_specs=tok(slots), out_shape=jax.ShapeDtypeStruct((n, slots), F32),
        compiler_params=_params("parallel", "arbitrary"), name="peer_u",
    )(h, ut, isel, jsel)


def _route_peer_u_kernel(pq_ref, sk_ref, h_ref, ut_ref, isel_ref, jsel_ref, i_ref, j_ref, g_ref, a_ref, *, ib):
    matmuls = _peer_u_groups(h_ref, ut_ref, isel_ref, jsel_ref, a_ref, ib)
    tiles = _route_kernel(pq_ref, sk_ref, i_ref, j_ref, g_ref, unrolled=True)
    for n in range(max(len(matmuls), len(tiles))):
        if n < len(matmuls):
            matmuls[n]()
        if n < len(tiles):
            tiles[n]()


def _route_peer_u(pq, subkeys, h, ut, isel, jsel, tt, ib, route_first, act_first):
    d = h.shape[1]
    n, slots = isel.shape
    hp, keys, half = subkeys.shape
    heads = hp // 2
    assert ut.shape[1] == heads * slots * ib
    k = PEER_TOPK
    tok = lambda c: pl.BlockSpec((tt, c), lambda t, e: (t, 0))
    rout = pl.BlockSpec((k, tt), lambda t, e: (e, t))
    return pl.pallas_call(
        functools.partial(_route_peer_u_kernel, ib=ib), grid=(n // tt, heads),
        in_specs=[pl.BlockSpec((tt, 2 * half), lambda t, e: (route_first + t, e)),
                  pl.BlockSpec((2, keys, half), lambda t, e: (e, 0, 0)),
                  pl.BlockSpec((tt, d), lambda t, e: (act_first + t, 0)),
                  pl.BlockSpec((d, slots * ib), lambda t, e: (0, e)), tok(slots), tok(slots)],
        out_specs=[rout, rout, rout, tok(slots)],
        out_shape=[jax.ShapeDtypeStruct((heads * k, n), I32), jax.ShapeDtypeStruct((heads * k, n), I32),
                   jax.ShapeDtypeStruct((heads * k, n), F32), jax.ShapeDtypeStruct((n, slots), F32)],
        compiler_params=_params("parallel", "arbitrary"), name="route_peer_u",
    )(pq, subkeys, h, ut, isel, jsel)


def _peer_v_kernel(i_ref, j_ref, g_ref, a_ref, x_ref, v_ref, o_ref, z_sc, zs_sc, *, ib, stride):
    step = pl.program_id(1)
    tt, slots = i_ref.shape

    @pl.when(step == 0)
    def _():
        a = a_ref[...]
        z_sc[...] = g_ref[...] * (0.5 * a * (1.0 + lax.erf(a * (2.0 ** -0.5))))

        def scatter(t, carry):
            irow = i_ref[pl.ds(t, 1), :]
            jrow = j_ref[pl.ds(t, 1), :]
            zrow = z_sc[pl.ds(t, 1), :]
            io = _iota((slots, slots), 0)
            zit = jnp.where(irow == io, zrow, 0.0).astype(BF16)
            oht = jnp.where(jrow == io, 1.0, 0.0).astype(BF16)
            zs_sc[pl.ds(pl.multiple_of(t * stride, 8), slots), :] = lax.dot_general(
                zit, oht, _NT, preferred_element_type=F32)
            return carry

        lax.fori_loop(0, tt, scatter, 0, unroll=64)
        o_ref[...] = x_ref[...]

    base = step * ib
    zblk = jnp.concatenate(
        [zs_sc[pl.ds(base + ii, tt, stride=stride), :].astype(BF16) for ii in range(ib)], axis=1)
    o_ref[...] += _dot(zblk, v_ref[...])


def _peer_v(isel, jsel, gates, act, x, v, tt, ib):
    n, d = x.shape
    slots = isel.shape[1]
    nblk = v.shape[0] // (slots * ib)
    stride = slots + 8
    tok = lambda c: pl.BlockSpec((tt, c), lambda t, e: (t, 0))
    return pl.pallas_call(
        functools.partial(_peer_v_kernel, ib=ib, stride=stride), grid=(n // tt, nblk),
        in_specs=[tok(slots)] * 4 + [tok(d), pl.BlockSpec((slots * ib, d), lambda t, e: (e, 0))],
        out_specs=tok(d), out_shape=jax.ShapeDtypeStruct((n, d), F32),
        scratch_shapes=[pltpu.VMEM((tt, slots), F32), pltpu.VMEM((tt * stride, slots), F32)],
        compiler_params=_params("parallel", "arbitrary"), name="peer_v",
    )(isel, jsel, gates, act, x, v)


def _final_kernel(x_ref, g_ref, o_ref):
    o_ref[...] = _rms(x_ref[...], g_ref[...])


def _final_norm(x, g, tm):
    n, d = x.shape
    return pl.pallas_call(
        _final_kernel, grid=(n // tm,),
        in_specs=[pl.BlockSpec((tm, d), lambda i: (i, 0)), pl.BlockSpec((1, d), lambda i: (0, 0))],
        out_specs=pl.BlockSpec((tm, d), lambda i: (i, 0)), out_shape=jax.ShapeDtypeStruct((n, d), F32),
        compiler_params=_params("parallel"), name="final_norm",
    )(x, g)


def _tile(n, name):
    pref = TILE[name]
    return pref if n % pref == 0 else n


def _peer(x2, h3, pq, lw):
    n = x2.shape[0]
    tt, ib = _tile(n, "peer_u"), TILE["peer_blocks"]
    assert _tile(n, "route") == tt
    groups = TILE["peer_groups"] if n % (TILE["peer_groups"] * tt) == 0 else 1
    per = n // tt // groups
    routed = [_route(pq, lw["subkeys"], tt, 0, per)]
    acts = []
    for g in range(groups):
        isel_g, jsel_g = routed[g][0].T, routed[g][1].T
        if g + 1 < groups:
            it, jt, gt, act = _route_peer_u(pq, lw["subkeys"], h3, lw["ut"], isel_g, jsel_g, tt, ib,
                                            (g + 1) * per, g * per)
            routed.append((it, jt, gt))
        else:
            act = _peer_u(h3, lw["ut"], isel_g, jsel_g, tt, ib, g * per)
        acts.append(act)
    cat = lambda k: jnp.concatenate([r[k] for r in routed], axis=1).T
    return _peer_v(cat(0), cat(1), cat(2), jnp.concatenate(acts, axis=0), x2, lw["v"], _tile(n, "peer_v"), ib)


def kernel(x_prompt, x_sample, cache_k, cache_v, cache_logf, state_hg, cache_mem_k, cache_mem_v, page_table, mem_prompt, ln_mix, w_in, b_fox_f, hg_lb, fox_gn, hg_gn, w_out, ln_x, ln_mem, w_xq, w_xk, w_xv, w_xo, ln_ffn, peer_wq, peer_subkeys, peer_u, peer_v, ln_final):
    depth, d_model, _ = w_in.shape
    bp, seq, _ = x_prompt.shape
    bs = x_sample.shape[0]
    fox_heads = b_fox_f.shape[1]
    fox_w = fox_gn.shape[1]
    hg_w = hg_gn.shape[1]
    hg_heads = hg_w // HG_DK
    n_phys, page = cache_k.shape[1], cache_k.shape[2]
    n_mem = mem_prompt.shape[1]
    x_w = w_xq.shape[2]
    x_heads = x_w // X_HEAD_DIM
    p_heads = peer_subkeys.shape[1]
    p_keys, p_half = peer_subkeys.shape[3], peer_subkeys.shape[4]
    pairs = fox_w // LANES

    lb_all = jnp.cumsum(jax.nn.softmax(hg_lb.astype(F32), axis=0), axis=0)
    row = lambda a: a.reshape(1, -1).astype(F32)

    xp = x_prompt.reshape(bp * seq, d_model)
    xs = x_sample.reshape(bs, d_model)
    mem = mem_prompt.reshape(bp * n_mem, d_model)
    ck_t = cache_k.transpose(0, 1, 3, 4, 2).reshape(depth, n_phys, fox_w, page)
    cv_t = cache_v.transpose(0, 1, 3, 4, 2).reshape(depth, n_phys, fox_w, page)
    clf_t = cache_logf.transpose(0, 1, 3, 2)
    state_t = state_hg.transpose(0, 2, 3, 4, 1)
    memk = cache_mem_k.reshape(depth, bs, n_mem * x_heads, X_HEAD_DIM)
    memv = cache_mem_v.reshape(depth, bs, n_mem * x_heads, X_HEAD_DIM)

    outs = {k: [] for k in ("kp", "vp", "fp", "hp", "mkp", "mvp", "ks", "vs", "fs", "hs")}
    for l in range(depth):
        wl = w_in[l]
        c0 = 3 * fox_w
        lw = dict(
            w_main=jnp.concatenate([wl[:, :c0], wl[:, c0 + fox_heads:]], axis=1).astype(BF16),
            w_ff=jnp.pad(wl[:, c0:c0 + fox_heads], ((0, 0), (0, LANES - fox_heads))).astype(BF16),
            b_f=jnp.pad(row(b_fox_f[l]), ((0, 0), (0, LANES - fox_heads))),
            lb=row(lb_all[l] - lb_all[0]),
            subkeys=peer_subkeys[l].reshape(p_heads * 2, p_keys, p_half).astype(BF16),
            ut=peer_u[l].astype(BF16).T, v=peer_v[l].astype(BF16),
        )
        w_out_b, w_xq_b, w_xo_b = w_out[l].astype(BF16), w_xq[l].astype(BF16), w_xo[l].astype(BF16)
        peer_wq_b = peer_wq[l].astype(BF16)

        tm = _tile(bp * seq, "tokenwise")
        fq, fk, fv, flf, hq, hk, hv, hlf, hgate = _inproj(xp, row(ln_mix[l]), lw["w_main"], lw["w_ff"], lw["b_f"],
                                                           lw["lb"], tm, fox_heads)
        lf_t = flf.reshape(bp, seq, fox_heads).transpose(0, 2, 1).reshape(bp * fox_heads, seq)
        c_t = _cumsum_lanes(lf_t, _tile(seq, "cumsum"))
        ck = c_t.reshape(bp * pairs, 2, seq)
        cq = c_t.reshape(bp, pairs, 2, seq).transpose(1, 0, 3, 2).reshape(pairs, bp * seq, 2)
        fo = _fox_prompt(fq, fk, fv, cq, ck, bp, _tile(seq, "fox"))
        sq, sk, sv, slf, tq, tk, tv, tlf, tgate = _inproj(xs, row(ln_mix[l]), lw["w_main"], lw["w_ff"], lw["b_f"],
                                                           lw["lb"], bs, fox_heads)
        colm = lambda a: a.reshape(bs, a.shape[1], 1)
        decode_args = (page_table, colm(sq), colm(sk), colm(sv), colm(slf), ck_t, cv_t, clf_t, l)
        if seq // HG_CHUNK == bs:
            ho, st, fo_s = _hgrn_prompt_fox_decode(hq, hk, hv, hlf, bp, *decode_args)
        else:
            ho, st = _hgrn_prompt(hq, hk, hv, hlf, bp)
            fo_s = _fox_decode(*decode_args)
        fo_s = fo_s.reshape(bs, fox_w)
        tw = _tile(bp * seq, "tokenwise_wide")
        x1, qx = _merge(fo, ho, hgate, xp, row(fox_gn[l]), row(hg_gn[l]), w_out_b, row(ln_x[l]), w_xq_b, tw)
        mk, mv = _memkv(mem, row(ln_mem[l]), w_xk[l].astype(BF16), w_xv[l].astype(BF16), n_mem)
        ctx = _xattn_prompt(qx, mk, mv, bp, tm)
        x2, h3, pq = _xo_peerq(x1, ctx, w_xo_b, row(ln_ffn[l]), peer_wq_b, tw)
        xp = _peer(x2, h3, pq, lw)
        outs["kp"].append(fk.reshape(bp, seq, fox_heads, HEAD_DIM))
        outs["vp"].append(fv.reshape(bp, seq, fox_heads, HEAD_DIM))
        outs["fp"].append(flf.reshape(bp, seq, fox_heads))
        st5 = st.reshape(bp, pairs, 2, HG_DK, 2, HG_DK)
        s_heads = jnp.stack([st5[:, :, 0, :, 0, :], st5[:, :, 1, :, 1, :]], axis=2)
        outs["hp"].append(s_heads.reshape(bp, hg_heads, HG_DK, HG_DK).transpose(0, 1, 3, 2))
        outs["mkp"].append(mk.reshape(bp, n_mem, x_heads, X_HEAD_DIM))
        outs["mvp"].append(mv.reshape(bp, n_mem, x_heads, X_HEAD_DIM))

        lanes_r = lambda a: a.T.reshape(hg_heads, HG_DK, 1, bs)
        ho_t, s_new = _hgrn_step(lanes_r(tq), lanes_r(tk), lanes_r(tlf), tv.T.reshape(hg_heads, HG_DK, bs),
                                 state_t, l)
        x1s, qxs = _merge(fo_s, ho_t.reshape(hg_w, bs).T, tgate, xs, row(fox_gn[l]), row(hg_gn[l]), w_out_b,
                          row(ln_x[l]), w_xq_b, bs)
        qxs_t = jnp.pad(qxs.reshape(bs, x_heads, X_HEAD_DIM).transpose(0, 2, 1), ((0, 0), (0, 0), (0, 8 - x_heads)))
        ctx_s = _xattn_sample(qxs_t, memk, memv, l, x_heads, _tile(bs, "xattn_sample")).reshape(bs, x_w)
        x2s, h3s, pqs = _xo_peerq(x1s, ctx_s, w_xo_b, row(ln_ffn[l]), peer_wq_b, bs)
        xs = _peer(x2s, h3s, pqs, lw)
        outs["ks"].append(sk.reshape(bs, 1, fox_heads, HEAD_DIM))
        outs["vs"].append(sv.reshape(bs, 1, fox_heads, HEAD_DIM))
        outs["fs"].append(slf.reshape(bs, 1, fox_heads))
        outs["hs"].append(s_new)

    y_prompt = _final_norm(xp, row(ln_final), _tile(bp * seq, "tokenwise")).reshape(bp, seq, d_model)
    y_sample = _final_norm(xs, row(ln_final), bs).reshape(bs, 1, d_model)
    st_ = lambda k: jnp.stack(outs[k])
    return (y_prompt, y_sample, st_("kp"), st_("vp"), st_("fp"), st_("hp"), st_("mkp"), st_("mvp"),
            st_("ks"), st_("vs"), st_("fs"), st_("hs").transpose(0, 4, 1, 2, 3))
```
